```python
import jax, jax.numpy as jnp
from jax import lax
import numpy as np


D_MODEL = 2048
BATCH = 1
SEQ = 16384
DEPTH = 1

N_MEM = 256
EPS = 1e-6
RET_HEADS = 4
RET_DK = 256
RET_DV = 256
RET_CHUNK = 128
RET_THETA = 10000.0
SWA_HEADS = 16
SWA_KV_HEADS = 4
SWA_HD = 64
WINDOW = 128
ROPE_THETA = 500000.0
ROPE_DIM = SWA_HD // 4
X_HEADS = 4
X_HD = 128
D_FF = -(-8 * D_MODEL // (3 * 256)) * 256
IN_SPLITS = (RET_HEADS * RET_DK, RET_HEADS * RET_DK, RET_HEADS * RET_DV, RET_HEADS * RET_DV,
             SWA_HEADS * SWA_HD, SWA_KV_HEADS * SWA_HD, SWA_KV_HEADS * SWA_HD,
             D_MODEL, D_MODEL)
D_IN = sum(IN_SPLITS)
NEG = -1e30

kernel_name = 'hybrid_retention_swa_block'


def rmsnorm(x, g=None):
    xf = x.astype(jnp.float32)
    y = xf * lax.rsqrt(jnp.mean(xf * xf, axis=-1, keepdims=True) + EPS)
    if g is not None:
        y = y * g.astype(jnp.float32)
    return y.astype(x.dtype)


def to_heads(t, n_heads):
    b, s, _ = t.shape
    return t.reshape(b, s, n_heads, -1).transpose(0, 2, 1, 3)


def rope(x, pos, theta, rot_dim):
    half = rot_dim // 2
    inv = 1.0 / (theta ** (jnp.arange(half, dtype=jnp.float32) / half))
    ang = pos.astype(jnp.float32)[:, None, :, None] * inv
    cos, sin = jnp.cos(ang), jnp.sin(ang)
    xf = x[..., :rot_dim].astype(jnp.float32)
    x1, x2 = xf[..., :half], xf[..., half:]
    rot = jnp.concatenate([x1 * cos - x2 * sin, x2 * cos + x1 * sin], axis=-1)
    return jnp.concatenate([rot.astype(x.dtype), x[..., rot_dim:]], axis=-1)


def retention(q, k, v):
    b, h, s, dk = q.shape
    dv = v.shape[-1]
    c = RET_CHUNK
    n = s // c
    log_g = jnp.log(1.0 - jnp.power(2.0, -5.0 - jnp.arange(h, dtype=jnp.float32)))
    qf = q.astype(jnp.float32).reshape(b, h, n, c, dk)
    kf = (k.astype(jnp.float32) * (dk ** -0.5)).reshape(b, h, n, c, dk)
    vf = v.astype(jnp.float32).reshape(b, h, n, c, dv)
    idx = jnp.arange(c, dtype=jnp.float32)
    rel = idx[:, None] - idx[None, :]
    dmask = jnp.where(rel >= 0, jnp.exp(log_g[:, None, None] * jnp.maximum(rel, 0.0)), 0.0)
    scores = jnp.einsum('bhnid,bhnjd->bhnij', qf, kf) * dmask[:, None]
    inner = jnp.einsum('bhnij,bhnje->bhnie', scores, vf)
    q_dec = qf * jnp.exp(log_g[:, None] * (idx + 1.0))[:, None, :, None]
    k_dec = kf * jnp.exp(log_g[:, None] * (c - 1.0 - idx))[:, None, :, None]
    chunk_decay = jnp.exp(log_g * c)[None, :, None, None]

    def step(state, inp):
        qn, kn, vn = inp
        cross = jnp.einsum('bhid,bhde->bhie', qn, state)
        state = state * chunk_decay + jnp.einsum('bhjd,bhje->bhde', kn, vn)
        return state, cross

    xs = (jnp.moveaxis(q_dec, 2, 0), jnp.moveaxis(k_dec, 2, 0), jnp.moveaxis(vf, 2, 0))
    _, cross = lax.scan(step, jnp.zeros((b, h, dk, dv), jnp.float32), xs)
    out = inner + jnp.moveaxis(cross, 0, 2)
    return out.reshape(b, h, s, dv)


def sliding_window_attention(q, k, v, sinks):
    b, hq, s, hd = q.shape
    hkv = k.shape[1]
    g = hq // hkv
    w = WINDOW
    n = s // w
    qb = q.reshape(b, hkv, g, n, w, hd)

    def band(t):
        tb = t.reshape(b, hkv, n, w, hd)
        prev = jnp.pad(tb[:, :, :-1], ((0, 0), (0, 0), (1, 0), (0, 0), (0, 0)))
        return jnp.concatenate([prev, tb], axis=3)

    kb, vb = band(k), band(v)
    scores = jnp.einsum('bkgnqd,bknjd->bkgnqj', qb, kb).astype(jnp.float32) * (hd ** -0.5)
    qi = jnp.arange(w)[:, None]
    kj = jnp.arange(2 * w)[None, :]
    dist = qi + w - kj
    valid = (dist >= 0) & (dist < WINDOW)
    has_prev = (jnp.arange(n)[:, None, None] > 0) | (kj[None] >= w)
    mask = valid[None] & has_prev
    scores = jnp.where(mask, scores, NEG)
    sink = jnp.broadcast_to(sinks.astype(jnp.float32).reshape(1, hkv, g, 1, 1, 1), scores.shape[:-1] + (1,))
    probs = jax.nn.softmax(jnp.concatenate([scores, sink], axis=-1), axis=-1)[..., :-1]
    out = jnp.einsum('bkgnqj,bknjd->bkgnqd', probs.astype(v.dtype), vb)
    return out.reshape(b, hq, s, hd)


def memory_cross_attention(hn, memn, w_xq, w_xkv, w_xo):
    b, s, _ = hn.shape
    q = (hn @ w_xq).reshape(b, s, X_HEADS, X_HD)
    kv = (memn @ w_xkv).reshape(b, memn.shape[1], 2, X_HEADS, X_HD)
    k, v = kv[:, :, 0], kv[:, :, 1]
    scores = jnp.einsum('bshd,bmhd->bhsm', q, k).astype(jnp.float32) * (X_HD ** -0.5)
    probs = jax.nn.softmax(scores, axis=-1).astype(v.dtype)
    out = jnp.einsum('bhsm,bmhd->bshd', probs, v).reshape(b, s, X_HEADS * X_HD)
    return out @ w_xo


def setup_inputs(seed: int = 0) -> dict:
    key = jax.random.key(seed)
    ks = jax.random.split(key, 20)
    f32 = jnp.float32
    L = DEPTH

    def wt(k, shape, fan_in):
        return jax.random.normal(k, shape, f32) * (fan_in ** -0.5)

    def gain(k, shape):
        return 1.0 + 0.02 * jax.random.normal(k, shape, f32)

    return {
        'x': jax.random.normal(ks[0], (BATCH, SEQ, D_MODEL), f32),
        'mem': jax.random.normal(ks[1], (BATCH, N_MEM, D_MODEL), f32),
        'positions': jnp.broadcast_to(jnp.arange(SEQ, dtype=jnp.int32)[None, :], (BATCH, SEQ)),
        'g_mix': gain(ks[2], (L, D_MODEL)),
        'w_in': wt(ks[3], (L, D_MODEL, D_IN), D_MODEL),
        'w_up_ret': wt(ks[4], (L, RET_HEADS * RET_DV, D_MODEL), RET_HEADS * RET_DV),
        'w_up_swa': wt(ks[5], (L, SWA_HEADS * SWA_HD, D_MODEL), SWA_HEADS * SWA_HD),
        'sinks': 0.5 * jax.random.normal(ks[6], (L, SWA_HEADS), f32),
        'w_o': wt(ks[7], (L, D_MODEL, D_MODEL), D_MODEL),
        'g_x': gain(ks[8], (L, D_MODEL)),
        'g_mem': gain(ks[9], (L, D_MODEL)),
        'w_xq': wt(ks[10], (L, D_MODEL, X_HEADS * X_HD), D_MODEL),
        'w_xkv': wt(ks[11], (L, D_MODEL, 2 * X_HEADS * X_HD), D_MODEL),
        'w_xo': wt(ks[12], (L, X_HEADS * X_HD, D_MODEL), X_HEADS * X_HD),
        'g_ffn': gain(ks[13], (L, D_MODEL)),
        'w_ffn_gate': wt(ks[14], (L, D_MODEL, D_FF), D_MODEL),
        'w_ffn_up': wt(ks[15], (L, D_MODEL, D_FF), D_MODEL),
        'w_ffn_down': wt(ks[16], (L, D_FF, D_MODEL), D_FF),
        'g_final': gain(ks[17], (D_MODEL,)),
    }


def reference(x, mem, positions, g_mix, w_in, w_up_ret, w_up_swa, sinks, w_o, g_x, g_mem,
              w_xq, w_xkv, w_xo, g_ffn, w_ffn_gate, w_ffn_up, w_ffn_down, g_final):
    h = x
    offsets = [int(o) for o in np.cumsum(IN_SPLITS)[:-1]]
    for l in range(DEPTH):
        n1 = rmsnorm(h, g_mix[l])
        proj = n1 @ w_in[l]
        q_r, k_r, v_r, g_r, q_s, k_s, v_s, gate_r, gate_s = jnp.split(proj, offsets, axis=-1)
        qr = rope(to_heads(q_r, RET_HEADS), positions, RET_THETA, RET_DK)
        kr = rope(to_heads(k_r, RET_HEADS), positions, RET_THETA, RET_DK)
        yr = retention(qr, kr, to_heads(v_r, RET_HEADS))
        yr = rmsnorm(yr.transpose(0, 2, 1, 3))
        b, s = yr.shape[0], yr.shape[1]
        yr = (yr.reshape(b, s, RET_HEADS * RET_DV) * jax.nn.silu(g_r.astype(jnp.float32))).astype(h.dtype)
        qs = rope(to_heads(q_s, SWA_HEADS), positions, ROPE_THETA, ROPE_DIM)
        ksw = rope(to_heads(k_s, SWA_KV_HEADS), positions, ROPE_THETA, ROPE_DIM)
        ys = sliding_window_attention(qs, ksw, to_heads(v_s, SWA_KV_HEADS), sinks[l])
        ys = ys.transpose(0, 2, 1, 3).reshape(b, s, SWA_HEADS * SWA_HD)
        merged = jax.nn.sigmoid(gate_r) * (yr @ w_up_ret[l]) + jax.nn.sigmoid(gate_s) * (ys @ w_up_swa[l])
        h = h + merged @ w_o[l]
        h = h + memory_cross_attention(rmsnorm(h, g_x[l]), rmsnorm(mem, g_mem[l]), w_xq[l], w_xkv[l], w_xo[l])
        n2 = rmsnorm(h, g_ffn[l])
        h = h + (jax.nn.silu(n2 @ w_ffn_gate[l]) * (n2 @ w_ffn_up[l])) @ w_ffn_down[l]
    return rmsnorm(h, g_final)
```

```python
import functools

import jax
import jax.numpy as jnp
import numpy as np
from jax import lax
from jax.experimental import pallas as pl
from jax.experimental.pallas import tpu as pltpu

F32 = jnp.float32
BF16 = jnp.bfloat16

D_MODEL = 2048
SEQ = 16384
N_MEM = 256
EPS = 1e-6
RET_HEADS = 4
RET_DK = 256
RET_DV = 256
RET_THETA = 10000.0
SWA_HEADS = 16
SWA_KV_HEADS = 4
SWA_HD = 64
WINDOW = 128
ROPE_THETA = 500000.0
ROPE_DIM = SWA_HD // 4
X_HEADS = 4
X_HD = 128
D_FF = -(-8 * D_MODEL // (3 * 256)) * 256
IN_SPLITS = (RET_HEADS * RET_DK, RET_HEADS * RET_DK, RET_HEADS * RET_DV, RET_HEADS * RET_DV,
             SWA_HEADS * SWA_HD, SWA_KV_HEADS * SWA_HD, SWA_KV_HEADS * SWA_HD, D_MODEL, D_MODEL)
D_IN = sum(IN_SPLITS)
IN_ORDER = (7, 8, 0, 1, 2, 3, 4, 5, 6)
_REF_OFFSETS = tuple(int(o) for o in np.cumsum((0,) + IN_SPLITS[:-1]))
_PERM_STARTS = np.cumsum((0,) + tuple(IN_SPLITS[s] for s in IN_ORDER[:-1]))
IN_OFFSETS = tuple(int(_PERM_STARTS[IN_ORDER.index(s)]) for s in range(len(IN_SPLITS)))
NEG = -1e30

LANES = 128
VMEM_LIMIT = 56 * 1024 * 1024

RET_CHUNK = 256
RET_ROWS = 1024
ROPE_ROWS = 1024
PROJ_TM, PROJ_TN = 512, 512
MERGE_TM = 256
FFN_TM, FFN_TF = 512, 512


def _sigmoid(x):
    return 1.0 / (1.0 + jnp.exp(-x))


def _rms(x):
    return x * lax.rsqrt(jnp.mean(x * x, axis=-1, keepdims=True) + EPS)


def _dot(a, b):
    return jnp.dot(a, b, preferred_element_type=F32)


def _dot_nt(a, b):
    return lax.dot_general(a, b, (((1,), (1,)), ((), ())), preferred_element_type=F32)


def _dot_tn(a, b):
    return lax.dot_general(a, b, (((0,), (0,)), ((), ())), preferred_element_type=F32)


def _params(*sem):
    return pltpu.CompilerParams(dimension_semantics=sem, vmem_limit_bytes=VMEM_LIMIT)


def _resident(shape):
    nd = len(shape)
    return pl.BlockSpec(shape, lambda *_: (0,) * nd, pipeline_mode=pl.Buffered(1))


def _rope_tables_kernel(pos_ref, invr_ref, invs_ref, m1_ref, m2_ref,
                        rc_ref, rs_ref, sc_ref, s1_ref, s2_ref):
    pos = pos_ref[...].astype(F32)
    ang_r = pos * invr_ref[...]
    rc_ref[...] = jnp.cos(ang_r)
    rs_ref[...] = jnp.sin(ang_r)
    ang_s = pos * invs_ref[...]
    sn = jnp.sin(ang_s)
    sc_ref[...] = jnp.cos(ang_s)
    s1_ref[...] = -sn * m1_ref[...]
    s2_ref[...] = sn * m2_ref[...]


def _rope_tables(positions):
    pos = positions.reshape(SEQ, 1)
    half_r = RET_DK // 2
    inv_r = 1.0 / (RET_THETA ** (jnp.arange(half_r, dtype=F32) / half_r))
    half_s = ROPE_DIM // 2
    inv_s = 1.0 / (ROPE_THETA ** (jnp.arange(half_s, dtype=F32) / half_s))
    d = np.arange(LANES) % SWA_HD
    inv_s_lanes = jnp.where(d < ROPE_DIM, inv_s[d % half_s], 0.0).astype(F32)
    m1 = jnp.asarray((d < half_s).astype(np.float32))
    m2 = jnp.asarray(((d >= half_s) & (d < ROPE_DIM)).astype(np.float32))
    row = lambda v: v.reshape(1, LANES)
    tab = jax.ShapeDtypeStruct((SEQ, LANES), F32)
    vec = pl.BlockSpec((1, LANES), lambda i: (0, 0))
    blk = pl.BlockSpec((ROPE_ROWS, LANES), lambda i: (i, 0))
    return pl.pallas_call(
        _rope_tables_kernel,
        grid=(SEQ // ROPE_ROWS,),
        in_specs=[pl.BlockSpec((ROPE_ROWS, 1), lambda i: (i, 0)), vec, vec, vec, vec],
        out_specs=[blk] * 5,
        out_shape=[tab] * 5,
        compiler_params=_params("parallel"),
    )(pos, row(inv_r), row(inv_s_lanes), row(m1), row(m2))


def _in_proj_kernel(x_ref, g_ref, w_ref, o_ref, n_ref):
    @pl.when(pl.program_id(1) == 0)
    def _():
        n_ref[...] = (_rms(x_ref[...]) * g_ref[...]).astype(BF16)

    o_ref[...] = _dot(n_ref[...], w_ref[...]).astype(o_ref.dtype)


def _in_proj(x2d, g, w):
    tm, tn = PROJ_TM, PROJ_TN
    return pl.pallas_call(
        _in_proj_kernel,
        grid=(SEQ // tm, D_IN // tn),
        in_specs=[pl.BlockSpec((tm, D_MODEL), lambda i, j: (i, 0)),
                  pl.BlockSpec((1, D_MODEL), lambda i, j: (0, 0)),
                  pl.BlockSpec((D_MODEL, tn), lambda i, j: (0, j))],
        out_specs=pl.BlockSpec((tm, tn), lambda i, j: (i, j)),
        out_shape=jax.ShapeDtypeStruct((SEQ, D_IN), BF16),
        scratch_shapes=[pltpu.VMEM((tm, D_MODEL), BF16)],
        compiler_params=_params("parallel", "arbitrary"),
    )(x2d, g.reshape(1, D_MODEL), w)


def _retention_kernel(logg_ref, q_ref, k_ref, v_ref, g_ref, cos_ref, sin_ref, o_ref,
                      state_ref, dmask_ref):
    c_rows = RET_CHUNK
    lg = logg_ref[pl.program_id(0)]

    @pl.when(pl.program_id(1) == 0)
    def _():
        state_ref[...] = jnp.zeros_like(state_ref)
        i = lax.broadcasted_iota(jnp.int32, (c_rows, c_rows), 0)
        j = lax.broadcasted_iota(jnp.int32, (c_rows, c_rows), 1)
        rel = (i - j).astype(F32)
        dmask_ref[...] = jnp.where(rel >= 0, jnp.exp(lg * jnp.maximum(rel, 0.0)), 0.0)

    idx = lax.broadcasted_iota(jnp.int32, (c_rows, 1), 0).astype(F32)
    q_dec = jnp.exp(lg * (idx + 1.0))
    k_dec = jnp.exp(lg * (c_rows - 1.0 - idx))
    chunk_decay = jnp.exp(jnp.full((1, RET_DV), lg * c_rows, F32))
    k_scale = RET_DK ** -0.5
    half = RET_DK // 2

    def rope(t, cos, sin):
        t1, t2 = t[:, :half], t[:, half:]
        return jnp.concatenate([t1 * cos - t2 * sin, t2 * cos + t1 * sin], axis=-1)

    def body(c, carry):
        r = pl.ds(pl.multiple_of(c * c_rows, c_rows), c_rows)
        cos, sin = cos_ref[r, :], sin_ref[r, :]
        q = rope(q_ref[r, :].astype(F32), cos, sin)
        k = rope(k_ref[r, :].astype(F32), cos, sin) * k_scale
        v = v_ref[r, :]
        scores = _dot_nt(q.astype(BF16), k.astype(BF16)) * dmask_ref[...]
        inner = _dot(scores.astype(BF16), v)
        state = state_ref[...]
        cross = _dot((q * q_dec).astype(BF16), state.astype(BF16))
        state_ref[...] = state * chunk_decay + _dot_tn((k * k_dec).astype(BF16), v)
        y = _rms(inner + cross)
        g = g_ref[r, :].astype(F32)
        o_ref[r, :] = (y * (g * _sigmoid(g))).astype(o_ref.dtype)
        return carry

    lax.fori_loop(0, RET_ROWS // c_rows, body, 0)


def _retention(proj, cos_r, sin_r):
    log_g = jnp.log(1.0 - jnp.power(2.0, -5.0 - jnp.arange(RET_HEADS, dtype=F32)))
    rows = RET_ROWS
    col = lambda split: IN_OFFSETS[split] // RET_DK
    head_blk = lambda split: pl.BlockSpec((rows, RET_DK), lambda h, n, s=split: (n, col(s) + h))
    tab = pl.BlockSpec((rows, LANES), lambda h, n: (n, 0))
    return pl.pallas_call(
        _retention_kernel,
        grid=(RET_HEADS, SEQ // rows),
        in_specs=[pl.BlockSpec(memory_space=pltpu.SMEM),
                  head_blk(0), head_blk(1), head_blk(2), head_blk(3), tab, tab],
        out_specs=pl.BlockSpec((rows, RET_DV), lambda h, n: (n, h)),
        out_shape=jax.ShapeDtypeStruct((SEQ, RET_HEADS * RET_DV), BF16),
        scratch_shapes=[pltpu.VMEM((RET_DK, RET_DV), F32), pltpu.VMEM((RET_CHUNK, RET_CHUNK), F32)],
        compiler_params=_params("parallel", "arbitrary"),
    )(log_g, proj, proj, proj, proj, cos_r, sin_r)


def _swa_rope(x, c, s1, s2):
    half = ROPE_DIM // 2
    outs = []
    for t in range(x.shape[1] // LANES):
        xt = x[:, t * LANES:(t + 1) * LANES]
        outs.append(xt * c + pltpu.roll(xt, LANES - half, 1) * s1 + pltpu.roll(xt, half, 1) * s2)
    return jnp.concatenate(outs, axis=-1)


def _swa_kernel(sinks_ref, q_ref, kc_ref, vc_ref, kp_ref, vp_ref,
                c_ref, s1_ref, s2_ref, cp_ref, s1p_ref, s2p_ref, o_ref):
    n = pl.program_id(0)
    w = WINDOW
    group = SWA_HEADS // SWA_KV_HEADS
    q = _swa_rope(q_ref[...].astype(F32), c_ref[...], s1_ref[...], s2_ref[...]) * (SWA_HD ** -0.5)
    q = q.astype(BF16)
    kc = _swa_rope(kc_ref[...].astype(F32), c_ref[...], s1_ref[...], s2_ref[...])
    kp = _swa_rope(kp_ref[...].astype(F32), cp_ref[...], s1p_ref[...], s2p_ref[...])
    k = jnp.concatenate([kp, kc], axis=0).astype(BF16)
    v = jnp.concatenate([vp_ref[...], vc_ref[...]], axis=0)

    qi = lax.broadcasted_iota(jnp.int32, (w, 2 * w), 0)
    kj = lax.broadcasted_iota(jnp.int32, (w, 2 * w), 1)
    dist = qi + w - kj
    mask = (dist >= 0) & (dist < WINDOW) & ((n > 0) | (kj >= w))
    mask = jnp.concatenate([mask] * group, axis=0)

    outs = []
    for kh in range(SWA_KV_HEADS):
        k_h = k[:, kh * SWA_HD:(kh + 1) * SWA_HD]
        v_h = v[:, kh * SWA_HD:(kh + 1) * SWA_HD]
        heads = [kh * group + g for g in range(group)]
        q_g = jnp.concatenate([q[:, h * SWA_HD:(h + 1) * SWA_HD] for h in heads], axis=0)
        sink = jnp.concatenate([jnp.full((w, 1), sinks_ref[h], F32) for h in heads], axis=0)
        s = jnp.where(mask, _dot_nt(q_g, k_h), NEG)
        m = jnp.maximum(jnp.max(s, axis=-1, keepdims=True), sink)
        e = jnp.exp(s - m)
        denom = jnp.sum(e, axis=-1, keepdims=True) + jnp.exp(sink - m)
        o = _dot((e / denom).astype(BF16), v_h)
        outs.extend(o[g * w:(g + 1) * w] for g in range(group))
    o_ref[...] = jnp.concatenate(outs, axis=-1).astype(o_ref.dtype)


def _swa(proj, sinks, cos_s, s1, s2):
    w = WINDOW
    nq, nkv = SWA_HEADS * SWA_HD, SWA_KV_HEADS * SWA_HD
    cur = lambda i: (i, 0)
    prev = lambda i: (jnp.maximum(i - 1, 0), 0)
    tab = lambda im: pl.BlockSpec((w, LANES), im)
    kv = lambda split, im: pl.BlockSpec(
        (w, nkv), lambda i, s=split, f=im: (f(i)[0], IN_OFFSETS[s] // nkv))
    return pl.pallas_call(
        _swa_kernel,
        grid=(SEQ // w,),
        in_specs=[pl.BlockSpec(memory_space=pltpu.SMEM),
                  pl.BlockSpec((w, nq), lambda i: (i, IN_OFFSETS[4] // nq)),
                  kv(5, cur), kv(6, cur), kv(5, prev), kv(6, prev),
                  tab(cur), tab(cur), tab(cur), tab(prev), tab(prev), tab(prev)],
        out_specs=pl.BlockSpec((w, nq), cur),
        out_shape=jax.ShapeDtypeStruct((SEQ, nq), BF16),
        compiler_params=_params("parallel"),
    )(sinks, proj, proj, proj, proj, proj, cos_s, s1, s2, cos_s, s1, s2)


def _mem_kv_kernel(mem_ref, g_ref, w_ref, o_ref):
    memn = (_rms(mem_ref[...]) * g_ref[...]).astype(BF16)
    o_ref[...] = _dot(memn, w_ref[...]).astype(o_ref.dtype)


def _mem_kv(mem2d, g, w):
    n_out = 2 * X_HEADS * X_HD
    return pl.pallas_call(
        _mem_kv_kernel,
        out_shape=jax.ShapeDtypeStruct((N_MEM, n_out), BF16),
        compiler_params=pltpu.CompilerParams(vmem_limit_bytes=VMEM_LIMIT),
    )(mem2d, g.reshape(1, D_MODEL), w)


def _merge_kernel(x_ref, yr_ref, ys_ref, gr_ref, gs_ref, kv_ref,
                  wur_ref, wus_ref, wo_ref, gx_ref, wxq_ref, wxo_ref, gf_ref,
                  h_ref, n_ref):
    up_r = _dot(yr_ref[...], wur_ref[...])
    up_s = _dot(ys_ref[...], wus_ref[...])
    merged = (_sigmoid(gr_ref[...].astype(F32)) * up_r + _sigmoid(gs_ref[...].astype(F32)) * up_s)
    h1 = x_ref[...] + _dot(merged.astype(BF16), wo_ref[...])

    nx = (_rms(h1) * gx_ref[...]).astype(BF16)
    qx = _dot(nx, wxq_ref[...]).astype(BF16)
    kv = kv_ref[...]
    outs = []
    for h in range(X_HEADS):
        q_h = qx[:, h * X_HD:(h + 1) * X_HD]
        k_h = kv[:, h * X_HD:(h + 1) * X_HD]
        v_h = kv[:, (X_HEADS + h) * X_HD:(X_HEADS + h + 1) * X_HD]
        s = _dot_nt(q_h, k_h) * (X_HD ** -0.5)
        e = jnp.exp(s - jnp.max(s, axis=-1, keepdims=True))
        p = e / jnp.sum(e, axis=-1, keepdims=True)
        outs.append(_dot(p.astype(BF16), v_h))
    att = jnp.concatenate(outs, axis=-1).astype(BF16)
    h2 = h1 + _dot(att, wxo_ref[...])
    h_ref[...] = h2
    n_ref[...] = (_rms(h2) * gf_ref[...]).astype(BF16)


def _merge(x2d, yr, ys, proj, kv, w_up_ret, w_up_swa, w_o, g_x, w_xq, w_xo, g_ffn):
    tm = MERGE_TM
    rows = lambda width, col=0: pl.BlockSpec((tm, width), lambda i, c=col: (i, c))
    n_x = X_HEADS * X_HD
    return pl.pallas_call(
        _merge_kernel,
        grid=(SEQ // tm,),
        in_specs=[rows(D_MODEL), rows(RET_HEADS * RET_DV), rows(SWA_HEADS * SWA_HD),
                  rows(D_MODEL, IN_OFFSETS[7] // D_MODEL), rows(D_MODEL, IN_OFFSETS[8] // D_MODEL),
                  _resident((N_MEM, 2 * n_x)),
                  _resident((RET_HEADS * RET_DV, D_MODEL)), _resident((SWA_HEADS * SWA_HD, D_MODEL)),
                  _resident((D_MODEL, D_MODEL)), _resident((1, D_MODEL)),
                  _resident((D_MODEL, n_x)), _resident((n_x, D_MODEL)), _resident((1, D_MODEL))],
        out_specs=[rows(D_MODEL), rows(D_MODEL)],
        out_shape=[jax.ShapeDtypeStruct((SEQ, D_MODEL), F32),
                   jax.ShapeDtypeStruct((SEQ, D_MODEL), BF16)],
        compiler_params=_params("parallel"),
    )(x2d, yr, ys, proj, proj, kv, w_up_ret, w_up_swa, w_o, g_x.reshape(1, D_MODEL),
      w_xq, w_xo, g_ffn.reshape(1, D_MODEL))


def _ffn_kernel(n_ref, h_ref, wg_ref, wu_ref, wd_ref, gf_ref, o_ref, acc_ref):
    f = pl.program_id(1)

    @pl.when(f == 0)
    def _():
        acc_ref[...] = jnp.zeros_like(acc_ref)

    n2 = n_ref[...]
    a = _dot(n2, wg_ref[...])
    b = _dot(n2, wu_ref[...])
    acc_ref[...] += _dot((a * _sigmoid(a) * b).astype(BF16), wd_ref[...])

    @pl.when(f == pl.num_programs(1) - 1)
    def _():
        o_ref[...] = _rms(h_ref[...] + acc_ref[...]) * gf_ref[...]


def _ffn(n2, h2, w_gate, w_up, w_down, g_final):
    tm, tf = FFN_TM, FFN_TF
    return pl.pallas_call(
        _ffn_kernel,
        grid=(SEQ // tm, D_FF // tf),
        in_specs=[pl.BlockSpec((tm, D_MODEL), lambda i, f: (i, 0)),
                  pl.BlockSpec((tm, D_MODEL), lambda i, f: (i, 0)),
                  pl.BlockSpec((D_MODEL, tf), lambda i, f: (0, f)),
                  pl.BlockSpec((D_MODEL, tf), lambda i, f: (0, f)),
                  pl.BlockSpec((tf, D_MODEL), lambda i, f: (f, 0)),
                  pl.BlockSpec((1, D_MODEL), lambda i, f: (0, 0))],
        out_specs=pl.BlockSpec((tm, D_MODEL), lambda i, f: (i, 0)),
        out_shape=jax.ShapeDtypeStruct((SEQ, D_MODEL), F32),
        scratch_shapes=[pltpu.VMEM((tm, D_MODEL), F32)],
        compiler_params=_params("parallel", "arbitrary"),
    )(n2, h2, w_gate, w_up, w_down, g_final.reshape(1, D_MODEL))


def kernel(x, mem, positions, g_mix, w_in, w_up_ret, w_up_swa, sinks, w_o, g_x, g_mem,
           w_xq, w_xkv, w_xo, g_ffn, w_ffn_gate, w_ffn_up, w_ffn_down, g_final):
    assert x.shape == (1, SEQ, D_MODEL) and mem.shape == (1, N_MEM, D_MODEL)
    assert w_in.shape == (1, D_MODEL, D_IN)
    bf = lambda w: w[0].astype(BF16)
    x2d = x.reshape(SEQ, D_MODEL)

    w_in_perm = jnp.concatenate(
        [w_in[0, :, _REF_OFFSETS[s]:_REF_OFFSETS[s] + IN_SPLITS[s]].astype(BF16) for s in IN_ORDER], axis=1)

    cos_r, sin_r, cos_s, s1, s2 = _rope_tables(positions)
    proj = _in_proj(x2d, g_mix[0], w_in_perm)
    yr = _retention(proj, cos_r, sin_r)
    ys = _swa(proj, sinks[0], cos_s, s1, s2)
    kv = _mem_kv(mem.reshape(N_MEM, D_MODEL), g_mem[0], bf(w_xkv))
    h2, n2 = _merge(x2d, yr, ys, proj, kv, bf(w_up_ret), bf(w_up_swa), bf(w_o), g_x[0],
                    bf(w_xq), bf(w_xo), g_ffn[0])
    out = _ffn(n2, h2, bf(w_ffn_gate), bf(w_ffn_up), bf(w_ffn_down), g_final)
    return out.reshape(1, SEQ, D_MODEL)
```

```python
import jax
import jax.numpy as jnp
import numpy as np
from jax import lax
from jax.experimental import pallas as pl
from jax.experimental.pallas import tpu as pltpu

F32 = jnp.float32
BF16 = jnp.bfloat16

D_MODEL = 2048
SEQ = 16384
N_MEM = 256
EPS = 1e-6
RET_HEADS = 4
RET_DK = 256
RET_DV = 256
RET_THETA = 10000.0
SWA_HEADS = 16
SWA_KV_HEADS = 4
SWA_HD = 64
WINDOW = 128
ROPE_THETA = 500000.0
ROPE_DIM = SWA_HD // 4
X_HEADS = 4
X_HD = 128
D_FF = -(-8 * D_MODEL // (3 * 256)) * 256
IN_SPLITS = (RET_HEADS * RET_DK, RET_HEADS * RET_DK, RET_HEADS * RET_DV, RET_HEADS * RET_DV,
             SWA_HEADS * SWA_HD, SWA_KV_HEADS * SWA_HD, SWA_KV_HEADS * SWA_HD, D_MODEL, D_MODEL)
D_IN = sum(IN_SPLITS)
IN_ORDER = (7, 8, 0, 1, 2, 3, 4, 5, 6)
_REF_OFFSETS = tuple(int(o) for o in np.cumsum((0,) + IN_SPLITS[:-1]))
_PERM_STARTS = np.cumsum((0,) + tuple(IN_SPLITS[s] for s in IN_ORDER[:-1]))
IN_OFFSETS = tuple(int(_PERM_STARTS[IN_ORDER.index(s)]) for s in range(len(IN_SPLITS)))
Q_R, K_R, V_R, G_R, Q_S, K_S, V_S, GATE_R, GATE_S = range(9)
NEG = -1e30

LANES = 128
VMEM_LIMIT = 56 * 1024 * 1024

RET_CHUNK = 256
RET_ROWS = 1024
ROPE_ROWS = 1024
PROJ_TM, PROJ_TN = 1024, 512
SWA_ROWS = 256
MERGE_TM = 256
FFN_TM, FFN_TF = 1024, 256


def _sigmoid(x):
    return 1.0 / (1.0 + jnp.exp(-x))


def _rms(x):
    return x * lax.rsqrt(jnp.mean(x * x, axis=-1, keepdims=True) + EPS)


def _dot(a, b):
    return jnp.dot(a, b, preferred_element_type=F32)


def _dot_nt(a, b):
    return lax.dot_general(a, b, (((1,), (1,)), ((), ())), preferred_element_type=F32)


def _dot_tn(a, b):
    return lax.dot_general(a, b, (((0,), (0,)), ((), ())), preferred_element_type=F32)


def _params(*sem):
    return pltpu.CompilerParams(dimension_semantics=sem, vmem_limit_bytes=VMEM_LIMIT)


def _resident(shape):
    nd = len(shape)
    return pl.BlockSpec(shape, lambda *_: (0,) * nd, pipeline_mode=pl.Buffered(1))


def _col_blocks(w, width):
    k, n = w.shape
    return w.reshape(k, n // width, width).transpose(1, 0, 2)


def _rope_tables_kernel(pos_ref, invr_ref, invs_ref, m1_ref, m2_ref,
                        rc_ref, rs_ref, sc_ref, s1_ref, s2_ref):
    pos = pos_ref[...].astype(F32)
    ang_r = pos * invr_ref[...]
    rc_ref[...] = jnp.cos(ang_r)
    rs_ref[...] = jnp.sin(ang_r)
    ang_s = pos * invs_ref[...]
    sn = jnp.sin(ang_s)
    sc_ref[...] = jnp.cos(ang_s)
    s1_ref[...] = -sn * m1_ref[...]
    s2_ref[...] = sn * m2_ref[...]


def _rope_tables(positions):
    pos = positions.reshape(SEQ, 1)
    half_r = RET_DK // 2
    inv_r = 1.0 / (RET_THETA ** (jnp.arange(half_r, dtype=F32) / half_r))
    half_s = ROPE_DIM // 2
    inv_s = 1.0 / (ROPE_THETA ** (jnp.arange(half_s, dtype=F32) / half_s))
    d = np.arange(LANES) % SWA_HD
    inv_s_lanes = jnp.where(d < ROPE_DIM, inv_s[d % half_s], 0.0).astype(F32)
    m1 = jnp.asarray((d < half_s).astype(np.float32))
    m2 = jnp.asarray(((d >= half_s) & (d < ROPE_DIM)).astype(np.float32))
    row = lambda v: v.reshape(1, LANES)
    tab = jax.ShapeDtypeStruct((SEQ, LANES), F32)
    vec = pl.BlockSpec((1, LANES), lambda i: (0, 0))
    blk = pl.BlockSpec((ROPE_ROWS, LANES), lambda i: (i, 0))
    return pl.pallas_call(
        _rope_tables_kernel,
        grid=(SEQ // ROPE_ROWS,),
        in_specs=[pl.BlockSpec((ROPE_ROWS, 1), lambda i: (i, 0)), vec, vec, vec, vec],
        out_specs=[blk] * 5,
        out_shape=[tab] * 5,
        compiler_params=_params("parallel"),
        name="rope_tables",
    )(pos, row(inv_r), row(inv_s_lanes), row(m1), row(m2))


def _in_proj_kernel(x_ref, g_ref, w_ref, rc_ref, rs_ref, sc_ref, s1_ref, s2_ref, o_ref, n_ref):
    tn = PROJ_TN
    j = pl.program_id(1)
    blk = lambda split: IN_OFFSETS[split] // tn

    @pl.when(j == 0)
    def _():
        n_ref[...] = (_rms(x_ref[...]) * g_ref[...]).astype(BF16)

    acc = _dot(n_ref[...], w_ref[0])
    is_ret = (j >= blk(Q_R)) & (j < blk(V_R))
    is_qs = (j >= blk(Q_S)) & (j < blk(K_S))
    is_kvs = j == blk(K_S)

    def swa_rope(xt):
        half = ROPE_DIM // 2
        return (xt * sc_ref[...] + pltpu.roll(xt, LANES - half, 1) * s1_ref[...]
                + pltpu.roll(xt, half, 1) * s2_ref[...])

    @pl.when(is_ret)
    def _():
        scale = jnp.where(j >= blk(K_R), RET_DK ** -0.5, 1.0).astype(F32)
        cos, sin = rc_ref[...] * scale, rs_ref[...] * scale
        half = RET_DK // 2
        for h in range(tn // RET_DK):
            a1 = acc[:, h * RET_DK:h * RET_DK + half]
            a2 = acc[:, h * RET_DK + half:(h + 1) * RET_DK]
            o_ref[:, h * RET_DK:h * RET_DK + half] = (a1 * cos - a2 * sin).astype(BF16)
            o_ref[:, h * RET_DK + half:(h + 1) * RET_DK] = (a2 * cos + a1 * sin).astype(BF16)

    @pl.when(is_qs)
    def _():
        for t in range(tn // LANES):
            cols = slice(t * LANES, (t + 1) * LANES)
            o_ref[:, cols] = (swa_rope(acc[:, cols]) * (SWA_HD ** -0.5)).astype(BF16)

    @pl.when(is_kvs)
    def _():
        n_k = IN_SPLITS[K_S] // LANES
        for t in range(tn // LANES):
            cols = slice(t * LANES, (t + 1) * LANES)
            xt = acc[:, cols]
            o_ref[:, cols] = (swa_rope(xt) if t < n_k else xt).astype(BF16)

    @pl.when(jnp.logical_not(is_ret | is_qs | is_kvs))
    def _():
        o_ref[...] = acc.astype(BF16)


def _in_proj(x2d, g, w_blocks, tables):
    tm, tn = PROJ_TM, PROJ_TN
    assert all(IN_OFFSETS[s] % tn == 0 for s in (Q_R, K_R, V_R, Q_S, K_S))
    assert IN_OFFSETS[V_S] == IN_OFFSETS[K_S] + IN_SPLITS[K_S] and IN_SPLITS[K_S] + IN_SPLITS[V_S] == tn
    tab = pl.BlockSpec((tm, LANES), lambda i, j: (i, 0))
    return pl.pallas_call(
        _in_proj_kernel,
        grid=(SEQ // tm, D_IN // tn),
        in_specs=[pl.BlockSpec((tm, D_MODEL), lambda i, j: (i, 0)),
                  pl.BlockSpec((1, D_MODEL), lambda i, j: (0, 0)),
                  pl.BlockSpec((1, D_MODEL, tn), lambda i, j: (j, 0, 0)),
                  tab, tab, tab, tab, tab],
        out_specs=pl.BlockSpec((tm, tn), lambda i, j: (i, j)),
        out_shape=jax.ShapeDtypeStruct((SEQ, D_IN), BF16),
        scratch_shapes=[pltpu.VMEM((tm, D_MODEL), BF16)],
        compiler_params=_params("parallel", "arbitrary"),
        name="in_proj",
    )(x2d, g.reshape(1, D_MODEL), w_blocks, *tables)


def _retention_kernel(logg_ref, q_ref, k_ref, v_ref, g_ref, o_ref, state_ref, dmask_ref):
    c_rows = RET_CHUNK
    lg = logg_ref[pl.program_id(0)]

    @pl.when(pl.program_id(1) == 0)
    def _():
        state_ref[...] = jnp.zeros_like(state_ref)
        i = lax.broadcasted_iota(jnp.int32, (c_rows, c_rows), 0)
        j = lax.broadcasted_iota(jnp.int32, (c_rows, c_rows), 1)
        rel = (i - j).astype(F32)
        dmask_ref[...] = jnp.where(rel >= 0, jnp.exp(lg * jnp.maximum(rel, 0.0)), 0.0)

    idx = lax.broadcasted_iota(jnp.int32, (c_rows, 1), 0).astype(F32)
    q_dec = jnp.exp(lg * (idx + 1.0))
    k_dec = jnp.exp(lg * (c_rows - 1.0 - idx))
    chunk_decay = jnp.exp(jnp.full((1, RET_DV), lg * c_rows, F32))

    def body(c, carry):
        r = pl.ds(pl.multiple_of(c * c_rows, c_rows), c_rows)
        q, k, v = q_ref[r, :], k_ref[r, :], v_ref[r, :]
        scores = _dot_nt(q, k) * dmask_ref[...]
        inner = _dot(scores.astype(BF16), v)
        state = state_ref[...]
        cross = _dot((q.astype(F32) * q_dec).astype(BF16), state.astype(BF16))
        state_ref[...] = state * chunk_decay + _dot_tn((k.astype(F32) * k_dec).astype(BF16), v)
        y = _rms(inner + cross)
        g = g_ref[r, :].astype(F32)
        o_ref[r, :] = (y * (g * _sigmoid(g))).astype(o_ref.dtype)
        return carry

    lax.fori_loop(0, RET_ROWS // c_rows, body, 0)


def _retention(proj):
    log_g = jnp.log(1.0 - jnp.power(2.0, -5.0 - jnp.arange(RET_HEADS, dtype=F32)))
    rows = RET_ROWS
    head_blk = lambda split: pl.BlockSpec(
        (rows, RET_DK), lambda h, n, c=IN_OFFSETS[split] // RET_DK: (n, c + h))
    return pl.pallas_call(
        _retention_kernel,
        grid=(RET_HEADS, SEQ // rows),
        in_specs=[pl.BlockSpec(memory_space=pltpu.SMEM),
                  head_blk(Q_R), head_blk(K_R), head_blk(V_R), head_blk(G_R)],
        out_specs=pl.BlockSpec((rows, RET_DV), lambda h, n: (n, h)),
        out_shape=jax.ShapeDtypeStruct((SEQ, RET_HEADS * RET_DV), BF16),
        scratch_shapes=[pltpu.VMEM((RET_DK, RET_DV), F32), pltpu.VMEM((RET_CHUNK, RET_CHUNK), F32)],
        compiler_params=_params("parallel", "arbitrary"),
        name="retention",
    )(log_g, proj, proj, proj, proj)


def _swa_kernel(sinks_ref, q_ref, k_ref, v_ref, kp_ref, vp_ref, o_ref):
    w = WINDOW
    hd = SWA_HD
    group = SWA_HEADS // SWA_KV_HEADS
    step = pl.program_id(0)
    low = lax.broadcasted_iota(jnp.int32, (w, LANES), 1) < hd

    def dup_halves(x):
        xf = x.astype(F32)
        sw = pltpu.roll(xf, hd, 1)
        lo = lax.broadcasted_iota(jnp.int32, xf.shape, 1) < hd
        return jnp.where(lo, xf, sw).astype(BF16), jnp.where(lo, sw, xf).astype(BF16)

    def per_kv_head(cur_ref, prev_ref):
        allrows = jnp.concatenate([prev_ref[...], cur_ref[...]], axis=0)
        heads = []
        for c in range(allrows.shape[1] // LANES):
            heads.extend(dup_halves(allrows[:, c * LANES:(c + 1) * LANES]))
        return heads

    k2 = per_kv_head(k_ref, kp_ref)
    v2 = per_kv_head(v_ref, vp_ref)

    qi = lax.broadcasted_iota(jnp.int32, (w, 2 * w), 0)
    kj = lax.broadcasted_iota(jnp.int32, (w, 2 * w), 1)
    dist = qi + w - kj
    band = (dist >= 0) & (dist < WINDOW)
    zero = jnp.zeros((w, LANES), BF16)

    for t in range(SWA_ROWS // w):
        rows = slice(t * w, (t + 1) * w)
        mask = band & ((step > 0) | (kj >= w)) if t == 0 else band
        mask = jnp.concatenate([mask] * group, axis=0)
        for kh in range(SWA_KV_HEADS):
            k_w = k2[kh][t * w:(t + 2) * w]
            v_w = v2[kh][t * w:(t + 2) * w]
            cols = [slice((kh * group // 2 + p) * LANES, (kh * group // 2 + p + 1) * LANES)
                    for p in range(group // 2)]
            parts, sink = [], []
            for p, c in enumerate(cols):
                qc = q_ref[rows, c]
                parts += [jnp.where(low, qc, zero), jnp.where(low, zero, qc)]
                sink += [jnp.full((w, 1), sinks_ref[kh * group + 2 * p + e], F32) for e in range(2)]
            sink = jnp.concatenate(sink, axis=0)
            s = jnp.where(mask, _dot_nt(jnp.concatenate(parts, axis=0), k_w), NEG)
            m = jnp.maximum(jnp.max(s, axis=-1, keepdims=True), sink)
            e = jnp.exp(s - m)
            denom = jnp.sum(e, axis=-1, keepdims=True) + jnp.exp(sink - m)
            o = _dot(e.astype(BF16), v_w) / denom
            for p, c in enumerate(cols):
                o_ref[rows, c] = jnp.where(low, o[2 * p * w:(2 * p + 1) * w],
                                           o[(2 * p + 1) * w:(2 * p + 2) * w]).astype(o_ref.dtype)


def _swa(proj, sinks):
    w, rows = WINDOW, SWA_ROWS
    nq, nkv = SWA_HEADS * SWA_HD, SWA_KV_HEADS * SWA_HD
    per = rows // w
    cur = lambda split: pl.BlockSpec((rows, nkv), lambda i, c=IN_OFFSETS[split] // nkv: (i, c))
    prev = lambda split: pl.BlockSpec(
        (w, nkv), lambda i, c=IN_OFFSETS[split] // nkv: (jnp.maximum(i * per - 1, 0), c))
    return pl.pallas_call(
        _swa_kernel,
        grid=(SEQ // rows,),
        in_specs=[pl.BlockSpec(memory_space=pltpu.SMEM),
                  pl.BlockSpec((rows, nq), lambda i: (i, IN_OFFSETS[Q_S] // nq)),
                  cur(K_S), cur(V_S), prev(K_S), prev(V_S)],
        out_specs=pl.BlockSpec((rows, nq), lambda i: (i, 0)),
        out_shape=jax.ShapeDtypeStruct((SEQ, nq), BF16),
        compiler_params=_params("parallel"),
        name="swa",
    )(sinks, proj, proj, proj, proj, proj)


def _mem_kv_kernel(mem_ref, g_ref, w_ref, o_ref):
    memn = (_rms(mem_ref[...]) * g_ref[...]).astype(BF16)
    o_ref[...] = _dot(memn, w_ref[...]).astype(o_ref.dtype)


def _mem_kv(mem2d, g, w):
    n_out = 2 * X_HEADS * X_HD
    return pl.pallas_call(
        _mem_kv_kernel,
        out_shape=jax.ShapeDtypeStruct((N_MEM, n_out), BF16),
        compiler_params=pltpu.CompilerParams(vmem_limit_bytes=VMEM_LIMIT),
        name="mem_kv",
    )(mem2d, g.reshape(1, D_MODEL), w)


def _merge_kernel(x_ref, yr_ref, ys_ref, gr_ref, gs_ref, kv_ref,
                  wur_ref, wus_ref, wo_ref, gx_ref, wxq_ref, wxo_ref, gf_ref,
                  h_ref, n_ref):
    up_r = _dot(yr_ref[...], wur_ref[...])
    up_s = _dot(ys_ref[...], wus_ref[...])
    merged = (_sigmoid(gr_ref[...].astype(F32)) * up_r + _sigmoid(gs_ref[...].astype(F32)) * up_s)
    h1 = x_ref[...] + _dot(merged.astype(BF16), wo_ref[...])

    nx = (_rms(h1) * gx_ref[...]).astype(BF16)
    qx = _dot(nx, wxq_ref[...]).astype(BF16)
    kv = kv_ref[...]
    outs = []
    for h in range(X_HEADS):
        q_h = qx[:, h * X_HD:(h + 1) * X_HD]
        k_h = kv[:, h * X_HD:(h + 1) * X_HD]
        v_h = kv[:, (X_HEADS + h) * X_HD:(X_HEADS + h + 1) * X_HD]
        s = _dot_nt(q_h, k_h) * (X_HD ** -0.5)
        e = jnp.exp(s - jnp.max(s, axis=-1, keepdims=True))
        p = e / jnp.sum(e, axis=-1, keepdims=True)
        outs.append(_dot(p.astype(BF16), v_h))
    att = jnp.concatenate(outs, axis=-1).astype(BF16)
    h2 = h1 + _dot(att, wxo_ref[...])
    h_ref[...] = h2
    n_ref[...] = (_rms(h2) * gf_ref[...]).astype(BF16)


def _merge(x2d, yr, ys, proj, kv, w_up_ret, w_up_swa, w_o, g_x, w_xq, w_xo, g_ffn):
    tm = MERGE_TM
    rows = lambda width, col=0: pl.BlockSpec((tm, width), lambda i, c=col: (i, c))
    n_x = X_HEADS * X_HD
    return pl.pallas_call(
        _merge_kernel,
        grid=(SEQ // tm,),
        in_specs=[rows(D_MODEL), rows(RET_HEADS * RET_DV), rows(SWA_HEADS * SWA_HD),
                  rows(D_MODEL, IN_OFFSETS[GATE_R] // D_MODEL), rows(D_MODEL, IN_OFFSETS[GATE_S] // D_MODEL),
                  _resident((N_MEM, 2 * n_x)),
                  _resident((RET_HEADS * RET_DV, D_MODEL)), _resident((SWA_HEADS * SWA_HD, D_MODEL)),
                  _resident((D_MODEL, D_MODEL)), _resident((1, D_MODEL)),
                  _resident((D_MODEL, n_x)), _resident((n_x, D_MODEL)), _resident((1, D_MODEL))],
        out_specs=[rows(D_MODEL), rows(D_MODEL)],
        out_shape=[jax.ShapeDtypeStruct((SEQ, D_MODEL), F32),
                   jax.ShapeDtypeStruct((SEQ, D_MODEL), BF16)],
        compiler_params=_params("parallel"),
        name="merge_xattn",
    )(x2d, yr, ys, proj, proj, kv, w_up_ret, w_up_swa, w_o, g_x.reshape(1, D_MODEL),
      w_xq, w_xo, g_ffn.reshape(1, D_MODEL))


def _ffn_kernel(n_ref, h_ref, wgu_ref, wd_ref, gf_ref, o_ref):
    f = pl.program_id(1)
    tf = FFN_TF

    @pl.when(f == 0)
    def _():
        o_ref[...] = h_ref[...]

    gu = _dot(n_ref[...], wgu_ref[0])
    a, b = gu[:, :tf], gu[:, tf:]
    o_ref[...] += _dot((a * _sigmoid(a) * b).astype(BF16), wd_ref[...])

    @pl.when(f == pl.num_programs(1) - 1)
    def _():
        o_ref[...] = _rms(o_ref[...]) * gf_ref[...]


def _ffn(n2, h2, w_gate_up, w_down, g_final):
    tm, tf = FFN_TM, FFN_TF
    return pl.pallas_call(
        _ffn_kernel,
        grid=(SEQ // tm, D_FF // tf),
        in_specs=[pl.BlockSpec((tm, D_MODEL), lambda i, f: (i, 0)),
                  pl.BlockSpec((tm, D_MODEL), lambda i, f: (i, 0)),
                  pl.BlockSpec((1, D_MODEL, 2 * tf), lambda i, f: (f, 0, 0)),
                  pl.BlockSpec((tf, D_MODEL), lambda i, f: (f, 0)),
                  pl.BlockSpec((1, D_MODEL), lambda i, f: (0, 0))],
        out_specs=pl.BlockSpec((tm, D_MODEL), lambda i, f: (i, 0)),
        out_shape=jax.ShapeDtypeStruct((SEQ, D_MODEL), F32),
        compiler_params=_params("parallel", "arbitrary"),
        name="ffn",
    )(n2, h2, w_gate_up, w_down, g_final.reshape(1, D_MODEL))


def kernel(x, mem, positions, g_mix, w_in, w_up_ret, w_up_swa, sinks, w_o, g_x, g_mem,
           w_xq, w_xkv, w_xo, g_ffn, w_ffn_gate, w_ffn_up, w_ffn_down, g_final):
    assert x.shape == (1, SEQ, D_MODEL) and mem.shape == (1, N_MEM, D_MODEL)
    assert w_in.shape == (1, D_MODEL, D_IN)
    bf = lambda w: w[0].astype(BF16)
    x2d = x.reshape(SEQ, D_MODEL)
    w_in_blocks = _col_blocks(jnp.concatenate(
        [w_in[0, :, _REF_OFFSETS[s]:_REF_OFFSETS[s] + IN_SPLITS[s]].astype(BF16) for s in IN_ORDER], axis=1),
        PROJ_TN)
    w_gate_up = jnp.concatenate([_col_blocks(bf(w_ffn_gate), FFN_TF), _col_blocks(bf(w_ffn_up), FFN_TF)], axis=2)

    tables = _rope_tables(positions)
    proj = _in_proj(x2d, g_mix[0], w_in_blocks, tables)
    yr = _retention(proj)
    ys = _swa(proj, sinks[0])
    kv = _mem_kv(mem.reshape(N_MEM, D_MODEL), g_mem[0], bf(w_xkv))
    h2, n2 = _merge(x2d, yr, ys, proj, kv, bf(w_up_ret), bf(w_up_swa), bf(w_o), g_x[0],
                    bf(w_xq), bf(w_xo), g_ffn[0])
    out = _ffn(n2, h2, w_gate_up, bf(w_ffn_down), g_final)
    return out.reshape(1, SEQ, D_MODEL)
```

```python
import jax
import jax.numpy as jnp
import numpy as np
from jax import lax
from jax.experimental import pallas as pl
from jax.experimental.pallas import tpu as pltpu

F32 = jnp.float32
BF16 = jnp.bfloat16

D_MODEL = 2048
SEQ = 16384
N_MEM = 256
EPS = 1e-6
RET_HEADS = 4
RET_DK = 256
RET_DV = 256
RET_THETA = 10000.0
SWA_HEADS = 16
SWA_KV_HEADS = 4
SWA_HD = 64
WINDOW = 128
ROPE_THETA = 500000.0
ROPE_DIM = SWA_HD // 4
X_HEADS = 4
X_HD = 128
D_FF = -(-8 * D_MODEL // (3 * 256)) * 256
IN_SPLITS = (RET_HEADS * RET_DK, RET_HEADS * RET_DK, RET_HEADS * RET_DV, RET_HEADS * RET_DV,
             SWA_HEADS * SWA_HD, SWA_KV_HEADS * SWA_HD, SWA_KV_HEADS * SWA_HD, D_MODEL, D_MODEL)
D_IN = sum(IN_SPLITS)
IN_ORDER = (7, 8, 0, 1, 2, 3, 4, 5, 6)
_REF_OFFSETS = tuple(int(o) for o in np.cumsum((0,) + IN_SPLITS[:-1]))
_PERM_STARTS = np.cumsum((0,) + tuple(IN_SPLITS[s] for s in IN_ORDER[:-1]))
IN_OFFSETS = tuple(int(_PERM_STARTS[IN_ORDER.index(s)]) for s in range(len(IN_SPLITS)))
Q_R, K_R, V_R, G_R, Q_S, K_S, V_S, GATE_R, GATE_S = range(9)
NEG = -1e30

LANES = 128
VMEM_LIMIT = 56 * 1024 * 1024

RET_CHUNK = 256
RET_ROWS = 1024
ROPE_ROWS = 1024
PROJ_TM, PROJ_TN = 1024, 512
SWA_ROWS = 256
MERGE_TM = 256
FFN_TM, FFN_TF = 1024, 256


def _sigmoid(x):
    return 1.0 / (1.0 + jnp.exp(-x))


def _rms(x):
    return x * lax.rsqrt(jnp.mean(x * x, axis=-1, keepdims=True) + EPS)


def _dot(a, b):
    return jnp.dot(a, b, preferred_element_type=F32)


def _dot_nt(a, b):
    return lax.dot_general(a, b, (((1,), (1,)), ((), ())), preferred_element_type=F32)


def _dot_tn(a, b):
    return lax.dot_general(a, b, (((0,), (0,)), ((), ())), preferred_element_type=F32)


def _params(*sem):
    return pltpu.CompilerParams(dimension_semantics=sem, vmem_limit_bytes=VMEM_LIMIT)


def _resident(shape):
    nd = len(shape)
    return pl.BlockSpec(shape, lambda *_: (0,) * nd, pipeline_mode=pl.Buffered(1))


def _rope_tables_kernel(pos_ref, invr_ref, invs_ref, m1_ref, m2_ref,
                        rc_ref, rs_ref, sc_ref, s1_ref, s2_ref):
    pos = pos_ref[...].astype(F32)
    ang_r = pos * invr_ref[...]
    rc_ref[...] = jnp.cos(ang_r)
    rs_ref[...] = jnp.sin(ang_r)
    ang_s = pos * invs_ref[...]
    sn = jnp.sin(ang_s)
    sc_ref[...] = jnp.cos(ang_s)
    s1_ref[...] = -sn * m1_ref[...]
    s2_ref[...] = sn * m2_ref[...]


def _rope_tables(positions):
    pos = positions.reshape(SEQ, 1)
    half_r = RET_DK // 2
    inv_r = 1.0 / (RET_THETA ** (jnp.arange(half_r, dtype=F32) / half_r))
    half_s = ROPE_DIM // 2
    inv_s = 1.0 / (ROPE_THETA ** (jnp.arange(half_s, dtype=F32) / half_s))
    d = np.arange(LANES) % SWA_HD
    inv_s_lanes = jnp.where(d < ROPE_DIM, inv_s[d % half_s], 0.0).astype(F32)
    m1 = jnp.asarray((d < half_s).astype(np.float32))
    m2 = jnp.asarray(((d >= half_s) & (d < ROPE_DIM)).astype(np.float32))
    row = lambda v: v.reshape(1, LANES)
    tab = jax.ShapeDtypeStruct((SEQ, LANES), F32)
    vec = pl.BlockSpec((1, LANES), lambda i: (0, 0))
    blk = pl.BlockSpec((ROPE_ROWS, LANES), lambda i: (i, 0))
    return pl.pallas_call(
        _rope_tables_kernel,
        grid=(SEQ // ROPE_ROWS,),
        in_specs=[pl.BlockSpec((ROPE_ROWS, 1), lambda i: (i, 0)), vec, vec, vec, vec],
        out_specs=[blk] * 5,
        out_shape=[tab] * 5,
        compiler_params=_params("parallel"),
        name="rope_tables",
    )(pos, row(inv_r), row(inv_s_lanes), row(m1), row(m2))


def _in_proj_kernel(x_ref, g_ref, w_ref, rc_ref, rs_ref, sc_ref, s1_ref, s2_ref, o_ref, n_ref):
    tn = PROJ_TN
    j = pl.program_id(1)
    blk = lambda split: IN_OFFSETS[split] // tn

    @pl.when(j == 0)
    def _():
        n_ref[...] = (_rms(x_ref[...]) * g_ref[...]).astype(BF16)

    acc = _dot(n_ref[...], w_ref[...])
    is_ret = (j >= blk(Q_R)) & (j < blk(V_R))
    is_qs = (j >= blk(Q_S)) & (j < blk(K_S))
    is_kvs = j == blk(K_S)

    def swa_rope(xt):
        half = ROPE_DIM // 2
        return (xt * sc_ref[...] + pltpu.roll(xt, LANES - half, 1) * s1_ref[...]
                + pltpu.roll(xt, half, 1) * s2_ref[...])

    @pl.when(is_ret)
    def _():
        scale = jnp.where(j >= blk(K_R), RET_DK ** -0.5, 1.0).astype(F32)
        cos, sin = rc_ref[...] * scale, rs_ref[...] * scale
        half = RET_DK // 2
        for h in range(tn // RET_DK):
            a1 = acc[:, h * RET_DK:h * RET_DK + half]
            a2 = acc[:, h * RET_DK + half:(h + 1) * RET_DK]
            o_ref[:, h * RET_DK:h * RET_DK + half] = (a1 * cos - a2 * sin).astype(BF16)
            o_ref[:, h * RET_DK + half:(h + 1) * RET_DK] = (a2 * cos + a1 * sin).astype(BF16)

    @pl.when(is_qs)
    def _():
        for t in range(tn // LANES):
            cols = slice(t * LANES, (t + 1) * LANES)
            o_ref[:, cols] = (swa_rope(acc[:, cols]) * (SWA_HD ** -0.5)).astype(BF16)

    @pl.when(is_kvs)
    def _():
        n_k = IN_SPLITS[K_S] // LANES
        for t in range(tn // LANES):
            cols = slice(t * LANES, (t + 1) * LANES)
            xt = acc[:, cols]
            o_ref[:, cols] = (swa_rope(xt) if t < n_k else xt).astype(BF16)

    @pl.when(jnp.logical_not(is_ret | is_qs | is_kvs))
    def _():
        o_ref[...] = acc.astype(BF16)


def _in_proj(x2d, g, w, tables):
    tm, tn = PROJ_TM, PROJ_TN
    assert all(IN_OFFSETS[s] % tn == 0 for s in (Q_R, K_R, V_R, Q_S, K_S))
    assert IN_OFFSETS[V_S] == IN_OFFSETS[K_S] + IN_SPLITS[K_S] and IN_SPLITS[K_S] + IN_SPLITS[V_S] == tn
    tab = pl.BlockSpec((tm, LANES), lambda i, j: (i, 0))
    return pl.pallas_call(
        _in_proj_kernel,
        grid=(SEQ // tm, D_IN // tn),
        in_specs=[pl.BlockSpec((tm, D_MODEL), lambda i, j: (i, 0)),
                  pl.BlockSpec((1, D_MODEL), lambda i, j: (0, 0)),
                  pl.BlockSpec((D_MODEL, tn), lambda i, j: (0, j)),
                  tab, tab, tab, tab, tab],
        out_specs=pl.BlockSpec((tm, tn), lambda i, j: (i, j)),
        out_shape=jax.ShapeDtypeStruct((SEQ, D_IN), BF16),
        scratch_shapes=[pltpu.VMEM((tm, D_MODEL), BF16)],
        compiler_params=_params("parallel", "arbitrary"),
        name="in_proj",
    )(x2d, g.reshape(1, D_MODEL), w, *tables)


def _retention_kernel(logg_ref, q_ref, k_ref, v_ref, g_ref, o_ref, state_ref, dmask_ref):
    c_rows = RET_CHUNK
    lg = logg_ref[pl.program_id(0)]

    @pl.when(pl.program_id(1) == 0)
    def _():
        state_ref[...] = jnp.zeros_like(state_ref)
        i = lax.broadcasted_iota(jnp.int32, (c_rows, c_rows), 0)
        j = lax.broadcasted_iota(jnp.int32, (c_rows, c_rows), 1)
        rel = (i - j).astype(F32)
        dmask_ref[...] = jnp.where(rel >= 0, jnp.exp(lg * jnp.maximum(rel, 0.0)), 0.0)

    idx = lax.broadcasted_iota(jnp.int32, (c_rows, 1), 0).astype(F32)
    q_dec = jnp.exp(lg * (idx + 1.0))
    k_dec = jnp.exp(lg * (c_rows - 1.0 - idx))
    chunk_decay = jnp.exp(jnp.full((1, RET_DV), lg * c_rows, F32))

    def body(c, carry):
        r = pl.ds(pl.multiple_of(c * c_rows, c_rows), c_rows)
        q, k, v = q_ref[r, :], k_ref[r, :], v_ref[r, :]
        scores = _dot_nt(q, k) * dmask_ref[...]
        inner = _dot(scores.astype(BF16), v)
        state = state_ref[...]
        cross = _dot((q.astype(F32) * q_dec).astype(BF16), state.astype(BF16))
        state_ref[...] = state * chunk_decay + _dot_tn((k.astype(F32) * k_dec).astype(BF16), v)
        y = _rms(inner + cross)
        g = g_ref[r, :].astype(F32)
        o_ref[r, :] = (y * (g * _sigmoid(g))).astype(o_ref.dtype)
        return carry

    lax.fori_loop(0, RET_ROWS // c_rows, body, 0, unroll=True)


def _retention(proj):
    log_g = jnp.log(1.0 - jnp.power(2.0, -5.0 - jnp.arange(RET_HEADS, dtype=F32)))
    rows = RET_ROWS
    head_blk = lambda split: pl.BlockSpec(
        (rows, RET_DK), lambda h, n, c=IN_OFFSETS[split] // RET_DK: (n, c + h))
    return pl.pallas_call(
        _retention_kernel,
        grid=(RET_HEADS, SEQ // rows),
        in_specs=[pl.BlockSpec(memory_space=pltpu.SMEM),
                  head_blk(Q_R), head_blk(K_R), head_blk(V_R), head_blk(G_R)],
        out_specs=pl.BlockSpec((rows, RET_DV), lambda h, n: (n, h)),
        out_shape=jax.ShapeDtypeStruct((SEQ, RET_HEADS * RET_DV), BF16),
        scratch_shapes=[pltpu.VMEM((RET_DK, RET_DV), F32), pltpu.VMEM((RET_CHUNK, RET_CHUNK), F32)],
        compiler_params=_params("parallel", "arbitrary"),
        name="retention",
    )(log_g, proj, proj, proj, proj)


def _swa_kernel(sinks_ref, q_ref, k_ref, v_ref, kp_ref, vp_ref, o_ref):
    w = WINDOW
    hd = SWA_HD
    group = SWA_HEADS // SWA_KV_HEADS
    step = pl.program_id(0)
    low = lax.broadcasted_iota(jnp.int32, (w, LANES), 1) < hd

    def dup_halves(x):
        xf = x.astype(F32)
        sw = pltpu.roll(xf, hd, 1)
        lo = lax.broadcasted_iota(jnp.int32, xf.shape, 1) < hd
        return jnp.where(lo, xf, sw).astype(BF16), jnp.where(lo, sw, xf).astype(BF16)

    def per_kv_head(cur_ref, prev_ref):
        allrows = jnp.concatenate([prev_ref[...], cur_ref[...]], axis=0)
        heads = []
        for c in range(allrows.shape[1] // LANES):
            heads.extend(dup_halves(allrows[:, c * LANES:(c + 1) * LANES]))
        return heads

    k2 = per_kv_head(k_ref, kp_ref)
    v2 = per_kv_head(v_ref, vp_ref)

    qi = lax.broadcasted_iota(jnp.int32, (w, 2 * w), 0)
    kj = lax.broadcasted_iota(jnp.int32, (w, 2 * w), 1)
    dist = qi + w - kj
    band = (dist >= 0) & (dist < WINDOW)
    zero = jnp.zeros((w, LANES), BF16)
    ones = jnp.ones((2 * w, LANES), BF16)

    sink_slot = lax.broadcasted_iota(jnp.int32, (group * w, LANES), 1) == 0
    first_row = lax.broadcasted_iota(jnp.int32, (2 * w, LANES), 0) == 0

    for kh in range(SWA_KV_HEADS):
        cols = [slice((kh * group // 2 + p) * LANES, (kh * group // 2 + p + 1) * LANES)
                for p in range(group // 2)]
        sink = jnp.concatenate([jnp.full((w, LANES), sinks_ref[kh * group + g], F32)
                                for g in range(group)], axis=0)
        fill_prev = jnp.where(sink_slot, sink, NEG)
        for t in range(SWA_ROWS // w):
            rows = slice(t * w, (t + 1) * w)
            mask = band & ((step > 0) | (kj >= w)) if t == 0 else band
            mask = jnp.concatenate([mask] * group, axis=0)
            k_w = k2[kh][t * w:(t + 2) * w]
            v_w = jnp.where(first_row, jnp.zeros_like(ones), v2[kh][t * w:(t + 2) * w])
            parts = []
            for c in cols:
                qc = q_ref[rows, c]
                parts += [jnp.where(low, qc, zero), jnp.where(low, zero, qc)]
            raw = _dot_nt(jnp.concatenate(parts, axis=0), k_w)
            s = jnp.concatenate([jnp.where(mask[:, :w], raw[:, :w], fill_prev),
                                 jnp.where(mask[:, w:], raw[:, w:], NEG)], axis=-1)
            e = jnp.exp(s - jnp.max(s, axis=-1, keepdims=True))
            pv = _dot(e.astype(BF16), jnp.concatenate([v_w, ones], axis=-1))
            o = pv[:, :LANES] / pv[:, LANES:]
            for p, c in enumerate(cols):
                o_ref[rows, c] = jnp.where(low, o[2 * p * w:(2 * p + 1) * w],
                                           o[(2 * p + 1) * w:(2 * p + 2) * w]).astype(o_ref.dtype)


def _swa(proj, sinks):
    w, rows = WINDOW, SWA_ROWS
    nq, nkv = SWA_HEADS * SWA_HD, SWA_KV_HEADS * SWA_HD
    per = rows // w
    cur = lambda split: pl.BlockSpec((rows, nkv), lambda i, c=IN_OFFSETS[split] // nkv: (i, c))
    prev = lambda split: pl.BlockSpec(
        (w, nkv), lambda i, c=IN_OFFSETS[split] // nkv: (jnp.maximum(i * per - 1, 0), c))
    return pl.pallas_call(
        _swa_kernel,
        grid=(SEQ // rows,),
        in_specs=[pl.BlockSpec(memory_space=pltpu.SMEM),
                  pl.BlockSpec((rows, nq), lambda i: (i, IN_OFFSETS[Q_S] // nq)),
                  cur(K_S), cur(V_S), prev(K_S), prev(V_S)],
        out_specs=pl.BlockSpec((rows, nq), lambda i: (i, 0)),
        out_shape=jax.ShapeDtypeStruct((SEQ, nq), BF16),
        compiler_params=_params("parallel"),
        name="swa",
    )(sinks, proj, proj, proj, proj, proj)


def _mem_kv_kernel(mem_ref, g_ref, w_ref, o_ref):
    memn = (_rms(mem_ref[...]) * g_ref[...]).astype(BF16)
    o_ref[...] = _dot(memn, w_ref[...]).astype(o_ref.dtype)


def _mem_kv(mem2d, g, w):
    n_out = 2 * X_HEADS * X_HD
    return pl.pallas_call(
        _mem_kv_kernel,
        out_shape=jax.ShapeDtypeStruct((N_MEM, n_out), BF16),
        compiler_params=pltpu.CompilerParams(vmem_limit_bytes=VMEM_LIMIT),
        name="mem_kv",
    )(mem2d, g.reshape(1, D_MODEL), w)


def _merge_kernel(x_ref, yr_ref, ys_ref, gr_ref, gs_ref, kv_ref,
                  wur_ref, wus_ref, wo_ref, gx_ref, wxq_ref, wxo_ref, gf_ref,
                  h_ref, n_ref):
    up_r = _dot(yr_ref[...], wur_ref[...])
    up_s = _dot(ys_ref[...], wus_ref[...])
    merged = (_sigmoid(gr_ref[...].astype(F32)) * up_r + _sigmoid(gs_ref[...].astype(F32)) * up_s)
    h1 = x_ref[...] + _dot(merged.astype(BF16), wo_ref[...])

    nx = (_rms(h1) * gx_ref[...]).astype(BF16)
    qx = _dot(nx, wxq_ref[...]).astype(BF16)
    kv = kv_ref[...]
    outs = []
    for h in range(X_HEADS):
        q_h = qx[:, h * X_HD:(h + 1) * X_HD]
        k_h = kv[:, h * X_HD:(h + 1) * X_HD]
        v_h = kv[:, (X_HEADS + h) * X_HD:(X_HEADS + h + 1) * X_HD]
        s = _dot_nt(q_h, k_h) * (X_HD ** -0.5)
        e = jnp.exp(s - jnp.max(s, axis=-1, keepdims=True))
        p = e / jnp.sum(e, axis=-1, keepdims=True)
        outs.append(_dot(p.astype(BF16), v_h))
    att = jnp.concatenate(outs, axis=-1).astype(BF16)
    h2 = h1 + _dot(att, wxo_ref[...])
    h_ref[...] = h2
    n_ref[...] = (_rms(h2) * gf_ref[...]).astype(BF16)


def _merge(x2d, yr, ys, proj, kv, w_up_ret, w_up_swa, w_o, g_x, w_xq, w_xo, g_ffn):
    tm = MERGE_TM
    rows = lambda width, col=0: pl.BlockSpec((tm, width), lambda i, c=col: (i, c))
    n_x = X_HEADS * X_HD
    return pl.pallas_call(
        _merge_kernel,
        grid=(SEQ // tm,),
        in_specs=[rows(D_MODEL), rows(RET_HEADS * RET_DV), rows(SWA_HEADS * SWA_HD),
                  rows(D_MODEL, IN_OFFSETS[GATE_R] // D_MODEL), rows(D_MODEL, IN_OFFSETS[GATE_S] // D_MODEL),
                  _resident((N_MEM, 2 * n_x)),
                  _resident((RET_HEADS * RET_DV, D_MODEL)), _resident((SWA_HEADS * SWA_HD, D_MODEL)),
                  _resident((D_MODEL, D_MODEL)), _resident((1, D_MODEL)),
                  _resident((D_MODEL, n_x)), _resident((n_x, D_MODEL)), _resident((1, D_MODEL))],
        out_specs=[rows(D_MODEL), rows(D_MODEL)],
        out_shape=[jax.ShapeDtypeStruct((SEQ, D_MODEL), F32),
                   jax.ShapeDtypeStruct((SEQ, D_MODEL), BF16)],
        compiler_params=_params("parallel"),
        name="merge_xattn",
    )(x2d, yr, ys, proj, proj, kv, w_up_ret, w_up_swa, w_o, g_x.reshape(1, D_MODEL),
      w_xq, w_xo, g_ffn.reshape(1, D_MODEL))


def _ffn_kernel(n_ref, h_ref, wg_ref, wu_ref, wd_ref, gf_ref, o_ref):
    f = pl.program_id(1)

    @pl.when(f == 0)
    def _():
        o_ref[...] = h_ref[...]

    n2 = n_ref[...]
    a = _dot(n2, wg_ref[...])
    b = _dot(n2, wu_ref[...])
    o_ref[...] += _dot((a * _sigmoid(a) * b).astype(BF16), wd_ref[...])

    @pl.when(f == pl.num_programs(1) - 1)
    def _():
        o_ref[...] = _rms(o_ref[...]) * gf_ref[...]


def _ffn(n2, h2, w_gate, w_up, w_down, g_final):
    tm, tf = FFN_TM, FFN_TF
    return pl.pallas_call(
        _ffn_kernel,
        grid=(SEQ // tm, D_FF // tf),
        in_specs=[pl.BlockSpec((tm, D_MODEL), lambda i, f: (i, 0)),
                  pl.BlockSpec((tm, D_MODEL), lambda i, f: (i, 0)),
                  pl.BlockSpec((D_MODEL, tf), lambda i, f: (0, f)),
                  pl.BlockSpec((D_MODEL, tf), lambda i, f: (0, f)),
                  pl.BlockSpec((tf, D_MODEL), lambda i, f: (f, 0)),
                  pl.BlockSpec((1, D_MODEL), lambda i, f: (0, 0))],
        out_specs=pl.BlockSpec((tm, D_MODEL), lambda i, f: (i, 0)),
        out_shape=jax.ShapeDtypeStruct((SEQ, D_MODEL), F32),
        compiler_params=_params("parallel", "arbitrary"),
        name="ffn",
    )(n2, h2, w_gate, w_up, w_down, g_final.reshape(1, D_MODEL))


def kernel(x, mem, positions, g_mix, w_in, w_up_ret, w_up_swa, sinks, w_o, g_x, g_mem,
           w_xq, w_xkv, w_xo, g_ffn, w_ffn_gate, w_ffn_up, w_ffn_down, g_final):
    assert x.shape == (1, SEQ, D_MODEL) and mem.shape == (1, N_MEM, D_MODEL)
    assert w_in.shape == (1, D_MODEL, D_IN)
    bf = lambda w: w[0].astype(BF16)
    x2d = x.reshape(SEQ, D_MODEL)
    w_in_perm = jnp.concatenate(
        [w_in[0, :, _REF_OFFSETS[s]:_REF_OFFSETS[s] + IN_SPLITS[s]].astype(BF16) for s in IN_ORDER], axis=1)

    tables = _rope_tables(positions)
    proj = _in_proj(x2d, g_mix[0], w_in_perm, tables)
    yr = _retention(proj)
    ys = _swa(proj, sinks[0])
    kv = _mem_kv(mem.reshape(N_MEM, D_MODEL), g_mem[0], bf(w_xkv))
    h2, n2 = _merge(x2d, yr, ys, proj, kv, bf(w_up_ret), bf(w_up_swa), bf(w_o), g_x[0],
                    bf(w_xq), bf(w_xo), g_ffn[0])
    out = _ffn(n2, h2, bf(w_ffn_gate), bf(w_ffn_up), bf(w_ffn_down), g_final)
    return out.reshape(1, SEQ, D_MODEL)
```

```python
import jax
import jax.numpy as jnp
import numpy as np
from jax import lax
from jax.experimental import pallas as pl
from jax.experimental.pallas import tpu as pltpu

F32 = jnp.float32
BF16 = jnp.bfloat16

D_MODEL = 2048
SEQ = 16384
N_MEM = 256
EPS = 1e-6
RET_HEADS = 4
RET_DK = 256
RET_DV = 256
RET_THETA = 10000.0
SWA_HEADS = 16
SWA_KV_HEADS = 4
SWA_HD = 64
WINDOW = 128
ROPE_THETA = 500000.0
ROPE_DIM = SWA_HD // 4
X_HEADS = 4
X_HD = 128
D_FF = -(-8 * D_MODEL // (3 * 256)) * 256
IN_SPLITS = (RET_HEADS * RET_DK, RET_HEADS * RET_DK, RET_HEADS * RET_DV, RET_HEADS * RET_DV,
             SWA_HEADS * SWA_HD, SWA_KV_HEADS * SWA_HD, SWA_KV_HEADS * SWA_HD, D_MODEL, D_MODEL)
D_IN = sum(IN_SPLITS)
IN_ORDER = (7, 8, 0, 1, 2, 3, 4, 5, 6)
_REF_OFFSETS = tuple(int(o) for o in np.cumsum((0,) + IN_SPLITS[:-1]))
_PERM_STARTS = np.cumsum((0,) + tuple(IN_SPLITS[s] for s in IN_ORDER[:-1]))
IN_OFFSETS = tuple(int(_PERM_STARTS[IN_ORDER.index(s)]) for s in range(len(IN_SPLITS)))
Q_R, K_R, V_R, G_R, Q_S, K_S, V_S, GATE_R, GATE_S = range(9)
NEG = -1e30

LANES = 128
VMEM_LIMIT = 58 * 1024 * 1024

RET_CHUNK = 256
RET_ROWS = 1024
ROPE_ROWS = 1024
PROJ_TM, PROJ_TN = 1024, 512
SWA_ROWS = 256
MERGE_TM = 256
FFN_TM, FFN_TF = 1024, 512
FFN_SLAB = 256


def _sigmoid(x):
    return 1.0 / (1.0 + jnp.exp(-x))


def _rms(x):
    return x * lax.rsqrt(jnp.mean(x * x, axis=-1, keepdims=True) + EPS)


def _dot(a, b):
    return jnp.dot(a, b, preferred_element_type=F32)


def _dot_nt(a, b):
    return lax.dot_general(a, b, (((1,), (1,)), ((), ())), preferred_element_type=F32)


def _dot_tn(a, b):
    return lax.dot_general(a, b, (((0,), (0,)), ((), ())), preferred_element_type=F32)


def _params(*sem):
    return pltpu.CompilerParams(dimension_semantics=sem, vmem_limit_bytes=VMEM_LIMIT)


def _resident(shape):
    nd = len(shape)
    return pl.BlockSpec(shape, lambda *_: (0,) * nd, pipeline_mode=pl.Buffered(1))


def _rope_tables_kernel(pos_ref, invr_ref, invs_ref, m1_ref, m2_ref,
                        rc_ref, rs_ref, sc_ref, s1_ref, s2_ref):
    pos = pos_ref[...].astype(F32)
    ang_r = pos * invr_ref[...]
    rc_ref[...] = jnp.cos(ang_r)
    rs_ref[...] = jnp.sin(ang_r)
    ang_s = pos * invs_ref[...]
    sn = jnp.sin(ang_s)
    sc_ref[...] = jnp.cos(ang_s)
    s1_ref[...] = -sn * m1_ref[...]
    s2_ref[...] = sn * m2_ref[...]


def _rope_tables(positions):
    pos = positions.reshape(SEQ, 1)
    half_r = RET_DK // 2
    inv_r = 1.0 / (RET_THETA ** (jnp.arange(half_r, dtype=F32) / half_r))
    half_s = ROPE_DIM // 2
    inv_s = 1.0 / (ROPE_THETA ** (jnp.arange(half_s, dtype=F32) / half_s))
    d = np.arange(LANES) % SWA_HD
    inv_s_lanes = jnp.where(d < ROPE_DIM, inv_s[d % half_s], 0.0).astype(F32)
    m1 = jnp.asarray((d < half_s).astype(np.float32))
    m2 = jnp.asarray(((d >= half_s) & (d < ROPE_DIM)).astype(np.float32))
    row = lambda v: v.reshape(1, LANES)
    tab = jax.ShapeDtypeStruct((SEQ, LANES), F32)
    vec = pl.BlockSpec((1, LANES), lambda i: (0, 0))
    blk = pl.BlockSpec((ROPE_ROWS, LANES), lambda i: (i, 0))
    return pl.pallas_call(
        _rope_tables_kernel,
        grid=(SEQ // ROPE_ROWS,),
        in_specs=[pl.BlockSpec((ROPE_ROWS, 1), lambda i: (i, 0)), vec, vec, vec, vec],
        out_specs=[blk] * 5,
        out_shape=[tab] * 5,
        compiler_params=_params("parallel"),
        name="rope_tables",
    )(pos, row(inv_r), row(inv_s_lanes), row(m1), row(m2))


def _in_proj_kernel(x_ref, g_ref, w_ref, rc_ref, rs_ref, sc_ref, s1_ref, s2_ref, o_ref, n_ref, acc_ref):
    tn = PROJ_TN
    j = pl.program_id(1)
    blk = lambda split: IN_OFFSETS[split] // tn

    @pl.when(j == 0)
    def _():
        n_ref[...] = (_rms(x_ref[...]) * g_ref[...]).astype(BF16)

    acc = _dot(n_ref[...], w_ref[...])
    o_ref[...] = acc.astype(BF16)
    acc_ref[...] = acc
    is_ret = (j >= blk(Q_R)) & (j < blk(V_R))
    is_qs = (j >= blk(Q_S)) & (j < blk(K_S))
    is_kvs = j == blk(K_S)

    def swa_rope(cols):
        half = ROPE_DIM // 2
        xt = acc_ref[:, cols]
        return (xt * sc_ref[...] + pltpu.roll(xt, LANES - half, 1) * s1_ref[...]
                + pltpu.roll(xt, half, 1) * s2_ref[...])

    @pl.when(is_ret)
    def _():
        scale = jnp.where(j >= blk(K_R), RET_DK ** -0.5, 1.0).astype(F32)
        cos, sin = rc_ref[...] * scale, rs_ref[...] * scale
        half = RET_DK // 2
        for h in range(tn // RET_DK):
            a1 = acc_ref[:, h * RET_DK:h * RET_DK + half]
            a2 = acc_ref[:, h * RET_DK + half:(h + 1) * RET_DK]
            o_ref[:, h * RET_DK:h * RET_DK + half] = (a1 * cos - a2 * sin).astype(BF16)
            o_ref[:, h * RET_DK + half:(h + 1) * RET_DK] = (a2 * cos + a1 * sin).astype(BF16)

    @pl.when(is_qs)
    def _():
        for t in range(tn // LANES):
            cols = slice(t * LANES, (t + 1) * LANES)
            o_ref[:, cols] = (swa_rope(cols) * (SWA_HD ** -0.5)).astype(BF16)

    @pl.when(is_kvs)
    def _():
        for t in range(IN_SPLITS[K_S] // LANES):
            cols = slice(t * LANES, (t + 1) * LANES)
            o_ref[:, cols] = swa_rope(cols).astype(BF16)


def _in_proj(x2d, g, w, tables):
    tm, tn = PROJ_TM, PROJ_TN
    assert all(IN_OFFSETS[s] % tn == 0 for s in (Q_R, K_R, V_R, Q_S, K_S))
    assert IN_OFFSETS[V_S] == IN_OFFSETS[K_S] + IN_SPLITS[K_S] and IN_SPLITS[K_S] + IN_SPLITS[V_S] == tn
    tab = pl.BlockSpec((tm, LANES), lambda i, j: (i, 0))
    return pl.pallas_call(
        _in_proj_kernel,
        grid=(SEQ // tm, D_IN // tn),
        in_specs=[pl.BlockSpec((tm, D_MODEL), lambda i, j: (i, 0)),
                  pl.BlockSpec((1, D_MODEL), lambda i, j: (0, 0)),
                  pl.BlockSpec((D_MODEL, tn), lambda i, j: (0, j)),
                  tab, tab, tab, tab, tab],
        out_specs=pl.BlockSpec((tm, tn), lambda i, j: (i, j)),
        out_shape=jax.ShapeDtypeStruct((SEQ, D_IN), BF16),
        scratch_shapes=[pltpu.VMEM((tm, D_MODEL), BF16), pltpu.VMEM((tm, tn), F32)],
        compiler_params=_params("parallel", "arbitrary"),
        name="in_proj",
    )(x2d, g.reshape(1, D_MODEL), w, *tables)


def _retention_kernel(logg_ref, q_ref, k_ref, v_ref, g_ref, o_ref, state_ref, dmask_ref):
    c_rows = RET_CHUNK
    lg = logg_ref[pl.program_id(0)]

    @pl.when(pl.program_id(1) == 0)
    def _():
        state_ref[...] = jnp.zeros_like(state_ref)
        i = lax.broadcasted_iota(jnp.int32, (c_rows, c_rows), 0)
        j = lax.broadcasted_iota(jnp.int32, (c_rows, c_rows), 1)
        rel = (i - j).astype(F32)
        dmask_ref[...] = jnp.where(rel >= 0, jnp.exp(lg * jnp.maximum(rel, 0.0)), 0.0)

    idx = lax.broadcasted_iota(jnp.int32, (c_rows, 1), 0).astype(F32)
    q_dec = jnp.exp(lg * (idx + 1.0))
    k_dec = jnp.exp(lg * (c_rows - 1.0 - idx))
    chunk_decay = jnp.exp(jnp.full((1, RET_DV), lg * c_rows, F32))

    def body(c, carry):
        r = pl.ds(pl.multiple_of(c * c_rows, c_rows), c_rows)
        q, k, v = q_ref[r, :], k_ref[r, :], v_ref[r, :]
        scores = _dot_nt(q, k) * dmask_ref[...]
        inner = _dot(scores.astype(BF16), v)
        state = state_ref[...]
        cross = _dot((q.astype(F32) * q_dec).astype(BF16), state.astype(BF16))
        state_ref[...] = state * chunk_decay + _dot_tn((k.astype(F32) * k_dec).astype(BF16), v)
        y = _rms(inner + cross)
        g = g_ref[r, :].astype(F32)
        o_ref[r, :] = (y * (g * _sigmoid(g))).astype(o_ref.dtype)
        return carry

    lax.fori_loop(0, RET_ROWS // c_rows, body, 0, unroll=True)


def _retention(proj):
    log_g = jnp.log(1.0 - jnp.power(2.0, -5.0 - jnp.arange(RET_HEADS, dtype=F32)))
    rows = RET_ROWS
    head_blk = lambda split: pl.BlockSpec(
        (rows, RET_DK), lambda h, n, c=IN_OFFSETS[split] // RET_DK: (n, c + h))
    return pl.pallas_call(
        _retention_kernel,
        grid=(RET_HEADS, SEQ // rows),
        in_specs=[pl.BlockSpec(memory_space=pltpu.SMEM),
                  head_blk(Q_R), head_blk(K_R), head_blk(V_R), head_blk(G_R)],
        out_specs=pl.BlockSpec((rows, RET_DV), lambda h, n: (n, h)),
        out_shape=jax.ShapeDtypeStruct((SEQ, RET_HEADS * RET_DV), BF16),
        scratch_shapes=[pltpu.VMEM((RET_DK, RET_DV), F32), pltpu.VMEM((RET_CHUNK, RET_CHUNK), F32)],
        compiler_params=_params("parallel", "arbitrary"),
        name="retention",
    )(log_g, proj, proj, proj, proj)


def _swa_kernel(sinks_ref, q_ref, k_ref, v_ref, kp_ref, vp_ref, o_ref):
    w = WINDOW
    hd = SWA_HD
    group = SWA_HEADS // SWA_KV_HEADS
    step = pl.program_id(0)
    low = lax.broadcasted_iota(jnp.int32, (w, LANES), 1) < hd

    def dup_halves(x):
        xf = x.astype(F32)
        sw = pltpu.roll(xf, hd, 1)
        lo = lax.broadcasted_iota(jnp.int32, xf.shape, 1) < hd
        return jnp.where(lo, xf, sw).astype(BF16), jnp.where(lo, sw, xf).astype(BF16)

    def per_kv_head(cur_ref, prev_ref):
        allrows = jnp.concatenate([prev_ref[...], cur_ref[...]], axis=0)
        heads = []
        for c in range(allrows.shape[1] // LANES):
            heads.extend(dup_halves(allrows[:, c * LANES:(c + 1) * LANES]))
        return heads

    k2 = per_kv_head(k_ref, kp_ref)
    v2 = per_kv_head(v_ref, vp_ref)

    qi = lax.broadcasted_iota(jnp.int32, (w, 2 * w), 0)
    kj = lax.broadcasted_iota(jnp.int32, (w, 2 * w), 1)
    dist = qi + w - kj
    band = (dist >= 0) & (dist < WINDOW)
    zero = jnp.zeros((w, LANES), BF16)
    ones = jnp.ones((2 * w, LANES), BF16)

    sink_slot = lax.broadcasted_iota(jnp.int32, (group * w, LANES), 1) == 0
    first_row = lax.broadcasted_iota(jnp.int32, (2 * w, LANES), 0) == 0

    for kh in range(SWA_KV_HEADS):
        cols = [slice((kh * group // 2 + p) * LANES, (kh * group // 2 + p + 1) * LANES)
                for p in range(group // 2)]
        sink = jnp.concatenate([jnp.full((w, LANES), sinks_ref[kh * group + g], F32)
                                for g in range(group)], axis=0)
        fill_prev = jnp.where(sink_slot, sink, NEG)
        for t in range(SWA_ROWS // w):
            rows = slice(t * w, (t + 1) * w)
            mask = band & ((step > 0) | (kj >= w)) if t == 0 else band
            mask = jnp.concatenate([mask] * group, axis=0)
            k_w = k2[kh][t * w:(t + 2) * w]
            v_w = jnp.where(first_row, jnp.zeros_like(ones), v2[kh][t * w:(t + 2) * w])
            parts = []
            for c in cols:
                qc = q_ref[rows, c]
                parts += [jnp.where(low, qc, zero), jnp.where(low, zero, qc)]
            raw = _dot_nt(jnp.concatenate(parts, axis=0), k_w)
            s = jnp.concatenate([jnp.where(mask[:, :w], raw[:, :w], fill_prev),
                                 jnp.where(mask[:, w:], raw[:, w:], NEG)], axis=-1)
            e = jnp.exp(s - jnp.max(s, axis=-1, keepdims=True))
            pv = _dot(e.astype(BF16), jnp.concatenate([v_w, ones], axis=-1))
            o = pv[:, :LANES] / pv[:, LANES:]
            for p, c in enumerate(cols):
                o_ref[rows, c] = jnp.where(low, o[2 * p * w:(2 * p + 1) * w],
                                           o[(2 * p + 1) * w:(2 * p + 2) * w]).astype(o_ref.dtype)


def _swa(proj, sinks):
    w, rows = WINDOW, SWA_ROWS
    nq, nkv = SWA_HEADS * SWA_HD, SWA_KV_HEADS * SWA_HD
    per = rows // w
    cur = lambda split: pl.BlockSpec((rows, nkv), lambda i, c=IN_OFFSETS[split] // nkv: (i, c))
    prev = lambda split: pl.BlockSpec(
        (w, nkv), lambda i, c=IN_OFFSETS[split] // nkv: (jnp.maximum(i * per - 1, 0), c))
    return pl.pallas_call(
        _swa_kernel,
        grid=(SEQ // rows,),
        in_specs=[pl.BlockSpec(memory_space=pltpu.SMEM),
                  pl.BlockSpec((rows, nq), lambda i: (i, IN_OFFSETS[Q_S] // nq)),
                  cur(K_S), cur(V_S), prev(K_S), prev(V_S)],
        out_specs=pl.BlockSpec((rows, nq), lambda i: (i, 0)),
        out_shape=jax.ShapeDtypeStruct((SEQ, nq), BF16),
        compiler_params=_params("parallel"),
        name="swa",
    )(sinks, proj, proj, proj, proj, proj)


def _mem_kv_kernel(mem_ref, g_ref, w_ref, o_ref):
    memn = (_rms(mem_ref[...]) * g_ref[...]).astype(BF16)
    o_ref[...] = _dot(memn, w_ref[...]).astype(o_ref.dtype)


def _mem_kv(mem2d, g, w):
    n_out = 2 * X_HEADS * X_HD
    return pl.pallas_call(
        _mem_kv_kernel,
        out_shape=jax.ShapeDtypeStruct((N_MEM, n_out), BF16),
        compiler_params=pltpu.CompilerParams(vmem_limit_bytes=VMEM_LIMIT),
        name="mem_kv",
    )(mem2d, g.reshape(1, D_MODEL), w)


def _merge_kernel(x_ref, yr_ref, ys_ref, gr_ref, gs_ref, kv_ref,
                  wur_ref, wus_ref, wo_ref, gx_ref, wxq_ref, wxo_ref, gf_ref,
                  h_ref, n_ref):
    up_r = _dot(yr_ref[...], wur_ref[...])
    up_s = _dot(ys_ref[...], wus_ref[...])
    merged = (_sigmoid(gr_ref[...].astype(F32)) * up_r + _sigmoid(gs_ref[...].astype(F32)) * up_s)
    h1 = x_ref[...] + _dot(merged.astype(BF16), wo_ref[...])

    nx = (_rms(h1) * gx_ref[...]).astype(BF16)
    qx = _dot(nx, wxq_ref[...]).astype(BF16)
    kv = kv_ref[...]
    outs = []
    for h in range(X_HEADS):
        q_h = qx[:, h * X_HD:(h + 1) * X_HD]
        k_h = kv[:, h * X_HD:(h + 1) * X_HD]
        v_h = kv[:, (X_HEADS + h) * X_HD:(X_HEADS + h + 1) * X_HD]
        s = _dot_nt(q_h, k_h) * (X_HD ** -0.5)
        e = jnp.exp(s - jnp.max(s, axis=-1, keepdims=True))
        p = e / jnp.sum(e, axis=-1, keepdims=True)
        outs.append(_dot(p.astype(BF16), v_h))
    att = jnp.concatenate(outs, axis=-1).astype(BF16)
    h2 = h1 + _dot(att, wxo_ref[...])
    h_ref[...] = h2
    n_ref[...] = (_rms(h2) * gf_ref[...]).astype(BF16)


def _merge(x2d, yr, ys, proj, kv, w_up_ret, w_up_swa, w_o, g_x, w_xq, w_xo, g_ffn):
    tm = MERGE_TM
    rows = lambda width, col=0: pl.BlockSpec((tm, width), lambda i, c=col: (i, c))
    n_x = X_HEADS * X_HD
    return pl.pallas_call(
        _merge_kernel,
        grid=(SEQ // tm,),
        in_specs=[rows(D_MODEL), rows(RET_HEADS * RET_DV), rows(SWA_HEADS * SWA_HD),
                  rows(D_MODEL, IN_OFFSETS[GATE_R] // D_MODEL), rows(D_MODEL, IN_OFFSETS[GATE_S] // D_MODEL),
                  _resident((N_MEM, 2 * n_x)),
                  _resident((RET_HEADS * RET_DV, D_MODEL)), _resident((SWA_HEADS * SWA_HD, D_MODEL)),
                  _resident((D_MODEL, D_MODEL)), _resident((1, D_MODEL)),
                  _resident((D_MODEL, n_x)), _resident((n_x, D_MODEL)), _resident((1, D_MODEL))],
        out_specs=[rows(D_MODEL), rows(D_MODEL)],
        out_shape=[jax.ShapeDtypeStruct((SEQ, D_MODEL), F32),
                   jax.ShapeDtypeStruct((SEQ, D_MODEL), BF16)],
        compiler_params=_params("parallel"),
        name="merge_xattn",
    )(x2d, yr, ys, proj, proj, kv, w_up_ret, w_up_swa, w_o, g_x.reshape(1, D_MODEL),
      w_xq, w_xo, g_ffn.reshape(1, D_MODEL))


def _ffn_kernel(n_ref, h_hbm, wg_ref, wu_ref, wd_ref, gf_ref, o_ref, h_ref, h_sem):
    i, f = pl.program_id(0), pl.program_id(1)
    rows = pl.ds(pl.multiple_of(i * FFN_TM, FFN_TM), FFN_TM)
    residual_copy = pltpu.make_async_copy(h_hbm.at[rows, :], h_ref, h_sem)

    @pl.when(f == 0)
    def _():
        residual_copy.start()
        for c in range(FFN_TM // FFN_SLAB):
            o_ref[c * FFN_SLAB:(c + 1) * FFN_SLAB, :] = jnp.zeros((FFN_SLAB, D_MODEL), F32)

    n2 = n_ref[...]
    ts = []
    for c in range(FFN_TF // FFN_SLAB):
        cols = slice(c * FFN_SLAB, (c + 1) * FFN_SLAB)
        a = _dot(n2, wg_ref[:, cols])
        b = _dot(n2, wu_ref[:, cols])
        ts.append((a * _sigmoid(a) * b).astype(BF16))
    t = jnp.concatenate(ts, axis=-1)
    for c in range(D_MODEL // FFN_TF):
        cols = slice(c * FFN_TF, (c + 1) * FFN_TF)
        o_ref[:, cols] += _dot(t, wd_ref[:, cols])

    @pl.when(f == pl.num_programs(1) - 1)
    def _():
        residual_copy.wait()
        for c in range(FFN_TM // FFN_SLAB):
            r = slice(c * FFN_SLAB, (c + 1) * FFN_SLAB)
            o_ref[r, :] = _rms(h_ref[r, :] + o_ref[r, :]) * gf_ref[...]


def _ffn(n2, h2, w_gate, w_up, w_down, g_final):
    tm, tf = FFN_TM, FFN_TF
    return pl.pallas_call(
        _ffn_kernel,
        grid=(SEQ // tm, D_FF // tf),
        in_specs=[pl.BlockSpec((tm, D_MODEL), lambda i, f: (i, 0)),
                  pl.BlockSpec(memory_space=pl.ANY),
                  pl.BlockSpec((D_MODEL, tf), lambda i, f: (0, f)),
                  pl.BlockSpec((D_MODEL, tf), lambda i, f: (0, f)),
                  pl.BlockSpec((tf, D_MODEL), lambda i, f: (f, 0)),
                  pl.BlockSpec((1, D_MODEL), lambda i, f: (0, 0))],
        out_specs=pl.BlockSpec((tm, D_MODEL), lambda i, f: (i, 0)),
        out_shape=jax.ShapeDtypeStruct((SEQ, D_MODEL), F32),
        scratch_shapes=[pltpu.VMEM((tm, D_MODEL), F32), pltpu.SemaphoreType.DMA(())],
        compiler_params=_params("arbitrary", "arbitrary"),
        name="ffn",
    )(n2, h2, w_gate, w_up, w_down, g_final.reshape(1, D_MODEL))


def kernel(x, mem, positions, g_mix, w_in, w_up_ret, w_up_swa, sinks, w_o, g_x, g_mem,
           w_xq, w_xkv, w_xo, g_ffn, w_ffn_gate, w_ffn_up, w_ffn_down, g_final):
    assert x.shape == (1, SEQ, D_MODEL) and mem.shape == (1, N_MEM, D_MODEL)
    assert w_in.shape == (1, D_MODEL, D_IN)
    bf = lambda w: w[0].astype(BF16)
    x2d = x.reshape(SEQ, D_MODEL)
    w_in_perm = jnp.concatenate(
        [w_in[0, :, _REF_OFFSETS[s]:_REF_OFFSETS[s] + IN_SPLITS[s]].astype(BF16) for s in IN_ORDER], axis=1)

    tables = _rope_tables(positions)
    proj = _in_proj(x2d, g_mix[0], w_in_perm, tables)
    yr = _retention(proj)
    ys = _swa(proj, sinks[0])
    kv = _mem_kv(mem.reshape(N_MEM, D_MODEL), g_mem[0], bf(w_xkv))
    h2, n2 = _merge(x2d, yr, ys, proj, kv, bf(w_up_ret), bf(w_up_swa), bf(w_o), g_x[0],
                    bf(w_xq), bf(w_xo), g_ffn[0])
    out = _ffn(n2, h2, bf(w_ffn_gate), bf(w_ffn_up), bf(w_ffn_down), g_final)
    return out.reshape(1, SEQ, D_MODEL)
```

```python
import jax
import jax.numpy as jnp
import numpy as np
from jax import lax
from jax.experimental import pallas as pl
from jax.experimental.pallas import tpu as pltpu

F32 = jnp.float32
BF16 = jnp.bfloat16

D_MODEL = 2048
SEQ = 16384
N_MEM = 256
EPS = 1e-6
RET_HEADS = 4
RET_DK = 256
RET_DV = 256
RET_THETA = 10000.0
SWA_HEADS = 16
SWA_KV_HEADS = 4
SWA_HD = 64
WINDOW = 128
ROPE_THETA = 500000.0
ROPE_DIM = SWA_HD // 4
X_HEADS = 4
X_HD = 128
D_FF = -(-8 * D_MODEL // (3 * 256)) * 256
IN_SPLITS = (RET_HEADS * RET_DK, RET_HEADS * RET_DK, RET_HEADS * RET_DV, RET_HEADS * RET_DV,
             SWA_HEADS * SWA_HD, SWA_KV_HEADS * SWA_HD, SWA_KV_HEADS * SWA_HD, D_MODEL, D_MODEL)
D_IN = sum(IN_SPLITS)
IN_OFFSETS = tuple(int(o) for o in np.cumsum((0,) + IN_SPLITS[:-1]))
Q_R, K_R, V_R, G_R, Q_S, K_S, V_S, GATE_R, GATE_S = range(9)
GATE_BLOCK = 512
NEG = -1e30

LANES = 128
VMEM_LIMIT = 58 * 1024 * 1024

RET_CHUNK = 256
RET_ROWS = 1024
ROPE_ROWS = 1024
PROJ_TM, PROJ_TN = 1024, 512
SWA_ROWS = 256
MERGE_TM = 256
FFN_TM, FFN_TF = 1024, 512
FFN_SLAB = 256


def _sigmoid(x):
    return 1.0 / (1.0 + jnp.exp(-x))


def _rms(x):
    return x * lax.rsqrt(jnp.mean(x * x, axis=-1, keepdims=True) + EPS)


def _dot(a, b):
    return jnp.dot(a, b, preferred_element_type=F32)


def _dot_nt(a, b):
    return lax.dot_general(a, b, (((1,), (1,)), ((), ())), preferred_element_type=F32)


def _dot_tn(a, b):
    return lax.dot_general(a, b, (((0,), (0,)), ((), ())), preferred_element_type=F32)


def _params(*sem):
    return pltpu.CompilerParams(dimension_semantics=sem, vmem_limit_bytes=VMEM_LIMIT)


def _resident(shape):
    nd = len(shape)
    return pl.BlockSpec(shape, lambda *_: (0,) * nd, pipeline_mode=pl.Buffered(1))


def _rope_tables_kernel(pos_ref, invr_ref, invs_ref, m1_ref, m2_ref,
                        rc_ref, rs_ref, sc_ref, s1_ref, s2_ref):
    pos = pos_ref[...].astype(F32)
    ang_r = pos * invr_ref[...]
    rc_ref[...] = jnp.cos(ang_r)
    rs_ref[...] = jnp.sin(ang_r)
    ang_s = pos * invs_ref[...]
    sn = jnp.sin(ang_s)
    sc_ref[...] = jnp.cos(ang_s)
    s1_ref[...] = -sn * m1_ref[...]
    s2_ref[...] = sn * m2_ref[...]


def _rope_tables(positions):
    pos = positions.reshape(SEQ, 1)
    half_r = RET_DK // 2
    inv_r = 1.0 / (RET_THETA ** (jnp.arange(half_r, dtype=F32) / half_r))
    half_s = ROPE_DIM // 2
    inv_s = 1.0 / (ROPE_THETA ** (jnp.arange(half_s, dtype=F32) / half_s))
    d = np.arange(LANES) % SWA_HD
    inv_s_lanes = jnp.where(d < ROPE_DIM, inv_s[d % half_s], 0.0).astype(F32)
    m1 = jnp.asarray((d < half_s).astype(np.float32))
    m2 = jnp.asarray(((d >= half_s) & (d < ROPE_DIM)).astype(np.float32))
    row = lambda v: v.reshape(1, LANES)
    tab = jax.ShapeDtypeStruct((SEQ, LANES), F32)
    vec = pl.BlockSpec((1, LANES), lambda i: (0, 0))
    blk = pl.BlockSpec((ROPE_ROWS, LANES), lambda i: (i, 0))
    return pl.pallas_call(
        _rope_tables_kernel,
        grid=(SEQ // ROPE_ROWS,),
        in_specs=[pl.BlockSpec((ROPE_ROWS, 1), lambda i: (i, 0)), vec, vec, vec, vec],
        out_specs=[blk] * 5,
        out_shape=[tab] * 5,
        compiler_params=_params("parallel"),
        name="rope_tables",
    )(pos, row(inv_r), row(inv_s_lanes), row(m1), row(m2))


def _in_proj_kernel(x_ref, g_ref, w_ref, rc_ref, rs_ref, sc_ref, s1_ref, s2_ref, o_ref, n_ref, acc_ref):
    tn = PROJ_TN
    j = pl.program_id(1)
    blk = lambda split: IN_OFFSETS[split] // tn

    @pl.when(j == 0)
    def _():
        n_ref[...] = (_rms(x_ref[...]) * g_ref[...]).astype(BF16)

    acc = _dot(n_ref[...], w_ref[...].astype(BF16))
    o_ref[...] = acc.astype(BF16)
    acc_ref[...] = acc
    is_ret = (j >= blk(Q_R)) & (j < blk(V_R))
    is_qs = (j >= blk(Q_S)) & (j < blk(K_S))
    is_kvs = j == blk(K_S)

    def swa_rope(cols):
        half = ROPE_DIM // 2
        xt = acc_ref[:, cols]
        return (xt * sc_ref[...] + pltpu.roll(xt, LANES - half, 1) * s1_ref[...]
                + pltpu.roll(xt, half, 1) * s2_ref[...])

    @pl.when(is_ret)
    def _():
        scale = jnp.where(j >= blk(K_R), RET_DK ** -0.5, 1.0).astype(F32)
        cos, sin = rc_ref[...] * scale, rs_ref[...] * scale
        half = RET_DK // 2
        for h in range(tn // RET_DK):
            a1 = acc_ref[:, h * RET_DK:h * RET_DK + half]
            a2 = acc_ref[:, h * RET_DK + half:(h + 1) * RET_DK]
            o_ref[:, h * RET_DK:h * RET_DK + half] = (a1 * cos - a2 * sin).astype(BF16)
            o_ref[:, h * RET_DK + half:(h + 1) * RET_DK] = (a2 * cos + a1 * sin).astype(BF16)

    @pl.when(is_qs)
    def _():
        for t in range(tn // LANES):
            cols = slice(t * LANES, (t + 1) * LANES)
            o_ref[:, cols] = (swa_rope(cols) * (SWA_HD ** -0.5)).astype(BF16)

    @pl.when(is_kvs)
    def _():
        for t in range(IN_SPLITS[K_S] // LANES):
            cols = slice(t * LANES, (t + 1) * LANES)
            o_ref[:, cols] = swa_rope(cols).astype(BF16)


def _in_proj(x2d, g, w, tables):
    tm, tn = PROJ_TM, PROJ_TN
    assert all(IN_OFFSETS[s] % tn == 0 for s in (Q_R, K_R, V_R, Q_S, K_S))
    assert IN_OFFSETS[V_S] == IN_OFFSETS[K_S] + IN_SPLITS[K_S] and IN_SPLITS[K_S] + IN_SPLITS[V_S] == tn
    tab = pl.BlockSpec((tm, LANES), lambda i, j: (i, 0))
    return pl.pallas_call(
        _in_proj_kernel,
        grid=(SEQ // tm, D_IN // tn),
        in_specs=[pl.BlockSpec((tm, D_MODEL), lambda i, j: (i, 0)),
                  pl.BlockSpec((1, D_MODEL), lambda i, j: (0, 0)),
                  pl.BlockSpec((D_MODEL, tn), lambda i, j: (0, j)),
                  tab, tab, tab, tab, tab],
        out_specs=pl.BlockSpec((tm, tn), lambda i, j: (i, j)),
        out_shape=jax.ShapeDtypeStruct((SEQ, D_IN), BF16),
        scratch_shapes=[pltpu.VMEM((tm, D_MODEL), BF16), pltpu.VMEM((tm, tn), F32)],
        compiler_params=_params("parallel", "arbitrary"),
        name="in_proj",
    )(x2d, g.reshape(1, D_MODEL), w, *tables)


def _retention_kernel(logg_ref, q_ref, k_ref, v_ref, g_ref, o_ref, state_ref, dmask_ref):
    c_rows = RET_CHUNK
    lg = logg_ref[pl.program_id(0)]

    @pl.when(pl.program_id(1) == 0)
    def _():
        state_ref[...] = jnp.zeros_like(state_ref)
        i = lax.broadcasted_iota(jnp.int32, (c_rows, c_rows), 0)
        j = lax.broadcasted_iota(jnp.int32, (c_rows, c_rows), 1)
        rel = (i - j).astype(F32)
        dmask_ref[...] = jnp.where(rel >= 0, jnp.exp(lg * jnp.maximum(rel, 0.0)), 0.0)

    idx = lax.broadcasted_iota(jnp.int32, (c_rows, 1), 0).astype(F32)
    q_dec = jnp.exp(lg * (idx + 1.0))
    k_dec = jnp.exp(lg * (c_rows - 1.0 - idx))
    chunk_decay = jnp.exp(jnp.full((1, RET_DV), lg * c_rows, F32))

    def body(c, carry):
        r = pl.ds(pl.multiple_of(c * c_rows, c_rows), c_rows)
        q, k, v = q_ref[r, :], k_ref[r, :], v_ref[r, :]
        scores = _dot_nt(q, k) * dmask_ref[...]
        inner = _dot(scores.astype(BF16), v)
        state = state_ref[...]
        cross = _dot((q.astype(F32) * q_dec).astype(BF16), state.astype(BF16))
        state_ref[...] = state * chunk_decay + _dot_tn((k.astype(F32) * k_dec).astype(BF16), v)
        y = _rms(inner + cross)
        g = g_ref[r, :].astype(F32)
        o_ref[r, :] = (y * (g * _sigmoid(g))).astype(o_ref.dtype)
        return carry

    lax.fori_loop(0, RET_ROWS // c_rows, body, 0, unroll=True)


def _retention(proj):
    log_g = jnp.log(1.0 - jnp.power(2.0, -5.0 - jnp.arange(RET_HEADS, dtype=F32)))
    rows = RET_ROWS
    head_blk = lambda split: pl.BlockSpec(
        (rows, RET_DK), lambda h, n, c=IN_OFFSETS[split] // RET_DK: (n, c + h))
    return pl.pallas_call(
        _retention_kernel,
        grid=(RET_HEADS, SEQ // rows),
        in_specs=[pl.BlockSpec(memory_space=pltpu.SMEM),
                  head_blk(Q_R), head_blk(K_R), head_blk(V_R), head_blk(G_R)],
        out_specs=pl.BlockSpec((rows, RET_DV), lambda h, n: (n, h)),
        out_shape=jax.ShapeDtypeStruct((SEQ, RET_HEADS * RET_DV), BF16),
        scratch_shapes=[pltpu.VMEM((RET_DK, RET_DV), F32), pltpu.VMEM((RET_CHUNK, RET_CHUNK), F32)],
        compiler_params=_params("parallel", "arbitrary"),
        name="retention",
    )(log_g, proj, proj, proj, proj)


def _swa_kernel(sinks_ref, q_ref, k_ref, v_ref, kp_ref, vp_ref, o_ref):
    w = WINDOW
    hd = SWA_HD
    group = SWA_HEADS // SWA_KV_HEADS
    step = pl.program_id(0)
    low = lax.broadcasted_iota(jnp.int32, (w, LANES), 1) < hd

    def dup_halves(x):
        xf = x.astype(F32)
        sw = pltpu.roll(xf, hd, 1)
        lo = lax.broadcasted_iota(jnp.int32, xf.shape, 1) < hd
        return jnp.where(lo, xf, sw).astype(BF16), jnp.where(lo, sw, xf).astype(BF16)

    def per_kv_head(cur_ref, prev_ref):
        allrows = jnp.concatenate([prev_ref[...], cur_ref[...]], axis=0)
        heads = []
        for c in range(allrows.shape[1] // LANES):
            heads.extend(dup_halves(allrows[:, c * LANES:(c + 1) * LANES]))
        return heads

    k2 = per_kv_head(k_ref, kp_ref)
    v2 = per_kv_head(v_ref, vp_ref)

    qi = lax.broadcasted_iota(jnp.int32, (w, 2 * w), 0)
    kj = lax.broadcasted_iota(jnp.int32, (w, 2 * w), 1)
    dist = qi + w - kj
    band = (dist >= 0) & (dist < WINDOW)
    zero = jnp.zeros((w, LANES), BF16)
    ones = jnp.ones((2 * w, LANES), BF16)

    sink_slot = lax.broadcasted_iota(jnp.int32, (group * w, LANES), 1) == 0
    first_row = lax.broadcasted_iota(jnp.int32, (2 * w, LANES), 0) == 0

    for kh in range(SWA_KV_HEADS):
        cols = [slice((kh * group // 2 + p) * LANES, (kh * group // 2 + p + 1) * LANES)
                for p in range(group // 2)]
        sink = jnp.concatenate([jnp.full((w, LANES), sinks_ref[kh * group + g], F32)
                                for g in range(group)], axis=0)
        fill_prev = jnp.where(sink_slot, sink, NEG)
        for t in range(SWA_ROWS // w):
            rows = slice(t * w, (t + 1) * w)
            mask = band & ((step > 0) | (kj >= w)) if t == 0 else band
            mask = jnp.concatenate([mask] * group, axis=0)
            k_w = k2[kh][t * w:(t + 2) * w]
            v_w = jnp.where(first_row, jnp.zeros_like(ones), v2[kh][t * w:(t + 2) * w])
            parts = []
            for c in cols:
                qc = q_ref[rows, c]
                parts += [jnp.where(low, qc, zero), jnp.where(low, zero, qc)]
            raw = _dot_nt(jnp.concatenate(parts, axis=0), k_w)
            s = jnp.concatenate([jnp.where(mask[:, :w], raw[:, :w], fill_prev),
                                 jnp.where(mask[:, w:], raw[:, w:], NEG)], axis=-1)
            e = jnp.exp(s - jnp.max(s, axis=-1, keepdims=True))
            pv = _dot(e.astype(BF16), jnp.concatenate([v_w, ones], axis=-1))
            o = pv[:, :LANES] / pv[:, LANES:]
            for p, c in enumerate(cols):
                o_ref[rows, c] = jnp.where(low, o[2 * p * w:(2 * p + 1) * w],
                                           o[(2 * p + 1) * w:(2 * p + 2) * w]).astype(o_ref.dtype)


def _swa(proj, sinks):
    w, rows = WINDOW, SWA_ROWS
    nq, nkv = SWA_HEADS * SWA_HD, SWA_KV_HEADS * SWA_HD
    per = rows // w
    cur = lambda split: pl.BlockSpec((rows, nkv), lambda i, c=IN_OFFSETS[split] // nkv: (i, c))
    prev = lambda split: pl.BlockSpec(
        (w, nkv), lambda i, c=IN_OFFSETS[split] // nkv: (jnp.maximum(i * per - 1, 0), c))
    return pl.pallas_call(
        _swa_kernel,
        grid=(SEQ // rows,),
        in_specs=[pl.BlockSpec(memory_space=pltpu.SMEM),
                  pl.BlockSpec((rows, nq), lambda i: (i, IN_OFFSETS[Q_S] // nq)),
                  cur(K_S), cur(V_S), prev(K_S), prev(V_S)],
        out_specs=pl.BlockSpec((rows, nq), lambda i: (i, 0)),
        out_shape=jax.ShapeDtypeStruct((SEQ, nq), BF16),
        compiler_params=_params("parallel"),
        name="swa",
    )(sinks, proj, proj, proj, proj, proj)


def _mem_kv_kernel(mem_ref, g_ref, w_ref, o_ref):
    memn = (_rms(mem_ref[...]) * g_ref[...]).astype(BF16)
    o_ref[...] = _dot(memn, w_ref[...]).astype(o_ref.dtype)


def _mem_kv(mem2d, g, w):
    n_out = 2 * X_HEADS * X_HD
    return pl.pallas_call(
        _mem_kv_kernel,
        out_shape=jax.ShapeDtypeStruct((N_MEM, n_out), BF16),
        compiler_params=pltpu.CompilerParams(vmem_limit_bytes=VMEM_LIMIT),
        name="mem_kv",
    )(mem2d, g.reshape(1, D_MODEL), w)


def _merge_kernel(x_ref, yr_ref, ys_ref, *refs):
    n_gate = D_MODEL // GATE_BLOCK
    gr_refs, gs_refs = refs[:n_gate], refs[n_gate:2 * n_gate]
    (kv_ref, wur_ref, wus_ref, wo_ref, gx_ref, wxq_ref, wxo_ref, gf_ref, h_ref, n_ref) = refs[2 * n_gate:]
    gate_r =jnp.concatenate([r[...] for r in gr_refs], axis=-1).astype(F32)
    gate_s = jnp.concatenate([r[...] for r in gs_refs], axis=-1).astype(F32)
    up_r = _dot(yr_ref[...], wur_ref[...])
    up_s = _dot(ys_ref[...], wus_ref[...])
    merged = _sigmoid(gate_r) * up_r + _sigmoid(gate_s) * up_s
    h1 = x_ref[...] + _dot(merged.astype(BF16), wo_ref[...])

    nx = (_rms(h1) * gx_ref[...]).astype(BF16)
    qx = _dot(nx, wxq_ref[...]).astype(BF16)
    kv = kv_ref[...]
    outs = []
    for h in range(X_HEADS):
        q_h = qx[:, h * X_HD:(h + 1) * X_HD]
        k_h = kv[:, h * X_HD:(h + 1) * X_HD]
        v_h = kv[:, (X_HEADS + h) * X_HD:(X_HEADS + h + 1) * X_HD]
        s = _dot_nt(q_h, k_h) * (X_HD ** -0.5)
        e = jnp.exp(s - jnp.max(s, axis=-1, keepdims=True))
        p = e / jnp.sum(e, axis=-1, keepdims=True)
        outs.append(_dot(p.astype(BF16), v_h))
    att = jnp.concatenate(outs, axis=-1).astype(BF16)
    h2 = h1 + _dot(att, wxo_ref[...])
    h_ref[...] = h2
    n_ref[...] = (_rms(h2) * gf_ref[...]).astype(BF16)


def _merge(x2d, yr, ys, proj, kv, w_up_ret, w_up_swa, w_o, g_x, w_xq, w_xo, g_ffn):
    tm = MERGE_TM
    rows = lambda width, col=0: pl.BlockSpec((tm, width), lambda i, c=col: (i, c))
    n_gate = D_MODEL // GATE_BLOCK
    gate = lambda split: [rows(GATE_BLOCK, IN_OFFSETS[split] // GATE_BLOCK + b) for b in range(n_gate)]
    assert IN_OFFSETS[GATE_R] % GATE_BLOCK == 0 and IN_OFFSETS[GATE_S] % GATE_BLOCK == 0
    n_x = X_HEADS * X_HD
    return pl.pallas_call(
        _merge_kernel,
        grid=(SEQ // tm,),
        in_specs=[rows(D_MODEL), rows(RET_HEADS * RET_DV), rows(SWA_HEADS * SWA_HD),
                  *gate(GATE_R), *gate(GATE_S),
                  _resident((N_MEM, 2 * n_x)),
                  _resident((RET_HEADS * RET_DV, D_MODEL)), _resident((SWA_HEADS * SWA_HD, D_MODEL)),
                  _resident((D_MODEL, D_MODEL)), _resident((1, D_MODEL)),
                  _resident((D_MODEL, n_x)), _resident((n_x, D_MODEL)), _resident((1, D_MODEL))],
        out_specs=[rows(D_MODEL), rows(D_MODEL)],
        out_shape=[jax.ShapeDtypeStruct((SEQ, D_MODEL), F32),
                   jax.ShapeDtypeStruct((SEQ, D_MODEL), BF16)],
        compiler_params=_params("parallel"),
        name="merge_xattn",
    )(x2d, yr, ys, *([proj] * (2 * n_gate)), kv, w_up_ret, w_up_swa, w_o, g_x.reshape(1, D_MODEL),
      w_xq, w_xo, g_ffn.reshape(1, D_MODEL))


def _ffn_kernel(n_ref, h_hbm, wg_ref, wu_ref, wd_ref, gf_ref, o_ref, h_ref, h_sem):
    i, f = pl.program_id(0), pl.program_id(1)
    rows = pl.ds(pl.multiple_of(i * FFN_TM, FFN_TM), FFN_TM)
    residual_copy = pltpu.make_async_copy(h_hbm.at[rows, :], h_ref, h_sem)

    @pl.when(f == 0)
    def _():
        residual_copy.start()
        for c in range(FFN_TM // FFN_SLAB):
            o_ref[c * FFN_SLAB:(c + 1) * FFN_SLAB, :] = jnp.zeros((FFN_SLAB, D_MODEL), F32)

    n2 = n_ref[...]
    ts = []
    for c in range(FFN_TF // FFN_SLAB):
        cols = slice(c * FFN_SLAB, (c + 1) * FFN_SLAB)
        a = _dot(n2, wg_ref[:, cols])
        b = _dot(n2, wu_ref[:, cols])
        ts.append((a * _sigmoid(a) * b).astype(BF16))
    t = jnp.concatenate(ts, axis=-1)
    for c in range(D_MODEL // FFN_TF):
        cols = slice(c * FFN_TF, (c + 1) * FFN_TF)
        o_ref[:, cols] += _dot(t, wd_ref[:, cols])

    @pl.when(f == pl.num_programs(1) - 1)
    def _():
        residual_copy.wait()
        for c in range(FFN_TM // FFN_SLAB):
            r = slice(c * FFN_SLAB, (c + 1) * FFN_SLAB)
            o_ref[r, :] = _rms(h_ref[r, :] + o_ref[r, :]) * gf_ref[...]


def _ffn(n2, h2, w_gate, w_up, w_down, g_final):
    tm, tf = FFN_TM, FFN_TF
    return pl.pallas_call(
        _ffn_kernel,
        grid=(SEQ // tm, D_FF // tf),
        in_specs=[pl.BlockSpec((tm, D_MODEL), lambda i, f: (i, 0)),
                  pl.BlockSpec(memory_space=pl.ANY),
                  pl.BlockSpec((D_MODEL, tf), lambda i, f: (0, f)),
                  pl.BlockSpec((D_MODEL, tf), lambda i, f: (0, f)),
                  pl.BlockSpec((tf, D_MODEL), lambda i, f: (f, 0)),
                  pl.BlockSpec((1, D_MODEL), lambda i, f: (0, 0))],
        out_specs=pl.BlockSpec((tm, D_MODEL), lambda i, f: (i, 0)),
        out_shape=jax.ShapeDtypeStruct((SEQ, D_MODEL), F32),
        scratch_shapes=[pltpu.VMEM((tm, D_MODEL), F32), pltpu.SemaphoreType.DMA(())],
        compiler_params=_params("arbitrary", "arbitrary"),
        name="ffn",
    )(n2, h2, w_gate, w_up, w_down, g_final.reshape(1, D_MODEL))


def kernel(x, mem, positions, g_mix, w_in, w_up_ret, w_up_swa, sinks, w_o, g_x, g_mem,
           w_xq, w_xkv, w_xo, g_ffn, w_ffn_gate, w_ffn_up, w_ffn_down, g_final):
    assert x.shape == (1, SEQ, D_MODEL) and mem.shape == (1, N_MEM, D_MODEL)
    assert w_in.shape == (1, D_MODEL, D_IN)
    bf = lambda w: w[0].astype(BF16)
    x2d = x.reshape(SEQ, D_MODEL)
    tables = _rope_tables(positions)
    proj = _in_proj(x2d, g_mix[0], w_in[0], tables)
    yr = _retention(proj)
    ys = _swa(proj, sinks[0])
    kv = _mem_kv(mem.reshape(N_MEM, D_MODEL), g_mem[0], bf(w_xkv))
    h2, n2 = _merge(x2d, yr, ys, proj, kv, bf(w_up_ret), bf(w_up_swa), bf(w_o), g_x[0],
                    bf(w_xq), bf(w_xo), g_ffn[0])
    out = _ffn(n2, h2, bf(w_ffn_gate), bf(w_ffn_up), bf(w_ffn_down), g_final)
    return out.reshape(1, SEQ, D_MODEL)
```

```python
import jax
import jax.numpy as jnp
import numpy as np
from jax import lax
from jax.experimental import pallas as pl
from jax.experimental.pallas import tpu as pltpu

F32 = jnp.float32
BF16 = jnp.bfloat16

D_MODEL = 2048
SEQ = 16384
N_MEM = 256
EPS = 1e-6
RET_HEADS = 4
RET_DK = 256
RET_DV = 256
RET_THETA = 10000.0
SWA_HEADS = 16
SWA_KV_HEADS = 4
SWA_HD = 64
WINDOW = 128
ROPE_THETA = 500000.0
ROPE_DIM = SWA_HD // 4
X_HEADS = 4
X_HD = 128
D_FF = -(-8 * D_MODEL // (3 * 256)) * 256
IN_SPLITS = (RET_HEADS * RET_DK, RET_HEADS * RET_DK, RET_HEADS * RET_DV, RET_HEADS * RET_DV,
             SWA_HEADS * SWA_HD, SWA_KV_HEADS * SWA_HD, SWA_KV_HEADS * SWA_HD, D_MODEL, D_MODEL)
D_IN = sum(IN_SPLITS)
IN_OFFSETS = tuple(int(o) for o in np.cumsum((0,) + IN_SPLITS[:-1]))
Q_R, K_R, V_R, G_R, Q_S, K_S, V_S, GATE_R, GATE_S = range(9)
GATE_BLOCK = 512
NEG = -1e30

LANES = 128
VMEM_LIMIT = 58 * 1024 * 1024

RET_CHUNK = 256
RET_ROWS = 1024
ROPE_ROWS = 1024
PROJ_TM, PROJ_TN = 2048, 512
NORM_SLAB = 256
SWA_ROWS = 256
MERGE_TM = 256
FFN_TM, FFN_TF = 1024, 512
FFN_SLAB = 256


def _sigmoid(x):
    return 1.0 / (1.0 + jnp.exp(-x))


def _rms(x):
    return x * lax.rsqrt(jnp.mean(x * x, axis=-1, keepdims=True) + EPS)


def _dot(a, b):
    return jnp.dot(a, b, preferred_element_type=F32)


def _dot_nt(a, b):
    return lax.dot_general(a, b, (((1,), (1,)), ((), ())), preferred_element_type=F32)


def _dot_tn(a, b):
    return lax.dot_general(a, b, (((0,), (0,)), ((), ())), preferred_element_type=F32)


def _params(*sem):
    return pltpu.CompilerParams(dimension_semantics=sem, vmem_limit_bytes=VMEM_LIMIT)


def _resident(shape):
    nd = len(shape)
    return pl.BlockSpec(shape, lambda *_: (0,) * nd, pipeline_mode=pl.Buffered(1))


def _rope_tables_kernel(pos_ref, invr_ref, invs_ref, m1_ref, m2_ref,
                        rc_ref, rs_ref, sc_ref, s1_ref, s2_ref):
    pos = pos_ref[...].astype(F32)
    ang_r = pos * invr_ref[...]
    rc_ref[...] = jnp.cos(ang_r)
    rs_ref[...] = jnp.sin(ang_r)
    ang_s = pos * invs_ref[...]
    sn = jnp.sin(ang_s)
    sc_ref[...] = jnp.cos(ang_s)
    s1_ref[...] = -sn * m1_ref[...]
    s2_ref[...] = sn * m2_ref[...]


def _rope_tables(positions):
    pos = positions.reshape(SEQ, 1)
    half_r = RET_DK // 2
    inv_r = 1.0 / (RET_THETA ** (jnp.arange(half_r, dtype=F32) / half_r))
    half_s = ROPE_DIM // 2
    inv_s = 1.0 / (ROPE_THETA ** (jnp.arange(half_s, dtype=F32) / half_s))
    d = np.arange(LANES) % SWA_HD
    inv_s_lanes = jnp.where(d < ROPE_DIM, inv_s[d % half_s], 0.0).astype(F32)
    m1 = jnp.asarray((d < half_s).astype(np.float32))
    m2 = jnp.asarray(((d >= half_s) & (d < ROPE_DIM)).astype(np.float32))
    row = lambda v: v.reshape(1, LANES)
    tab = jax.ShapeDtypeStruct((SEQ, LANES), F32)
    vec = pl.BlockSpec((1, LANES), lambda i: (0, 0))
    blk = pl.BlockSpec((ROPE_ROWS, LANES), lambda i: (i, 0))
    return pl.pallas_call(
        _rope_tables_kernel,
        grid=(SEQ // ROPE_ROWS,),
        in_specs=[pl.BlockSpec((ROPE_ROWS, 1), lambda i: (i, 0)), vec, vec, vec, vec],
        out_specs=[blk] * 5,
        out_shape=[tab] * 5,
        compiler_params=_params("parallel"),
        name="rope_tables",
    )(pos, row(inv_r), row(inv_s_lanes), row(m1), row(m2))


def _in_proj_kernel(x_hbm, g_ref, w_ref, rc_ref, rs_ref, sc_ref, s1_ref, s2_ref, o_ref,
                    n_ref, acc_ref, x_buf, x_sem):
    tm, tn = PROJ_TM, PROJ_TN
    i, j = pl.program_id(0), pl.program_id(1)
    blk = lambda split: IN_OFFSETS[split] // tn

    def x_copy(block):
        rows = pl.ds(pl.multiple_of(block * tm, tm), tm)
        return pltpu.make_async_copy(x_hbm.at[rows, :], x_buf, x_sem)

    @pl.when(j == 0)
    def _():
        @pl.when(i == 0)
        def _():
            x_copy(0).start()

        x_copy(i).wait()
        for c in range(tm // NORM_SLAB):
            r = slice(c * NORM_SLAB, (c + 1) * NORM_SLAB)
            n_ref[r, :] = (_rms(x_buf[r, :]) * g_ref[...]).astype(BF16)

    @pl.when((j == 1) & (i + 1 < pl.num_programs(0)))
    def _():
        x_copy(i + 1).start()

    acc = _dot(n_ref[...], w_ref[...])
    o_ref[...] = acc.astype(BF16)
    acc_ref[...] = acc
    is_ret = (j >= blk(Q_R)) & (j < blk(V_R))
    is_qs = (j >= blk(Q_S)) & (j < blk(K_S))
    is_kvs = j == blk(K_S)

    def swa_rope(cols):
        half = ROPE_DIM // 2
        xt = acc_ref[:, cols]
        return (xt * sc_ref[...] + pltpu.roll(xt, LANES - half, 1) * s1_ref[...]
                + pltpu.roll(xt, half, 1) * s2_ref[...])

    @pl.when(is_ret)
    def _():
        scale = jnp.where(j >= blk(K_R), RET_DK ** -0.5, 1.0).astype(F32)
        cos, sin = rc_ref[...] * scale, rs_ref[...] * scale
        half = RET_DK // 2
        for h in range(tn // RET_DK):
            a1 = acc_ref[:, h * RET_DK:h * RET_DK + half]
            a2 = acc_ref[:, h * RET_DK + half:(h + 1) * RET_DK]
            o_ref[:, h * RET_DK:h * RET_DK + half] = (a1 * cos - a2 * sin).astype(BF16)
            o_ref[:, h * RET_DK + half:(h + 1) * RET_DK] = (a2 * cos + a1 * sin).astype(BF16)

    @pl.when(is_qs)
    def _():
        for t in range(tn // LANES):
            cols = slice(t * LANES, (t + 1) * LANES)
            o_ref[:, cols] = (swa_rope(cols) * (SWA_HD ** -0.5)).astype(BF16)

    @pl.when(is_kvs)
    def _():
        for t in range(IN_SPLITS[K_S] // LANES):
            cols = slice(t * LANES, (t + 1) * LANES)
            o_ref[:, cols] = swa_rope(cols).astype(BF16)


def _in_proj(x2d, g, w, tables):
    tm, tn = PROJ_TM, PROJ_TN
    assert all(IN_OFFSETS[s] % tn == 0 for s in (Q_R, K_R, V_R, Q_S, K_S))
    assert IN_OFFSETS[V_S] == IN_OFFSETS[K_S] + IN_SPLITS[K_S] and IN_SPLITS[K_S] + IN_SPLITS[V_S] == tn
    tab = pl.BlockSpec((tm, LANES), lambda i, j: (i, 0))
    return pl.pallas_call(
        _in_proj_kernel,
        grid=(SEQ // tm, D_IN // tn),
        in_specs=[pl.BlockSpec(memory_space=pl.ANY),
                  pl.BlockSpec((1, D_MODEL), lambda i, j: (0, 0)),
                  pl.BlockSpec((D_MODEL, tn), lambda i, j: (0, j)),
                  tab, tab, tab, tab, tab],
        out_specs=pl.BlockSpec((tm, tn), lambda i, j: (i, j)),
        out_shape=jax.ShapeDtypeStruct((SEQ, D_IN), BF16),
        scratch_shapes=[pltpu.VMEM((tm, D_MODEL), BF16), pltpu.VMEM((tm, tn), F32),
                        pltpu.VMEM((tm, D_MODEL), F32), pltpu.SemaphoreType.DMA(())],
        compiler_params=_params("arbitrary", "arbitrary"),
        name="in_proj",
    )(x2d, g.reshape(1, D_MODEL), w, *tables)


def _retention_kernel(logg_ref, q_ref, k_ref, v_ref, g_ref, o_ref, state_ref, dmask_ref):
    c_rows = RET_CHUNK
    lg = logg_ref[pl.program_id(0)]

    @pl.when(pl.program_id(1) == 0)
    def _():
        state_ref[...] = jnp.zeros_like(state_ref)
        i = lax.broadcasted_iota(jnp.int32, (c_rows, c_rows), 0)
        j = lax.broadcasted_iota(jnp.int32, (c_rows, c_rows), 1)
        rel = (i - j).astype(F32)
        dmask_ref[...] = jnp.where(rel >= 0, jnp.exp(lg * jnp.maximum(rel, 0.0)), 0.0)

    idx = lax.broadcasted_iota(jnp.int32, (c_rows, 1), 0).astype(F32)
    q_dec = jnp.exp(lg * (idx + 1.0))
    k_dec = jnp.exp(lg * (c_rows - 1.0 - idx))
    chunk_decay = jnp.exp(jnp.full((1, RET_DV), lg * c_rows, F32))

    def body(c, carry):
        r = pl.ds(pl.multiple_of(c * c_rows, c_rows), c_rows)
        q, k, v = q_ref[r, :], k_ref[r, :], v_ref[r, :]
        scores = _dot_nt(q, k) * dmask_ref[...]
        inner = _dot(scores.astype(BF16), v)
        state = state_ref[...]
        cross = _dot((q.astype(F32) * q_dec).astype(BF16), state.astype(BF16))
        state_ref[...] = state * chunk_decay + _dot_tn((k.astype(F32) * k_dec).astype(BF16), v)
        y = _rms(inner + cross)
        g = g_ref[r, :].astype(F32)
        o_ref[r, :] = (y * (g * _sigmoid(g))).astype(o_ref.dtype)
        return carry

    lax.fori_loop(0, RET_ROWS // c_rows, body, 0, unroll=True)


def _retention(proj):
    log_g = jnp.log(1.0 - jnp.power(2.0, -5.0 - jnp.arange(RET_HEADS, dtype=F32)))
    rows = RET_ROWS
    head_blk = lambda split: pl.BlockSpec(
        (rows, RET_DK), lambda h, n, c=IN_OFFSETS[split] // RET_DK: (n, c + h))
    return pl.pallas_call(
        _retention_kernel,
        grid=(RET_HEADS, SEQ // rows),
        in_specs=[pl.BlockSpec(memory_space=pltpu.SMEM),
                  head_blk(Q_R), head_blk(K_R), head_blk(V_R), head_blk(G_R)],
        out_specs=pl.BlockSpec((rows, RET_DV), lambda h, n: (n, h)),
        out_shape=jax.ShapeDtypeStruct((SEQ, RET_HEADS * RET_DV), BF16),
        scratch_shapes=[pltpu.VMEM((RET_DK, RET_DV), F32), pltpu.VMEM((RET_CHUNK, RET_CHUNK), F32)],
        compiler_params=_params("parallel", "arbitrary"),
        name="retention",
    )(log_g, proj, proj, proj, proj)


def _swa_kernel(sinks_ref, q_ref, k_ref, v_ref, kp_ref, vp_ref, o_ref):
    w = WINDOW
    hd = SWA_HD
    group = SWA_HEADS // SWA_KV_HEADS
    step = pl.program_id(0)
    low = lax.broadcasted_iota(jnp.int32, (w, LANES), 1) < hd

    def dup_halves(x):
        xf = x.astype(F32)
        sw = pltpu.roll(xf, hd, 1)
        lo = lax.broadcasted_iota(jnp.int32, xf.shape, 1) < hd
        return jnp.where(lo, xf, sw).astype(BF16), jnp.where(lo, sw, xf).astype(BF16)

    def per_kv_head(cur_ref, prev_ref):
        allrows = jnp.concatenate([prev_ref[...], cur_ref[...]], axis=0)
        heads = []
        for c in range(allrows.shape[1] // LANES):
            heads.extend(dup_halves(allrows[:, c * LANES:(c + 1) * LANES]))
        return heads

    k2 = per_kv_head(k_ref, kp_ref)
    v2 = per_kv_head(v_ref, vp_ref)

    qi = lax.broadcasted_iota(jnp.int32, (w, 2 * w), 0)
    kj = lax.broadcasted_iota(jnp.int32, (w, 2 * w), 1)
    dist = qi + w - kj
    band = (dist >= 0) & (dist < WINDOW)
    zero = jnp.zeros((w, LANES), BF16)
    ones = jnp.ones((2 * w, LANES), BF16)

    sink_slot = lax.broadcasted_iota(jnp.int32, (group * w, LANES), 1) == 0
    first_row = lax.broadcasted_iota(jnp.int32, (2 * w, LANES), 0) == 0

    for kh in range(SWA_KV_HEADS):
        cols = [slice((kh * group // 2 + p) * LANES, (kh * group // 2 + p + 1) * LANES)
                for p in range(group // 2)]
        sink = jnp.concatenate([jnp.full((w, LANES), sinks_ref[kh * group + g], F32)
                                for g in range(group)], axis=0)
        fill_prev = jnp.where(sink_slot, sink, NEG)
        for t in range(SWA_ROWS // w):
            rows = slice(t * w, (t + 1) * w)
            mask = band & ((step > 0) | (kj >= w)) if t == 0 else band
            mask = jnp.concatenate([mask] * group, axis=0)
            k_w = k2[kh][t * w:(t + 2) * w]
            v_w = jnp.where(first_row, jnp.zeros_like(ones), v2[kh][t * w:(t + 2) * w])
            parts = []
            for c in cols:
                qc = q_ref[rows, c]
                parts += [jnp.where(low, qc, zero), jnp.where(low, zero, qc)]
            raw = _dot_nt(jnp.concatenate(parts, axis=0), k_w)
            s = jnp.concatenate([jnp.where(mask[:, :w], raw[:, :w], fill_prev),
                                 jnp.where(mask[:, w:], raw[:, w:], NEG)], axis=-1)
            e = jnp.exp(s - jnp.max(s, axis=-1, keepdims=True))
            pv = _dot(e.astype(BF16), jnp.concatenate([v_w, ones], axis=-1))
            o = pv[:, :LANES] / pv[:, LANES:]
            for p, c in enumerate(cols):
                o_ref[rows, c] = jnp.where(low, o[2 * p * w:(2 * p + 1) * w],
                                           o[(2 * p + 1) * w:(2 * p + 2) * w]).astype(o_ref.dtype)


def _swa(proj, sinks):
    w, rows = WINDOW, SWA_ROWS
    nq, nkv = SWA_HEADS * SWA_HD, SWA_KV_HEADS * SWA_HD
    per = rows // w
    cur = lambda split: pl.BlockSpec((rows, nkv), lambda i, c=IN_OFFSETS[split] // nkv: (i, c))
    prev = lambda split: pl.BlockSpec(
        (w, nkv), lambda i, c=IN_OFFSETS[split] // nkv: (jnp.maximum(i * per - 1, 0), c))
    return pl.pallas_call(
        _swa_kernel,
        grid=(SEQ // rows,),
        in_specs=[pl.BlockSpec(memory_space=pltpu.SMEM),
                  pl.BlockSpec((rows, nq), lambda i: (i, IN_OFFSETS[Q_S] // nq)),
                  cur(K_S), cur(V_S), prev(K_S), prev(V_S)],
        out_specs=pl.BlockSpec((rows, nq), lambda i: (i, 0)),
        out_shape=jax.ShapeDtypeStruct((SEQ, nq), BF16),
        compiler_params=_params("parallel"),
        name="swa",
    )(sinks, proj, proj, proj, proj, proj)


def _mem_kv_kernel(mem_ref, g_ref, w_ref, o_ref):
    memn = (_rms(mem_ref[...]) * g_ref[...]).astype(BF16)
    o_ref[...] = _dot(memn, w_ref[...]).astype(o_ref.dtype)


def _mem_kv(mem2d, g, w):
    n_out = 2 * X_HEADS * X_HD
    return pl.pallas_call(
        _mem_kv_kernel,
        out_shape=jax.ShapeDtypeStruct((N_MEM, n_out), BF16),
        compiler_params=pltpu.CompilerParams(vmem_limit_bytes=VMEM_LIMIT),
        name="mem_kv",
    )(mem2d, g.reshape(1, D_MODEL), w)


def _merge_kernel(x_ref, yr_ref, ys_ref, *refs):
    n_gate = D_MODEL // GATE_BLOCK
    gr_refs, gs_refs = refs[:n_gate], refs[n_gate:2 * n_gate]
    (kv_ref, wur_ref, wus_ref, wo_ref, gx_ref, wxq_ref, wxo_ref, gf_ref, h_ref, n_ref) = refs[2 * n_gate:]
    gate_r =jnp.concatenate([r[...] for r in gr_refs], axis=-1).astype(F32)
    gate_s = jnp.concatenate([r[...] for r in gs_refs], axis=-1).astype(F32)
    up_r = _dot(yr_ref[...], wur_ref[...])
    up_s = _dot(ys_ref[...], wus_ref[...])
    merged = _sigmoid(gate_r) * up_r + _sigmoid(gate_s) * up_s
    h1 = x_ref[...] + _dot(merged.astype(BF16), wo_ref[...])

    nx = (_rms(h1) * gx_ref[...]).astype(BF16)
    qx = _dot(nx, wxq_ref[...]).astype(BF16)
    kv = kv_ref[...]
    outs = []
    for h in range(X_HEADS):
        q_h = qx[:, h * X_HD:(h + 1) * X_HD]
        k_h = kv[:, h * X_HD:(h + 1) * X_HD]
        v_h = kv[:, (X_HEADS + h) * X_HD:(X_HEADS + h + 1) * X_HD]
        s = _dot_nt(q_h, k_h) * (X_HD ** -0.5)
        e = jnp.exp(s - jnp.max(s, axis=-1, keepdims=True))
        p = e / jnp.sum(e, axis=-1, keepdims=True)
        outs.append(_dot(p.astype(BF16), v_h))
    att = jnp.concatenate(outs, axis=-1).astype(BF16)
    h2 = h1 + _dot(att, wxo_ref[...])
    h_ref[...] = h2
    n_ref[...] = (_rms(h2) * gf_ref[...]).astype(BF16)


def _merge(x2d, yr, ys, proj, kv, w_up_ret, w_up_swa, w_o, g_x, w_xq, w_xo, g_ffn):
    tm = MERGE_TM
    rows = lambda width, col=0: pl.BlockSpec((tm, width), lambda i, c=col: (i, c))
    n_gate = D_MODEL // GATE_BLOCK
    gate = lambda split: [rows(GATE_BLOCK, IN_OFFSETS[split] // GATE_BLOCK + b) for b in range(n_gate)]
    assert IN_OFFSETS[GATE_R] % GATE_BLOCK == 0 and IN_OFFSETS[GATE_S] % GATE_BLOCK == 0
    n_x = X_HEADS * X_HD
    return pl.pallas_call(
        _merge_kernel,
        grid=(SEQ // tm,),
        in_specs=[rows(D_MODEL), rows(RET_HEADS * RET_DV), rows(SWA_HEADS * SWA_HD),
                  *gate(GATE_R), *gate(GATE_S),
                  _resident((N_MEM, 2 * n_x)),
                  _resident((RET_HEADS * RET_DV, D_MODEL)), _resident((SWA_HEADS * SWA_HD, D_MODEL)),
                  _resident((D_MODEL, D_MODEL)), _resident((1, D_MODEL)),
                  _resident((D_MODEL, n_x)), _resident((n_x, D_MODEL)), _resident((1, D_MODEL))],
        out_specs=[rows(D_MODEL), rows(D_MODEL)],
        out_shape=[jax.ShapeDtypeStruct((SEQ, D_MODEL), F32),
                   jax.ShapeDtypeStruct((SEQ, D_MODEL), BF16)],
        compiler_params=_params("parallel"),
        name="merge_xattn",
    )(x2d, yr, ys, *([proj] * (2 * n_gate)), kv, w_up_ret, w_up_swa, w_o, g_x.reshape(1, D_MODEL),
      w_xq, w_xo, g_ffn.reshape(1, D_MODEL))


def _ffn_kernel(n_ref, h_hbm, wg_ref, wu_ref, wd_ref, gf_ref, o_ref, h_ref, h_sem):
    i, f = pl.program_id(0), pl.program_id(1)
    rows = pl.ds(pl.multiple_of(i * FFN_TM, FFN_TM), FFN_TM)
    residual_copy = pltpu.make_async_copy(h_hbm.at[rows, :], h_ref, h_sem)

    @pl.when(f == 0)
    def _():
        residual_copy.start()
        for c in range(FFN_TM // FFN_SLAB):
            o_ref[c * FFN_SLAB:(c + 1) * FFN_SLAB, :] = jnp.zeros((FFN_SLAB, D_MODEL), F32)

    n2 = n_ref[...]
    ts = []
    for c in range(FFN_TF // FFN_SLAB):
        cols = slice(c * FFN_SLAB, (c + 1) * FFN_SLAB)
        a = _dot(n2, wg_ref[:, cols])
        b = _dot(n2, wu_ref[:, cols])
        ts.append((a * _sigmoid(a) * b).astype(BF16))
    t = jnp.concatenate(ts, axis=-1)
    for c in range(D_MODEL // FFN_TF):
        cols = slice(c * FFN_TF, (c + 1) * FFN_TF)
        o_ref[:, cols] += _dot(t, wd_ref[:, cols])

    @pl.when(f == pl.num_programs(1) - 1)
    def _():
        residual_copy.wait()
        for c in range(FFN_TM // FFN_SLAB):
            r = slice(c * FFN_SLAB, (c + 1) * FFN_SLAB)
            o_ref[r, :] = _rms(h_ref[r, :] + o_ref[r, :]) * gf_ref[...]


def _ffn(n2, h2, w_gate, w_up, w_down, g_final):
    tm, tf = FFN_TM, FFN_TF
    return pl.pallas_call(
        _ffn_kernel,
        grid=(SEQ // tm, D_FF // tf),
        in_specs=[pl.BlockSpec((tm, D_MODEL), lambda i, f: (i, 0)),
                  pl.BlockSpec(memory_space=pl.ANY),
                  pl.BlockSpec((D_MODEL, tf), lambda i, f: (0, f)),
                  pl.BlockSpec((D_MODEL, tf), lambda i, f: (0, f)),
                  pl.BlockSpec((tf, D_MODEL), lambda i, f: (f, 0)),
                  pl.BlockSpec((1, D_MODEL), lambda i, f: (0, 0))],
        out_specs=pl.BlockSpec((tm, D_MODEL), lambda i, f: (i, 0)),
        out_shape=jax.ShapeDtypeStruct((SEQ, D_MODEL), F32),
        scratch_shapes=[pltpu.VMEM((tm, D_MODEL), F32), pltpu.SemaphoreType.DMA(())],
        compiler_params=_params("arbitrary", "arbitrary"),
        name="ffn",
    )(n2, h2, w_gate, w_up, w_down, g_final.reshape(1, D_MODEL))


def kernel(x, mem, positions, g_mix, w_in, w_up_ret, w_up_swa, sinks, w_o, g_x, g_mem,
           w_xq, w_xkv, w_xo, g_ffn, w_ffn_gate, w_ffn_up, w_ffn_down, g_final):
    assert x.shape == (1, SEQ, D_MODEL) and mem.shape == (1, N_MEM, D_MODEL)
    assert w_in.shape == (1, D_MODEL, D_IN)
    bf = lambda w: w[0].astype(BF16)
    x2d = x.reshape(SEQ, D_MODEL)
    tables = _rope_tables(positions)
    proj = _in_proj(x2d, g_mix[0], bf(w_in), tables)
    yr = _retention(proj)
    ys = _swa(proj, sinks[0])
    kv = _mem_kv(mem.reshape(N_MEM, D_MODEL), g_mem[0], bf(w_xkv))
    h2, n2 = _merge(x2d, yr, ys, proj, kv, bf(w_up_ret), bf(w_up_swa), bf(w_o), g_x[0],
                    bf(w_xq), bf(w_xo), g_ffn[0])
    out = _ffn(n2, h2, bf(w_ffn_gate), bf(w_ffn_up), bf(w_ffn_down), g_final)
    return out.reshape(1, SEQ, D_MODEL)
```

```python
import jax
import jax.numpy as jnp
import numpy as np
from jax import lax
from jax.experimental import pallas as pl
from jax.experimental.pallas import tpu as pltpu

F32 = jnp.float32
BF16 = jnp.bfloat16

D_MODEL = 2048
SEQ = 16384
N_MEM = 256
EPS = 1e-6
RET_HEADS = 4
RET_DK = 256
RET_DV = 256
RET_THETA = 10000.0
SWA_HEADS = 16
SWA_KV_HEADS = 4
SWA_HD = 64
WINDOW = 128
ROPE_THETA = 500000.0
ROPE_DIM = SWA_HD // 4
X_HEADS = 4
X_HD = 128
D_FF = -(-8 * D_MODEL // (3 * 256)) * 256
IN_SPLITS = (RET_HEADS * RET_DK, RET_HEADS * RET_DK, RET_HEADS * RET_DV, RET_HEADS * RET_DV,
             SWA_HEADS * SWA_HD, SWA_KV_HEADS * SWA_HD, SWA_KV_HEADS * SWA_HD, D_MODEL, D_MODEL)
D_IN = sum(IN_SPLITS)
IN_OFFSETS = tuple(int(o) for o in np.cumsum((0,) + IN_SPLITS[:-1]))
Q_R, K_R, V_R, G_R, Q_S, K_S, V_S, GATE_R, GATE_S = range(9)
GATE_BLOCK = 512
NEG = -1e30

LANES = 128
VMEM_LIMIT = 58 * 1024 * 1024

RET_CHUNK = 256
RET_ROWS = 1024
ROPE_ROWS = 1024
PROJ_TM, PROJ_TN = 2048, 512
NORM_SLAB = 256
SWA_ROWS = 512
MERGE_TM = 256
FFN_TM, FFN_TF = 1024, 512
FFN_SLAB = 256


def _sigmoid(x):
    return 1.0 / (1.0 + jnp.exp(-x))


def _rms(x):
    return x * lax.rsqrt(jnp.mean(x * x, axis=-1, keepdims=True) + EPS)


def _dot(a, b):
    return jnp.dot(a, b, preferred_element_type=F32)


def _dot_nt(a, b):
    return lax.dot_general(a, b, (((1,), (1,)), ((), ())), preferred_element_type=F32)


def _dot_tn(a, b):
    return lax.dot_general(a, b, (((0,), (0,)), ((), ())), preferred_element_type=F32)


def _params(*sem):
    return pltpu.CompilerParams(dimension_semantics=sem, vmem_limit_bytes=VMEM_LIMIT)


def _resident(shape):
    nd = len(shape)
    return pl.BlockSpec(shape, lambda *_: (0,) * nd, pipeline_mode=pl.Buffered(1))


def _rope_tables_kernel(pos_ref, invr_ref, invs_ref, m1_ref, m2_ref,
                        rc_ref, rs_ref, sc_ref, s1_ref, s2_ref):
    pos = pos_ref[...].astype(F32)
    ang_r = pos * invr_ref[...]
    rc_ref[...] = jnp.cos(ang_r)
    rs_ref[...] = jnp.sin(ang_r)
    ang_s = pos * invs_ref[...]
    sn = jnp.sin(ang_s)
    sc_ref[...] = jnp.cos(ang_s)
    s1_ref[...] = -sn * m1_ref[...]
    s2_ref[...] = sn * m2_ref[...]


def _rope_tables(positions):
    pos = positions.reshape(SEQ, 1)
    half_r = RET_DK // 2
    inv_r = 1.0 / (RET_THETA ** (jnp.arange(half_r, dtype=F32) / half_r))
    half_s = ROPE_DIM // 2
    inv_s = 1.0 / (ROPE_THETA ** (jnp.arange(half_s, dtype=F32) / half_s))
    d = np.arange(LANES) % SWA_HD
    inv_s_lanes = jnp.where(d < ROPE_DIM, inv_s[d % half_s], 0.0).astype(F32)
    m1 = jnp.asarray((d < half_s).astype(np.float32))
    m2 = jnp.asarray(((d >= half_s) & (d < ROPE_DIM)).astype(np.float32))
    row = lambda v: v.reshape(1, LANES)
    tab = jax.ShapeDtypeStruct((SEQ, LANES), F32)
    vec = pl.BlockSpec((1, LANES), lambda i: (0, 0))
    blk = pl.BlockSpec((ROPE_ROWS, LANES), lambda i: (i, 0))
    return pl.pallas_call(
        _rope_tables_kernel,
        grid=(SEQ // ROPE_ROWS,),
        in_specs=[pl.BlockSpec((ROPE_ROWS, 1), lambda i: (i, 0)), vec, vec, vec, vec],
        out_specs=[blk] * 5,
        out_shape=[tab] * 5,
        compiler_params=_params("parallel"),
        name="rope_tables",
    )(pos, row(inv_r), row(inv_s_lanes), row(m1), row(m2))


def _in_proj_kernel(x_hbm, g_ref, w_ref, rc_ref, rs_ref, sc_ref, s1_ref, s2_ref, o_ref,
                    n_ref, acc_ref, x_buf, x_sem):
    tm, tn = PROJ_TM, PROJ_TN
    i, j = pl.program_id(0), pl.program_id(1)
    blk = lambda split: IN_OFFSETS[split] // tn

    def x_copy(block):
        rows = pl.ds(pl.multiple_of(block * tm, tm), tm)
        return pltpu.make_async_copy(x_hbm.at[rows, :], x_buf, x_sem)

    @pl.when(j == 0)
    def _():
        @pl.when(i == 0)
        def _():
            x_copy(0).start()

        x_copy(i).wait()
        for c in range(tm // NORM_SLAB):
            r = slice(c * NORM_SLAB, (c + 1) * NORM_SLAB)
            n_ref[r, :] = (_rms(x_buf[r, :]) * g_ref[...]).astype(BF16)

    @pl.when((j == 1) & (i + 1 < pl.num_programs(0)))
    def _():
        x_copy(i + 1).start()

    acc = _dot(n_ref[...], w_ref[...].astype(BF16))
    o_ref[...] = acc.astype(BF16)
    acc_ref[...] = acc
    is_ret = (j >= blk(Q_R)) & (j < blk(V_R))
    is_qs = (j >= blk(Q_S)) & (j < blk(K_S))
    is_kvs = j == blk(K_S)

    def swa_rope(cols):
        half = ROPE_DIM // 2
        xt = acc_ref[:, cols]
        return (xt * sc_ref[...] + pltpu.roll(xt, LANES - half, 1) * s1_ref[...]
                + pltpu.roll(xt, half, 1) * s2_ref[...])

    @pl.when(is_ret)
    def _():
        scale = jnp.where(j >= blk(K_R), RET_DK ** -0.5, 1.0).astype(F32)
        cos, sin = rc_ref[...] * scale, rs_ref[...] * scale
        half = RET_DK // 2
        for h in range(tn // RET_DK):
            a1 = acc_ref[:, h * RET_DK:h * RET_DK + half]
            a2 = acc_ref[:, h * RET_DK + half:(h + 1) * RET_DK]
            o_ref[:, h * RET_DK:h * RET_DK + half] = (a1 * cos - a2 * sin).astype(BF16)
            o_ref[:, h * RET_DK + half:(h + 1) * RET_DK] = (a2 * cos + a1 * sin).astype(BF16)

    @pl.when(is_qs)
    def _():
        for t in range(tn // LANES):
            cols = slice(t * LANES, (t + 1) * LANES)
            o_ref[:, cols] = (swa_rope(cols) * (SWA_HD ** -0.5)).astype(BF16)

    @pl.when(is_kvs)
    def _():
        for t in range(IN_SPLITS[K_S] // LANES):
            cols = slice(t * LANES, (t + 1) * LANES)
            o_ref[:, cols] = swa_rope(cols).astype(BF16)


def _in_proj(x2d, g, w, tables):
    tm, tn = PROJ_TM, PROJ_TN
    assert all(IN_OFFSETS[s] % tn == 0 for s in (Q_R, K_R, V_R, Q_S, K_S))
    assert IN_OFFSETS[V_S] == IN_OFFSETS[K_S] + IN_SPLITS[K_S] and IN_SPLITS[K_S] + IN_SPLITS[V_S] == tn
    tab = pl.BlockSpec((tm, LANES), lambda i, j: (i, 0))
    return pl.pallas_call(
        _in_proj_kernel,
        grid=(SEQ // tm, D_IN // tn),
        in_specs=[pl.BlockSpec(memory_space=pl.ANY),
                  pl.BlockSpec((1, D_MODEL), lambda i, j: (0, 0)),
                  pl.BlockSpec((D_MODEL, tn), lambda i, j: (0, j)),
                  tab, tab, tab, tab, tab],
        out_specs=pl.BlockSpec((tm, tn), lambda i, j: (i, j)),
        out_shape=jax.ShapeDtypeStruct((SEQ, D_IN), BF16),
        scratch_shapes=[pltpu.VMEM((tm, D_MODEL), BF16), pltpu.VMEM((tm, tn), F32),
                        pltpu.VMEM((tm, D_MODEL), F32), pltpu.SemaphoreType.DMA(())],
        compiler_params=_params("arbitrary", "arbitrary"),
        name="in_proj",
    )(x2d, g.reshape(1, D_MODEL), w, *tables)


def _retention_kernel(logg_ref, q_ref, k_ref, v_ref, g_ref, o_ref, state_ref, dmask_ref):
    c_rows = RET_CHUNK
    idx = lax.broadcasted_iota(jnp.int32, (c_rows, 1), 0).astype(F32)

    @pl.when(pl.program_id(0) == 0)
    def _():
        state_ref[...] = jnp.zeros_like(state_ref)
        i = lax.broadcasted_iota(jnp.int32, (c_rows, c_rows), 0)
        j = lax.broadcasted_iota(jnp.int32, (c_rows, c_rows), 1)
        rel = (i - j).astype(F32)
        for h in range(RET_HEADS):
            dmask_ref[h] = jnp.where(rel >= 0, jnp.exp(logg_ref[h] * jnp.maximum(rel, 0.0)), 0.0)

    for h in range(RET_HEADS):
        lg = logg_ref[h]
        cols = slice(h * RET_DK, (h + 1) * RET_DK)
        q_dec = jnp.exp(lg * (idx + 1.0))
        k_dec = jnp.exp(lg * (c_rows - 1.0 - idx))
        chunk_decay = jnp.exp(jnp.full((1, RET_DV), lg * c_rows, F32))
        for c in range(RET_ROWS // c_rows):
            r = slice(c * c_rows, (c + 1) * c_rows)
            q, k, v = q_ref[r, cols], k_ref[r, cols], v_ref[r, cols]
            scores = _dot_nt(q, k) * dmask_ref[h]
            inner = _dot(scores.astype(BF16), v)
            state = state_ref[h]
            cross = _dot((q.astype(F32) * q_dec).astype(BF16), state.astype(BF16))
            state_ref[h] = state * chunk_decay + _dot_tn((k.astype(F32) * k_dec).astype(BF16), v)
            y = _rms(inner + cross)
            g = g_ref[r, cols].astype(F32)
            o_ref[r, cols] = (y * (g * _sigmoid(g))).astype(o_ref.dtype)


def _retention(proj):
    log_g = jnp.log(1.0 - jnp.power(2.0, -5.0 - jnp.arange(RET_HEADS, dtype=F32)))
    rows, width = RET_ROWS, RET_HEADS * RET_DK
    assert RET_DK == RET_DV
    all_heads = lambda split: pl.BlockSpec((rows, width), lambda n, c=IN_OFFSETS[split] // width: (n, c))
    assert all(IN_OFFSETS[s] % width == 0 for s in (Q_R, K_R, V_R, G_R))
    return pl.pallas_call(
        _retention_kernel,
        grid=(SEQ // rows,),
        in_specs=[pl.BlockSpec(memory_space=pltpu.SMEM),
                  all_heads(Q_R), all_heads(K_R), all_heads(V_R), all_heads(G_R)],
        out_specs=pl.BlockSpec((rows, width), lambda n: (n, 0)),
        out_shape=jax.ShapeDtypeStruct((SEQ, width), BF16),
        scratch_shapes=[pltpu.VMEM((RET_HEADS, RET_DK, RET_DV), F32),
                        pltpu.VMEM((RET_HEADS, RET_CHUNK, RET_CHUNK), F32)],
        compiler_params=_params("arbitrary"),
        name="retention",
    )(log_g, proj, proj, proj, proj)


def _swa_kernel(sinks_ref, q_ref, k_ref, v_ref, kp_ref, vp_ref, o_ref):
    w = WINDOW
    hd = SWA_HD
    group = SWA_HEADS // SWA_KV_HEADS
    step = pl.program_id(0)
    low = lax.broadcasted_iota(jnp.int32, (w, LANES), 1) < hd

    def dup_halves(x):
        xf = x.astype(F32)
        sw = pltpu.roll(xf, hd, 1)
        lo = lax.broadcasted_iota(jnp.int32, xf.shape, 1) < hd
        return jnp.where(lo, xf, sw).astype(BF16), jnp.where(lo, sw, xf).astype(BF16)

    def per_kv_head(cur_ref, prev_ref):
        allrows = jnp.concatenate([prev_ref[...], cur_ref[...]], axis=0)
        heads = []
        for c in range(allrows.shape[1] // LANES):
            heads.extend(dup_halves(allrows[:, c * LANES:(c + 1) * LANES]))
        return heads

    k2 = per_kv_head(k_ref, kp_ref)
    v2 = per_kv_head(v_ref, vp_ref)

    qi = lax.broadcasted_iota(jnp.int32, (w, 2 * w), 0)
    kj = lax.broadcasted_iota(jnp.int32, (w, 2 * w), 1)
    dist = qi + w - kj
    band = (dist >= 0) & (dist < WINDOW)
    zero = jnp.zeros((w, LANES), BF16)
    ones = jnp.ones((2 * w, LANES), BF16)
    zero_kv = jnp.zeros((2 * w, LANES), BF16)

    tile = lambda a: jnp.concatenate([a] * group, axis=0)
    bias_cur = tile(jnp.where(band[:, w:], 0.0, NEG))
    bias_prev = tile(jnp.where(band[:, :w], 0.0, NEG))
    sink_slot = lax.broadcasted_iota(jnp.int32, (group * w, LANES), 1) == 0
    first_row = lax.broadcasted_iota(jnp.int32, (2 * w, LANES), 0) == 0

    for kh in range(SWA_KV_HEADS):
        cols = [slice((kh * group // 2 + p) * LANES, (kh * group // 2 + p + 1) * LANES)
                for p in range(group // 2)]
        sink = jnp.concatenate([jnp.full((w, LANES), sinks_ref[kh * group + g], F32)
                                for g in range(group)], axis=0)
        bias_later = jnp.concatenate([jnp.where(sink_slot, sink, bias_prev), bias_cur], axis=-1)
        bias_first = jnp.concatenate(
            [jnp.where(sink_slot, sink, jnp.where(step > 0, bias_prev, NEG)), bias_cur], axis=-1)
        for t in range(SWA_ROWS // w):
            rows = slice(t * w, (t + 1) * w)
            bias = bias_first if t == 0 else bias_later
            k_w = jnp.where(first_row, zero_kv, k2[kh][t * w:(t + 2) * w])
            v_w = jnp.where(first_row, zero_kv, v2[kh][t * w:(t + 2) * w])
            parts = []
            for c in cols:
                qc = q_ref[rows, c]
                parts += [jnp.where(low, qc, zero), jnp.where(low, zero, qc)]
            s = _dot_nt(jnp.concatenate(parts, axis=0), k_w) + bias
            e = jnp.exp(s - jnp.max(s, axis=-1, keepdims=True))
            pv = _dot(e.astype(BF16), jnp.concatenate([v_w, ones], axis=-1))
            o = pv[:, :LANES] / pv[:, LANES:]
            for p, c in enumerate(cols):
                o_ref[rows, c] = jnp.where(low, o[2 * p * w:(2 * p + 1) * w],
                                           o[(2 * p + 1) * w:(2 * p + 2) * w]).astype(o_ref.dtype)


def _swa(proj, sinks):
    w, rows = WINDOW, SWA_ROWS
    nq, nkv = SWA_HEADS * SWA_HD, SWA_KV_HEADS * SWA_HD
    per = rows // w
    cur = lambda split: pl.BlockSpec((rows, nkv), lambda i, c=IN_OFFSETS[split] // nkv: (i, c))
    prev = lambda split: pl.BlockSpec(
        (w, nkv), lambda i, c=IN_OFFSETS[split] // nkv: (jnp.maximum(i * per - 1, 0), c))
    return pl.pallas_call(
        _swa_kernel,
        grid=(SEQ // rows,),
        in_specs=[pl.BlockSpec(memory_space=pltpu.SMEM),
                  pl.BlockSpec((rows, nq), lambda i: (i, IN_OFFSETS[Q_S] // nq)),
                  cur(K_S), cur(V_S), prev(K_S), prev(V_S)],
        out_specs=pl.BlockSpec((rows, nq), lambda i: (i, 0)),
        out_shape=jax.ShapeDtypeStruct((SEQ, nq), BF16),
        compiler_params=_params("parallel"),
        name="swa",
    )(sinks, proj, proj, proj, proj, proj)


def _mem_kv_kernel(mem_ref, g_ref, w_ref, o_ref):
    memn = (_rms(mem_ref[...]) * g_ref[...]).astype(BF16)
    o_ref[...] = _dot(memn, w_ref[...]).astype(o_ref.dtype)


def _mem_kv(mem2d, g, w):
    n_out = 2 * X_HEADS * X_HD
    return pl.pallas_call(
        _mem_kv_kernel,
        out_shape=jax.ShapeDtypeStruct((N_MEM, n_out), BF16),
        compiler_params=pltpu.CompilerParams(vmem_limit_bytes=VMEM_LIMIT),
        name="mem_kv",
    )(mem2d, g.reshape(1, D_MODEL), w)


def _merge_kernel(x_ref, yr_ref, ys_ref, *refs):
    n_gate = D_MODEL // GATE_BLOCK
    gr_refs, gs_refs = refs[:n_gate], refs[n_gate:2 * n_gate]
    (kv_ref, wur_ref, wus_ref, wo_ref, gx_ref, wxq_ref, wxo_ref, gf_ref, h_ref, n_ref) = refs[2 * n_gate:]
    gate_r =jnp.concatenate([r[...] for r in gr_refs], axis=-1).astype(F32)
    gate_s = jnp.concatenate([r[...] for r in gs_refs], axis=-1).astype(F32)
    up_r = _dot(yr_ref[...], wur_ref[...])
    up_s = _dot(ys_ref[...], wus_ref[...])
    merged = _sigmoid(gate_r) * up_r + _sigmoid(gate_s) * up_s
    h1 = x_ref[...] + _dot(merged.astype(BF16), wo_ref[...])

    nx = (_rms(h1) * gx_ref[...]).astype(BF16)
    qx = _dot(nx, wxq_ref[...]).astype(BF16)
    kv = kv_ref[...]
    outs = []
    for h in range(X_HEADS):
        q_h = qx[:, h * X_HD:(h + 1) * X_HD]
        k_h = kv[:, h * X_HD:(h + 1) * X_HD]
        v_h = kv[:, (X_HEADS + h) * X_HD:(X_HEADS + h + 1) * X_HD]
        s = _dot_nt(q_h, k_h) * (X_HD ** -0.5)
        e = jnp.exp(s - jnp.max(s, axis=-1, keepdims=True))
        p = e / jnp.sum(e, axis=-1, keepdims=True)
        outs.append(_dot(p.astype(BF16), v_h))
    att = jnp.concatenate(outs, axis=-1).astype(BF16)
    h2 = h1 + _dot(att, wxo_ref[...])
    h_ref[...] = h2
    n_ref[...] = (_rms(h2) * gf_ref[...]).astype(BF16)


def _merge(x2d, yr, ys, proj, kv, w_up_ret, w_up_swa, w_o, g_x, w_xq, w_xo, g_ffn):
    tm = MERGE_TM
    rows = lambda width, col=0: pl.BlockSpec((tm, width), lambda i, c=col: (i, c))
    n_gate = D_MODEL // GATE_BLOCK
    gate = lambda split: [rows(GATE_BLOCK, IN_OFFSETS[split] // GATE_BLOCK + b) for b in range(n_gate)]
    assert IN_OFFSETS[GATE_R] % GATE_BLOCK == 0 and IN_OFFSETS[GATE_S] % GATE_BLOCK == 0
    n_x = X_HEADS * X_HD
    return pl.pallas_call(
        _merge_kernel,
        grid=(SEQ // tm,),
        in_specs=[rows(D_MODEL), rows(RET_HEADS * RET_DV), rows(SWA_HEADS * SWA_HD),
                  *gate(GATE_R), *gate(GATE_S),
                  _resident((N_MEM, 2 * n_x)),
                  _resident((RET_HEADS * RET_DV, D_MODEL)), _resident((SWA_HEADS * SWA_HD, D_MODEL)),
                  _resident((D_MODEL, D_MODEL)), _resident((1, D_MODEL)),
                  _resident((D_MODEL, n_x)), _resident((n_x, D_MODEL)), _resident((1, D_MODEL))],
        out_specs=[rows(D_MODEL), rows(D_MODEL)],
        out_shape=[jax.ShapeDtypeStruct((SEQ, D_MODEL), F32),
                   jax.ShapeDtypeStruct((SEQ, D_MODEL), BF16)],
        compiler_params=_params("parallel"),
        name="merge_xattn",
    )(x2d, yr, ys, *([proj] * (2 * n_gate)), kv, w_up_ret, w_up_swa, w_o, g_x.reshape(1, D_MODEL),
      w_xq, w_xo, g_ffn.reshape(1, D_MODEL))


def _ffn_kernel(n_ref, h_hbm, wg_ref, wu_ref, wd_ref, gf_ref, o_ref, h_ref, h_sem):
    i, f = pl.program_id(0), pl.program_id(1)
    rows = pl.ds(pl.multiple_of(i * FFN_TM, FFN_TM), FFN_TM)
    residual_copy = pltpu.make_async_copy(h_hbm.at[rows, :], h_ref, h_sem)

    @pl.when(f == 0)
    def _():
        residual_copy.start()
        for c in range(FFN_TM // FFN_SLAB):
            o_ref[c * FFN_SLAB:(c + 1) * FFN_SLAB, :] = jnp.zeros((FFN_SLAB, D_MODEL), F32)

    n2 = n_ref[...]
    ts = []
    for c in range(FFN_TF // FFN_SLAB):
        cols = slice(c * FFN_SLAB, (c + 1) * FFN_SLAB)
        a = _dot(n2, wg_ref[:, cols])
        b = _dot(n2, wu_ref[:, cols])
        ts.append((a * _sigmoid(a) * b).astype(BF16))
    t = jnp.concatenate(ts, axis=-1)
    for c in range(D_MODEL // FFN_TF):
        cols = slice(c * FFN_TF, (c + 1) * FFN_TF)
        o_ref[:, cols] += _dot(t, wd_ref[:, cols])

    @pl.when(f == pl.num_programs(1) - 1)
    def _():
        residual_copy.wait()
        for c in range(FFN_TM // FFN_SLAB):
            r = slice(c * FFN_SLAB, (c + 1) * FFN_SLAB)
            o_ref[r, :] = _rms(h_ref[r, :] + o_ref[r, :]) * gf_ref[...]


def _ffn(n2, h2, w_gate, w_up, w_down, g_final):
    tm, tf = FFN_TM, FFN_TF
    return pl.pallas_call(
        _ffn_kernel,
        grid=(SEQ // tm, D_FF // tf),
        in_specs=[pl.BlockSpec((tm, D_MODEL), lambda i, f: (i, 0)),
                  pl.BlockSpec(memory_space=pl.ANY),
                  pl.BlockSpec((D_MODEL, tf), lambda i, f: (0, f)),
                  pl.BlockSpec((D_MODEL, tf), lambda i, f: (0, f)),
                  pl.BlockSpec((tf, D_MODEL), lambda i, f: (f, 0)),
                  pl.BlockSpec((1, D_MODEL), lambda i, f: (0, 0))],
        out_specs=pl.BlockSpec((tm, D_MODEL), lambda i, f: (i, 0)),
        out_shape=jax.ShapeDtypeStruct((SEQ, D_MODEL), F32),
        scratch_shapes=[pltpu.VMEM((tm, D_MODEL), F32), pltpu.SemaphoreType.DMA(())],
        compiler_params=_params("arbitrary", "arbitrary"),
        name="ffn",
    )(n2, h2, w_gate, w_up, w_down, g_final.reshape(1, D_MODEL))


def kernel(x, mem, positions, g_mix, w_in, w_up_ret, w_up_swa, sinks, w_o, g_x, g_mem,
           w_xq, w_xkv, w_xo, g_ffn, w_ffn_gate, w_ffn_up, w_ffn_down, g_final):
    assert x.shape == (1, SEQ, D_MODEL) and mem.shape == (1, N_MEM, D_MODEL)
    assert w_in.shape == (1, D_MODEL, D_IN)
    bf = lambda w: w[0].astype(BF16)
    x2d = x.reshape(SEQ, D_MODEL)
    tables = _rope_tables(positions)
    proj = _in_proj(x2d, g_mix[0], w_in[0], tables)
    yr = _retention(proj)
    ys = _swa(proj, sinks[0])
    kv = _mem_kv(mem.reshape(N_MEM, D_MODEL), g_mem[0], bf(w_xkv))
    h2, n2 = _merge(x2d, yr, ys, proj, kv, bf(w_up_ret), bf(w_up_swa), bf(w_o), g_x[0],
                    bf(w_xq), bf(w_xo), g_ffn[0])
    out = _ffn(n2, h2, bf(w_ffn_gate), bf(w_ffn_up), bf(w_ffn_down), g_final)
    return out.reshape(1, SEQ, D_MODEL)
```

```python
import jax
import jax.numpy as jnp
import numpy as np
from jax import lax
from jax.experimental import pallas as pl
from jax.experimental.pallas import tpu as pltpu

F32 = jnp.float32
BF16 = jnp.bfloat16

D_MODEL = 2048
SEQ = 16384
N_MEM = 256
EPS = 1e-6
RET_HEADS = 4
RET_DK = 256
RET_DV = 256
RET_THETA = 10000.0
SWA_HEADS = 16
SWA_KV_HEADS = 4
SWA_HD = 64
WINDOW = 128
ROPE_THETA = 500000.0
ROPE_DIM = SWA_HD // 4
X_HEADS = 4
X_HD = 128
D_FF = -(-8 * D_MODEL // (3 * 256)) * 256
IN_SPLITS = (RET_HEADS * RET_DK, RET_HEADS * RET_DK, RET_HEADS * RET_DV, RET_HEADS * RET_DV,
             SWA_HEADS * SWA_HD, SWA_KV_HEADS * SWA_HD, SWA_KV_HEADS * SWA_HD, D_MODEL, D_MODEL)
D_IN = sum(IN_SPLITS)
IN_OFFSETS = tuple(int(o) for o in np.cumsum((0,) + IN_SPLITS[:-1]))
Q_R, K_R, V_R, G_R, Q_S, K_S, V_S, GATE_R, GATE_S = range(9)
GATE_BLOCK = 512
NEG = -1e30

LANES = 128
VMEM_LIMIT = 58 * 1024 * 1024

RET_CHUNK = 256
RET_ROWS = 1024
ROPE_ROWS = 1024
PROJ_TM, PROJ_TN = 2048, 512
NORM_SLAB = 256
CAST_STEPS_PER_ROW = 16
CAST_ROW_BLOCKS, CAST_COL_BLOCKS = 32, 4
SWA_ROWS = 512
MERGE_TM = 256
FFN_TM, FFN_TF = 1024, 512
FFN_SLAB = 256


def _sigmoid(x):
    return 1.0 / (1.0 + jnp.exp(-x))


def _rms(x):
    return x * lax.rsqrt(jnp.mean(x * x, axis=-1, keepdims=True) + EPS)


def _dot(a, b):
    return jnp.dot(a, b, preferred_element_type=F32)


def _dot_nt(a, b):
    return lax.dot_general(a, b, (((1,), (1,)), ((), ())), preferred_element_type=F32)


def _dot_tn(a, b):
    return lax.dot_general(a, b, (((0,), (0,)), ((), ())), preferred_element_type=F32)


def _params(*sem):
    return pltpu.CompilerParams(dimension_semantics=sem, vmem_limit_bytes=VMEM_LIMIT)


def _resident(shape):
    nd = len(shape)
    return pl.BlockSpec(shape, lambda *_: (0,) * nd, pipeline_mode=pl.Buffered(1))


def _rope_tables_kernel(pos_ref, invr_ref, invs_ref, sign_ref, rc_ref, rs_ref, sc_ref, ss_ref):
    pos = pos_ref[...].astype(F32)
    ang_r = pos * invr_ref[...]
    rc_ref[...] = jnp.cos(ang_r)
    rs_ref[...] = jnp.sin(ang_r)
    ang_s = pos * invs_ref[...]
    sc_ref[...] = jnp.cos(ang_s)
    ss_ref[...] = jnp.sin(ang_s) * sign_ref[...]


def _rope_tables(positions):
    pos = positions.reshape(SEQ, 1)
    half_r = RET_DK // 2
    inv_r = 1.0 / (RET_THETA ** (jnp.arange(half_r, dtype=F32) / half_r))
    half_s = ROPE_DIM // 2
    inv_s = 1.0 / (ROPE_THETA ** (jnp.arange(half_s, dtype=F32) / half_s))
    d = np.arange(LANES) % SWA_HD
    inv_s_lanes = jnp.where(d < ROPE_DIM, inv_s[d % half_s], 0.0).astype(F32)
    sign = jnp.asarray(np.where(d < half_s, -1.0, np.where(d < ROPE_DIM, 1.0, 0.0)).astype(np.float32))
    row = lambda v: v.reshape(1, LANES)
    tab = jax.ShapeDtypeStruct((SEQ, LANES), F32)
    vec = pl.BlockSpec((1, LANES), lambda i: (0, 0))
    blk = pl.BlockSpec((ROPE_ROWS, LANES), lambda i: (i, 0))
    return pl.pallas_call(
        _rope_tables_kernel,
        grid=(SEQ // ROPE_ROWS,),
        in_specs=[pl.BlockSpec((ROPE_ROWS, 1), lambda i: (i, 0)), vec, vec, vec],
        out_specs=[blk] * 4,
        out_shape=[tab] * 4,
        compiler_params=_params("parallel"),
        name="rope_tables",
    )(pos, row(inv_r), row(inv_s_lanes), row(sign))


def _in_proj_kernel(x_hbm, g_ref, w_ref, rc_ref, rs_ref, sc_ref, ss_ref, *refs):
    tm, tn = PROJ_TM, PROJ_TN
    i, j = pl.program_id(0), pl.program_id(1)
    blk = lambda split: IN_OFFSETS[split] // tn
    n_cast = (len(refs) - 5) // 2
    cast_in, o_ref, cast_out = refs[:n_cast], refs[n_cast], refs[n_cast + 1:2 * n_cast + 1]
    n_ref, acc_ref, x_buf, x_sem = refs[2 * n_cast + 1:]
    for src, dst in zip(cast_in, cast_out):
        dst[...] = src[...].astype(BF16)

    def x_copy(block):
        rows = pl.ds(pl.multiple_of(block * tm, tm), tm)
        return pltpu.make_async_copy(x_hbm.at[rows, :], x_buf, x_sem)

    @pl.when(j == 0)
    def _():
        @pl.when(i == 0)
        def _():
            x_copy(0).start()

        x_copy(i).wait()
        for c in range(tm // NORM_SLAB):
            r = slice(c * NORM_SLAB, (c + 1) * NORM_SLAB)
            n_ref[r, :] = (_rms(x_buf[r, :]) * g_ref[...]).astype(BF16)

    @pl.when((j == 1) & (i + 1 < pl.num_programs(0)))
    def _():
        x_copy(i + 1).start()

    acc = _dot(n_ref[...], w_ref[...].astype(BF16))
    o_ref[...] = acc.astype(BF16)
    acc_ref[...] = acc
    is_ret = (j >= blk(Q_R)) & (j < blk(V_R))
    is_qs = (j >= blk(Q_S)) & (j < blk(K_S))
    is_kvs = j == blk(K_S)

    def swa_rope(cols):
        half = ROPE_DIM // 2
        xt = acc_ref[:, cols]
        assert SWA_HD & (SWA_HD - 1) == 0
        takes_upper = (lax.broadcasted_iota(jnp.int32, xt.shape, 1) & (SWA_HD - 1)) < half
        partner = jnp.where(takes_upper, pltpu.roll(xt, LANES - half, 1), pltpu.roll(xt, half, 1))
        return xt * sc_ref[...] + partner * ss_ref[...]

    @pl.when(is_ret)
    def _():
        scale = jnp.where(j >= blk(K_R), RET_DK ** -0.5, 1.0).astype(F32)
        cos, sin = rc_ref[...] * scale, rs_ref[...] * scale
        half = RET_DK // 2
        for h in range(tn // RET_DK):
            a1 = acc_ref[:, h * RET_DK:h * RET_DK + half]
            a2 = acc_ref[:, h * RET_DK + half:(h + 1) * RET_DK]
            o_ref[:, h * RET_DK:h * RET_DK + half] = (a1 * cos - a2 * sin).astype(BF16)
            o_ref[:, h * RET_DK + half:(h + 1) * RET_DK] = (a2 * cos + a1 * sin).astype(BF16)

    @pl.when(is_qs)
    def _():
        for t in range(tn // LANES):
            cols = slice(t * LANES, (t + 1) * LANES)
            o_ref[:, cols] = (swa_rope(cols) * (SWA_HD ** -0.5)).astype(BF16)

    @pl.when(is_kvs)
    def _():
        for t in range(IN_SPLITS[K_S] // LANES):
            cols = slice(t * LANES, (t + 1) * LANES)
            o_ref[:, cols] = swa_rope(cols).astype(BF16)


def _in_proj(x2d, g, w, tables, cast_weights):
    tm, tn = PROJ_TM, PROJ_TN
    assert all(IN_OFFSETS[s] % tn == 0 for s in (Q_R, K_R, V_R, Q_S, K_S))
    assert IN_OFFSETS[V_S] == IN_OFFSETS[K_S] + IN_SPLITS[K_S] and IN_SPLITS[K_S] + IN_SPLITS[V_S] == tn
    tab = pl.BlockSpec((tm, LANES), lambda i, j: (i, 0))
    n_i, n_j = SEQ // tm, D_IN // tn
    assert n_i * CAST_STEPS_PER_ROW == CAST_ROW_BLOCKS * CAST_COL_BLOCKS and CAST_STEPS_PER_ROW <= n_j

    def cast_spec(shape):
        rows, cols = shape[0] // CAST_ROW_BLOCKS, shape[1] // CAST_COL_BLOCKS
        assert rows % 16 == 0 and cols % LANES == 0 and rows * CAST_ROW_BLOCKS == shape[0]

        def index(i, j):
            b = i * CAST_STEPS_PER_ROW + jnp.minimum(j, CAST_STEPS_PER_ROW - 1)
            return b // CAST_COL_BLOCKS, b % CAST_COL_BLOCKS
        return pl.BlockSpec((rows, cols), index)

    cast_specs = [cast_spec(cw.shape) for cw in cast_weights]
    return pl.pallas_call(
        _in_proj_kernel,
        grid=(SEQ // tm, D_IN // tn),
        in_specs=[pl.BlockSpec(memory_space=pl.ANY),
                  pl.BlockSpec((1, D_MODEL), lambda i, j: (0, 0)),
                  pl.BlockSpec((D_MODEL, tn), lambda i, j: (0, j)),
                  tab, tab, tab, tab, *cast_specs],
        out_specs=[pl.BlockSpec((tm, tn), lambda i, j: (i, j)), *cast_specs],
        out_shape=[jax.ShapeDtypeStruct((SEQ, D_IN), BF16)]
                  + [jax.ShapeDtypeStruct(cw.shape, BF16) for cw in cast_weights],
        scratch_shapes=[pltpu.VMEM((tm, D_MODEL), BF16), pltpu.VMEM((tm, tn), F32),
                        pltpu.VMEM((tm, D_MODEL), F32), pltpu.SemaphoreType.DMA(())],
        compiler_params=_params("arbitrary", "arbitrary"),
        name="in_proj",
    )(x2d, g.reshape(1, D_MODEL), w, *tables, *cast_weights)


def _retention_kernel(logg_ref, q_ref, k_ref, v_ref, g_ref, o_ref, state_ref, dmask_ref):
    c_rows = RET_CHUNK
    idx = lax.broadcasted_iota(jnp.int32, (c_rows, 1), 0).astype(F32)

    @pl.when(pl.program_id(0) == 0)
    def _():
        state_ref[...] = jnp.zeros_like(state_ref)
        i = lax.broadcasted_iota(jnp.int32, (c_rows, c_rows), 0)
        j = lax.broadcasted_iota(jnp.int32, (c_rows, c_rows), 1)
        rel = (i - j).astype(F32)
        for h in range(RET_HEADS):
            dmask_ref[h] = jnp.where(rel >= 0, jnp.exp(logg_ref[h] * jnp.maximum(rel, 0.0)), 0.0)

    for h in range(RET_HEADS):
        lg = logg_ref[h]
        cols = slice(h * RET_DK, (h + 1) * RET_DK)
        q_dec = jnp.exp(lg * (idx + 1.0))
        k_dec = jnp.exp(lg * (c_rows - 1.0 - idx))
        chunk_decay = jnp.exp(jnp.full((1, RET_DV), lg * c_rows, F32))
        for c in range(RET_ROWS // c_rows):
            r = slice(c * c_rows, (c + 1) * c_rows)
            q, k, v = q_ref[r, cols], k_ref[r, cols], v_ref[r, cols]
            scores = _dot_nt(q, k) * dmask_ref[h]
            inner = _dot(scores.astype(BF16), v)
            state = state_ref[h]
            cross = _dot((q.astype(F32) * q_dec).astype(BF16), state.astype(BF16))
            state_ref[h] = state * chunk_decay + _dot_tn((k.astype(F32) * k_dec).astype(BF16), v)
            y = _rms(inner + cross)
            g = g_ref[r, cols].astype(F32)
            o_ref[r, cols] = (y * (g * _sigmoid(g))).astype(o_ref.dtype)


def _retention(proj):
    log_g = jnp.log(1.0 - jnp.power(2.0, -5.0 - jnp.arange(RET_HEADS, dtype=F32)))
    rows, width = RET_ROWS, RET_HEADS * RET_DK
    assert RET_DK == RET_DV
    all_heads = lambda split: pl.BlockSpec((rows, width), lambda n, c=IN_OFFSETS[split] // width: (n, c))
    assert all(IN_OFFSETS[s] % width == 0 for s in (Q_R, K_R, V_R, G_R))
    return pl.pallas_call(
        _retention_kernel,
        grid=(SEQ // rows,),
        in_specs=[pl.BlockSpec(memory_space=pltpu.SMEM),
                  all_heads(Q_R), all_heads(K_R), all_heads(V_R), all_heads(G_R)],
        out_specs=pl.BlockSpec((rows, width), lambda n: (n, 0)),
        out_shape=jax.ShapeDtypeStruct((SEQ, width), BF16),
        scratch_shapes=[pltpu.VMEM((RET_HEADS, RET_DK, RET_DV), F32),
                        pltpu.VMEM((RET_HEADS, RET_CHUNK, RET_CHUNK), F32)],
        compiler_params=_params("arbitrary"),
        name="retention",
    )(log_g, proj, proj, proj, proj)


def _swa_kernel(sinks_ref, q_ref, k_ref, v_ref, kp_ref, vp_ref, o_ref):
    w = WINDOW
    hd = SWA_HD
    group = SWA_HEADS // SWA_KV_HEADS
    step = pl.program_id(0)
    low = lax.broadcasted_iota(jnp.int32, (w, LANES), 1) < hd

    def dup_halves(x):
        xf = x.astype(F32)
        sw = pltpu.roll(xf, hd, 1)
        lo = lax.broadcasted_iota(jnp.int32, xf.shape, 1) < hd
        return jnp.where(lo, xf, sw).astype(BF16), jnp.where(lo, sw, xf).astype(BF16)

    def per_kv_head(cur_ref, prev_ref):
        allrows = jnp.concatenate([prev_ref[...], cur_ref[...]], axis=0)
        heads = []
        for c in range(allrows.shape[1] // LANES):
            heads.extend(dup_halves(allrows[:, c * LANES:(c + 1) * LANES]))
        return heads

    k2 = per_kv_head(k_ref, kp_ref)
    v2 = per_kv_head(v_ref, vp_ref)

    qi = lax.broadcasted_iota(jnp.int32, (w, 2 * w), 0)
    kj = lax.broadcasted_iota(jnp.int32, (w, 2 * w), 1)
    dist = qi + w - kj
    band = (dist >= 0) & (dist < WINDOW)
    zero = jnp.zeros((w, LANES), BF16)
    ones = jnp.ones((2 * w, LANES), BF16)
    zero_kv = jnp.zeros((2 * w, LANES), BF16)

    tile = lambda a: jnp.concatenate([a] * group, axis=0)
    bias_cur = tile(jnp.where(band[:, w:], 0.0, NEG))
    bias_prev = tile(jnp.where(band[:, :w], 0.0, NEG))
    sink_slot = lax.broadcasted_iota(jnp.int32, (group * w, LANES), 1) == 0
    first_row = lax.broadcasted_iota(jnp.int32, (2 * w, LANES), 0) == 0

    for kh in range(SWA_KV_HEADS):
        cols = [slice((kh * group // 2 + p) * LANES, (kh * group // 2 + p + 1) * LANES)
                for p in range(group // 2)]
        sink = jnp.concatenate([jnp.full((w, LANES), sinks_ref[kh * group + g], F32)
                                for g in range(group)], axis=0)
        bias_later = jnp.concatenate([jnp.where(sink_slot, sink, bias_prev), bias_cur], axis=-1)
        bias_first = jnp.concatenate(
            [jnp.where(sink_slot, sink, jnp.where(step > 0, bias_prev, NEG)), bias_cur], axis=-1)
        for t in range(SWA_ROWS // w):
            rows = slice(t * w, (t + 1) * w)
            bias = bias_first if t == 0 else bias_later
            k_w = jnp.where(first_row, zero_kv, k2[kh][t * w:(t + 2) * w])
            v_w = jnp.where(first_row, zero_kv, v2[kh][t * w:(t + 2) * w])
            parts = []
            for c in cols:
                qc = q_ref[rows, c]
                parts += [jnp.where(low, qc, zero), jnp.where(low, zero, qc)]
            s = _dot_nt(jnp.concatenate(parts, axis=0), k_w) + bias
            e = jnp.exp(s - jnp.max(s, axis=-1, keepdims=True))
            pv = _dot(e.astype(BF16), jnp.concatenate([v_w, ones], axis=-1))
            o = pv[:, :LANES] / pv[:, LANES:]
            for p, c in enumerate(cols):
                o_ref[rows, c] = jnp.where(low, o[2 * p * w:(2 * p + 1) * w],
                                           o[(2 * p + 1) * w:(2 * p + 2) * w]).astype(o_ref.dtype)


def _swa(proj, sinks):
    w, rows = WINDOW, SWA_ROWS
    nq, nkv = SWA_HEADS * SWA_HD, SWA_KV_HEADS * SWA_HD
    per = rows // w
    cur = lambda split: pl.BlockSpec((rows, nkv), lambda i, c=IN_OFFSETS[split] // nkv: (i, c))
    prev = lambda split: pl.BlockSpec(
        (w, nkv), lambda i, c=IN_OFFSETS[split] // nkv: (jnp.maximum(i * per - 1, 0), c))
    return pl.pallas_call(
        _swa_kernel,
        grid=(SEQ // rows,),
        in_specs=[pl.BlockSpec(memory_space=pltpu.SMEM),
                  pl.BlockSpec((rows, nq), lambda i: (i, IN_OFFSETS[Q_S] // nq)),
                  cur(K_S), cur(V_S), prev(K_S), prev(V_S)],
        out_specs=pl.BlockSpec((rows, nq), lambda i: (i, 0)),
        out_shape=jax.ShapeDtypeStruct((SEQ, nq), BF16),
        compiler_params=_params("parallel"),
        name="swa",
    )(sinks, proj, proj, proj, proj, proj)


def _mem_kv_kernel(mem_ref, g_ref, w_ref, o_ref):
    memn = (_rms(mem_ref[...]) * g_ref[...]).astype(BF16)
    o_ref[...] = _dot(memn, w_ref[...]).astype(o_ref.dtype)


def _mem_kv(mem2d, g, w):
    n_out = 2 * X_HEADS * X_HD
    return pl.pallas_call(
        _mem_kv_kernel,
        out_shape=jax.ShapeDtypeStruct((N_MEM, n_out), BF16),
        compiler_params=pltpu.CompilerParams(vmem_limit_bytes=VMEM_LIMIT),
        name="mem_kv",
    )(mem2d, g.reshape(1, D_MODEL), w)


def _merge_kernel(x_ref, yr_ref, ys_ref, *refs):
    n_gate = D_MODEL // GATE_BLOCK
    gr_refs, gs_refs = refs[:n_gate], refs[n_gate:2 * n_gate]
    (kv_ref, wur_ref, wus_ref, wo_ref, gx_ref, wxq_ref, wxo_ref, gf_ref, h_ref, n_ref) = refs[2 * n_gate:]
    gate_r = jnp.concatenate([r[...] for r in gr_refs], axis=-1).astype(F32)
    gate_s = jnp.concatenate([r[...] for r in gs_refs], axis=-1).astype(F32)
    up_r = _dot(yr_ref[...], wur_ref[...])
    up_s = _dot(ys_ref[...], wus_ref[...])
    merged = _sigmoid(gate_r) * up_r + _sigmoid(gate_s) * up_s
    h1 = x_ref[...] + _dot(merged.astype(BF16), wo_ref[...])

    nx = (_rms(h1) * gx_ref[...]).astype(BF16)
    qx = _dot(nx, wxq_ref[...]).astype(BF16)
    kv = kv_ref[...]
    outs = []
    for h in range(X_HEADS):
        q_h = qx[:, h * X_HD:(h + 1) * X_HD]
        k_h = kv[:, h * X_HD:(h + 1) * X_HD]
        v_h = kv[:, (X_HEADS + h) * X_HD:(X_HEADS + h + 1) * X_HD]
        s = _dot_nt(q_h, k_h) * (X_HD ** -0.5)
        e = jnp.exp(s - jnp.max(s, axis=-1, keepdims=True))
        p = e / jnp.sum(e, axis=-1, keepdims=True)
        outs.append(_dot(p.astype(BF16), v_h))
    att = jnp.concatenate(outs, axis=-1).astype(BF16)
    h2 = h1 + _dot(att, wxo_ref[...])
    h_ref[...] = h2
    n_ref[...] = (_rms(h2) * gf_ref[...]).astype(BF16)


def _merge(x2d, yr, ys, proj, kv, w_up_ret, w_up_swa, w_o, g_x, w_xq, w_xo, g_ffn):
    tm = MERGE_TM
    rows = lambda width, col=0: pl.BlockSpec((tm, width), lambda i, c=col: (i, c))
    n_gate = D_MODEL // GATE_BLOCK
    gate = lambda split: [rows(GATE_BLOCK, IN_OFFSETS[split] // GATE_BLOCK + b) for b in range(n_gate)]
    assert IN_OFFSETS[GATE_R] % GATE_BLOCK == 0 and IN_OFFSETS[GATE_S] % GATE_BLOCK == 0
    n_x = X_HEADS * X_HD
    return pl.pallas_call(
        _merge_kernel,
        grid=(SEQ // tm,),
        in_specs=[rows(D_MODEL), rows(RET_HEADS * RET_DV), rows(SWA_HEADS * SWA_HD),
                  *gate(GATE_R), *gate(GATE_S),
                  _resident((N_MEM, 2 * n_x)),
                  _resident((RET_HEADS * RET_DV, D_MODEL)), _resident((SWA_HEADS * SWA_HD, D_MODEL)),
                  _resident((D_MODEL, D_MODEL)), _resident((1, D_MODEL)),
                  _resident((D_MODEL, n_x)), _resident((n_x, D_MODEL)), _resident((1, D_MODEL))],
        out_specs=[rows(D_MODEL), rows(D_MODEL)],
        out_shape=[jax.ShapeDtypeStruct((SEQ, D_MODEL), F32),
                   jax.ShapeDtypeStruct((SEQ, D_MODEL), BF16)],
        compiler_params=_params("parallel"),
        name="merge_xattn",
    )(x2d, yr, ys, *([proj] * (2 * n_gate)), kv, w_up_ret, w_up_swa, w_o, g_x.reshape(1, D_MODEL),
      w_xq, w_xo, g_ffn.reshape(1, D_MODEL))


def _ffn_kernel(n_ref, h_hbm, wg_ref, wu_ref, wd_ref, gf_ref, o_ref, h_ref, h_sem):
    i, f = pl.program_id(0), pl.program_id(1)
    rows = pl.ds(pl.multiple_of(i * FFN_TM, FFN_TM), FFN_TM)
    residual_copy = pltpu.make_async_copy(h_hbm.at[rows, :], h_ref, h_sem)

    @pl.when(f == 0)
    def _():
        residual_copy.start()
        for c in range(FFN_TM // FFN_SLAB):
            o_ref[c * FFN_SLAB:(c + 1) * FFN_SLAB, :] = jnp.zeros((FFN_SLAB, D_MODEL), F32)

    n2 = n_ref[...]
    ts = []
    for c in range(FFN_TF // FFN_SLAB):
        cols = slice(c * FFN_SLAB, (c + 1) * FFN_SLAB)
        a = _dot(n2, wg_ref[:, cols])
        b = _dot(n2, wu_ref[:, cols])
        ts.append((a * _sigmoid(a) * b).astype(BF16))
    t = jnp.concatenate(ts, axis=-1)
    for c in range(D_MODEL // FFN_TF):
        cols = slice(c * FFN_TF, (c + 1) * FFN_TF)
        o_ref[:, cols] += _dot(t, wd_ref[:, cols])

    @pl.when(f == pl.num_programs(1) - 1)
    def _():
        residual_copy.wait()
        for c in range(FFN_TM // FFN_SLAB):
            r = slice(c * FFN_SLAB, (c + 1) * FFN_SLAB)
            o_ref[r, :] = _rms(h_ref[r, :] + o_ref[r, :]) * gf_ref[...]


def _ffn(n2, h2, w_gate, w_up, w_down, g_final):
    tm, tf = FFN_TM, FFN_TF
    return pl.pallas_call(
        _ffn_kernel,
        grid=(SEQ // tm, D_FF // tf),
        in_specs=[pl.BlockSpec((tm, D_MODEL), lambda i, f: (i, 0)),
                  pl.BlockSpec(memory_space=pl.ANY),
                  pl.BlockSpec((D_MODEL, tf), lambda i, f: (0, f)),
                  pl.BlockSpec((D_MODEL, tf), lambda i, f: (0, f)),
                  pl.BlockSpec((tf, D_MODEL), lambda i, f: (f, 0)),
                  pl.BlockSpec((1, D_MODEL), lambda i, f: (0, 0))],
        out_specs=pl.BlockSpec((tm, D_MODEL), lambda i, f: (i, 0)),
        out_shape=jax.ShapeDtypeStruct((SEQ, D_MODEL), F32),
        scratch_shapes=[pltpu.VMEM((tm, D_MODEL), F32), pltpu.SemaphoreType.DMA(())],
        compiler_params=_params("arbitrary", "arbitrary"),
        name="ffn",
    )(n2, h2, w_gate, w_up, w_down, g_final.reshape(1, D_MODEL))


def kernel(x, mem, positions, g_mix, w_in, w_up_ret, w_up_swa, sinks, w_o, g_x, g_mem,
           w_xq, w_xkv, w_xo, g_ffn, w_ffn_gate, w_ffn_up, w_ffn_down, g_final):
    assert x.shape == (1, SEQ, D_MODEL) and mem.shape == (1, N_MEM, D_MODEL)
    assert w_in.shape == (1, D_MODEL, D_IN)
    bf = lambda w: w[0].astype(BF16)
    x2d = x.reshape(SEQ, D_MODEL)
    tables = _rope_tables(positions)
    proj, w_gate, w_up, w_down = _in_proj(x2d, g_mix[0], w_in[0], tables,
                                          (w_ffn_gate[0], w_ffn_up[0], w_ffn_down[0]))
    yr = _retention(proj)
    ys = _swa(proj, sinks[0])
    kv = _mem_kv(mem.reshape(N_MEM, D_MODEL), g_mem[0], bf(w_xkv))
    h2, n2 = _merge(x2d, yr, ys, proj, kv, bf(w_up_ret), bf(w_up_swa), bf(w_o), g_x[0],
                    bf(w_xq), bf(w_xo), g_ffn[0])
    out = _ffn(n2, h2, w_gate, w_up, w_down, g_final)
    return out.reshape(1, SEQ, D_MODEL)
```

```python
import jax
import jax.numpy as jnp
import numpy as np
from jax import lax
from jax.experimental import pallas as pl
from jax.experimental.pallas import tpu as pltpu

F32 = jnp.float32
BF16 = jnp.bfloat16

D_MODEL = 2048
SEQ = 16384
N_MEM = 256
EPS = 1e-6
RET_HEADS = 4
RET_DK = 256
RET_DV = 256
RET_THETA = 10000.0
SWA_HEADS = 16
SWA_KV_HEADS = 4
SWA_HD = 64
WINDOW = 128
ROPE_THETA = 500000.0
ROPE_DIM = SWA_HD // 4
X_HEADS = 4
X_HD = 128
D_FF = -(-8 * D_MODEL // (3 * 256)) * 256
IN_SPLITS = (RET_HEADS * RET_DK, RET_HEADS * RET_DK, RET_HEADS * RET_DV, RET_HEADS * RET_DV,
             SWA_HEADS * SWA_HD, SWA_KV_HEADS * SWA_HD, SWA_KV_HEADS * SWA_HD, D_MODEL, D_MODEL)
D_IN = sum(IN_SPLITS)
IN_OFFSETS = tuple(int(o) for o in np.cumsum((0,) + IN_SPLITS[:-1]))
Q_R, K_R, V_R, G_R, Q_S, K_S, V_S, GATE_R, GATE_S = range(9)
GATE_BLOCK = 512
NEG = -1e30

LANES = 128
VMEM_LIMIT = 58 * 1024 * 1024

RET_CHUNK = 256
RET_ROWS = 1024
ROPE_ROWS = 1024
PROJ_TM, PROJ_TN = 2048, 512
NORM_SLAB = 256
CAST_STEPS_PER_ROW = 16
CAST_ROW_BLOCKS, CAST_COL_BLOCKS = 32, 4
SWA_ROWS = 512
MERGE_TM = 256
FFN_TM, FFN_TF = 1024, 512
FFN_SLAB = 256


def _sigmoid(x):
    return 1.0 / (1.0 + jnp.exp(-x))


def _rms(x):
    return x * lax.rsqrt(jnp.mean(x * x, axis=-1, keepdims=True) + EPS)


def _dot(a, b):
    return jnp.dot(a, b, preferred_element_type=F32)


def _dot_nt(a, b):
    return lax.dot_general(a, b, (((1,), (1,)), ((), ())), preferred_element_type=F32)


def _dot_tn(a, b):
    return lax.dot_general(a, b, (((0,), (0,)), ((), ())), preferred_element_type=F32)


def _params(*sem):
    return pltpu.CompilerParams(dimension_semantics=sem, vmem_limit_bytes=VMEM_LIMIT)


def _resident(shape):
    nd = len(shape)
    return pl.BlockSpec(shape, lambda *_: (0,) * nd, pipeline_mode=pl.Buffered(1))


def _rope_tables_kernel(pos_ref, invr_ref, invs_ref, sign_ref, rc_ref, rs_ref, sc_ref, ss_ref):
    pos = pos_ref[...].astype(F32)
    ang_r = pos * invr_ref[...]
    rc_ref[...] = jnp.cos(ang_r)
    rs_ref[...] = jnp.sin(ang_r)
    ang_s = pos * invs_ref[...]
    sc_ref[...] = jnp.cos(ang_s)
    ss_ref[...] = jnp.sin(ang_s) * sign_ref[...]


def _rope_tables(positions):
    pos = positions.reshape(SEQ, 1)
    half_r = RET_DK // 2
    inv_r = 1.0 / (RET_THETA ** (jnp.arange(half_r, dtype=F32) / half_r))
    half_s = ROPE_DIM // 2
    inv_s = 1.0 / (ROPE_THETA ** (jnp.arange(half_s, dtype=F32) / half_s))
    d = np.arange(LANES) % SWA_HD
    inv_s_lanes = jnp.where(d < ROPE_DIM, inv_s[d % half_s], 0.0).astype(F32)
    sign = jnp.asarray(np.where(d < half_s, -1.0, np.where(d < ROPE_DIM, 1.0, 0.0)).astype(np.float32))
    row = lambda v: v.reshape(1, LANES)
    tab = jax.ShapeDtypeStruct((SEQ, LANES), F32)
    vec = pl.BlockSpec((1, LANES), lambda i: (0, 0))
    blk = pl.BlockSpec((ROPE_ROWS, LANES), lambda i: (i, 0))
    return pl.pallas_call(
        _rope_tables_kernel,
        grid=(SEQ // ROPE_ROWS,),
        in_specs=[pl.BlockSpec((ROPE_ROWS, 1), lambda i: (i, 0)), vec, vec, vec],
        out_specs=[blk] * 4,
        out_shape=[tab] * 4,
        compiler_params=_params("parallel"),
        name="rope_tables",
    )(pos, row(inv_r), row(inv_s_lanes), row(sign))


def _in_proj_kernel(x_hbm, g_ref, w_ref, rc_ref, rs_ref, sc_ref, ss_ref, *refs):
    tm, tn = PROJ_TM, PROJ_TN
    i, j = pl.program_id(0), pl.program_id(1)
    blk = lambda split: IN_OFFSETS[split] // tn
    n_cast = (len(refs) - 5) // 2
    cast_in, o_ref, cast_out = refs[:n_cast], refs[n_cast], refs[n_cast + 1:2 * n_cast + 1]
    n_ref, acc_ref, x_buf, x_sem = refs[2 * n_cast + 1:]
    for src, dst in zip(cast_in, cast_out):
        dst[...] = src[...].astype(BF16)

    def x_copy(block):
        rows = pl.ds(pl.multiple_of(block * tm, tm), tm)
        return pltpu.make_async_copy(x_hbm.at[rows, :], x_buf, x_sem)

    @pl.when(j == 0)
    def _():
        @pl.when(i == 0)
        def _():
            x_copy(0).start()

        x_copy(i).wait()
        for c in range(tm // NORM_SLAB):
            r = slice(c * NORM_SLAB, (c + 1) * NORM_SLAB)
            n_ref[r, :] = (_rms(x_buf[r, :]) * g_ref[...]).astype(BF16)

    @pl.when((j == 1) & (i + 1 < pl.num_programs(0)))
    def _():
        x_copy(i + 1).start()

    acc_ref[...] = _dot(n_ref[...], w_ref[...].astype(BF16))
    o_ref[...] = acc_ref[...].astype(BF16)
    is_ret = (j >= blk(Q_R)) & (j < blk(V_R))
    is_qs = (j >= blk(Q_S)) & (j < blk(K_S))
    is_kvs = j == blk(K_S)

    def swa_rope(cols):
        half = ROPE_DIM // 2
        xt = acc_ref[:, cols]
        assert SWA_HD & (SWA_HD - 1) == 0
        takes_upper = (lax.broadcasted_iota(jnp.int32, xt.shape, 1) & (SWA_HD - 1)) < half
        partner = jnp.where(takes_upper, pltpu.roll(xt, LANES - half, 1), pltpu.roll(xt, half, 1))
        return xt * sc_ref[...] + partner * ss_ref[...]

    @pl.when(is_ret)
    def _():
        scale = jnp.where(j >= blk(K_R), RET_DK ** -0.5, 1.0).astype(F32)
        cos, sin = rc_ref[...] * scale, rs_ref[...] * scale
        half = RET_DK // 2
        for h in range(tn // RET_DK):
            a1 = acc_ref[:, h * RET_DK:h * RET_DK + half]
            a2 = acc_ref[:, h * RET_DK + half:(h + 1) * RET_DK]
            o_ref[:, h * RET_DK:h * RET_DK + half] = (a1 * cos - a2 * sin).astype(BF16)
            o_ref[:, h * RET_DK + half:(h + 1) * RET_DK] = (a2 * cos + a1 * sin).astype(BF16)

    @pl.when(is_qs)
    def _():
        for t in range(tn // LANES):
            cols = slice(t * LANES, (t + 1) * LANES)
            o_ref[:, cols] = (swa_rope(cols) * (SWA_HD ** -0.5)).astype(BF16)

    @pl.when(is_kvs)
    def _():
        for t in range(IN_SPLITS[K_S] // LANES):
            cols = slice(t * LANES, (t + 1) * LANES)
            o_ref[:, cols] = swa_rope(cols).astype(BF16)


def _in_proj(x2d, g, w, tables, cast_weights):
    tm, tn = PROJ_TM, PROJ_TN
    assert all(IN_OFFSETS[s] % tn == 0 for s in (Q_R, K_R, V_R, Q_S, K_S))
    assert IN_OFFSETS[V_S] == IN_OFFSETS[K_S] + IN_SPLITS[K_S] and IN_SPLITS[K_S] + IN_SPLITS[V_S] == tn
    tab = pl.BlockSpec((tm, LANES), lambda i, j: (i, 0))
    n_i, n_j = SEQ // tm, D_IN // tn
    assert n_i * CAST_STEPS_PER_ROW == CAST_ROW_BLOCKS * CAST_COL_BLOCKS and CAST_STEPS_PER_ROW <= n_j

    def cast_spec(shape):
        rows, cols = shape[0] // CAST_ROW_BLOCKS, shape[1] // CAST_COL_BLOCKS
        assert rows % 16 == 0 and cols % LANES == 0 and rows * CAST_ROW_BLOCKS == shape[0]

        def index(i, j):
            b = i * CAST_STEPS_PER_ROW + jnp.minimum(j, CAST_STEPS_PER_ROW - 1)
            return b // CAST_COL_BLOCKS, b % CAST_COL_BLOCKS
        return pl.BlockSpec((rows, cols), index)

    cast_specs = [cast_spec(cw.shape) for cw in cast_weights]
    return pl.pallas_call(
        _in_proj_kernel,
        grid=(SEQ // tm, D_IN // tn),
        in_specs=[pl.BlockSpec(memory_space=pl.ANY),
                  pl.BlockSpec((1, D_MODEL), lambda i, j: (0, 0)),
                  pl.BlockSpec((D_MODEL, tn), lambda i, j: (0, j)),
                  tab, tab, tab, tab, *cast_specs],
        out_specs=[pl.BlockSpec((tm, tn), lambda i, j: (i, j)), *cast_specs],
        out_shape=[jax.ShapeDtypeStruct((SEQ, D_IN), BF16)]
                  + [jax.ShapeDtypeStruct(cw.shape, BF16) for cw in cast_weights],
        scratch_shapes=[pltpu.VMEM((tm, D_MODEL), BF16), pltpu.VMEM((tm, tn), F32),
                        pltpu.VMEM((tm, D_MODEL), F32), pltpu.SemaphoreType.DMA(())],
        compiler_params=_params("arbitrary", "arbitrary"),
        name="in_proj",
    )(x2d, g.reshape(1, D_MODEL), w, *tables, *cast_weights)


def _retention_kernel(logg_ref, q_ref, k_ref, v_ref, g_ref, o_ref, state_ref, dmask_ref):
    c_rows = RET_CHUNK
    idx = lax.broadcasted_iota(jnp.int32, (c_rows, 1), 0).astype(F32)

    @pl.when(pl.program_id(0) == 0)
    def _():
        state_ref[...] = jnp.zeros_like(state_ref)
        i = lax.broadcasted_iota(jnp.int32, (c_rows, c_rows), 0)
        j = lax.broadcasted_iota(jnp.int32, (c_rows, c_rows), 1)
        rel = (i - j).astype(F32)
        for h in range(RET_HEADS):
            dmask_ref[h] = jnp.where(rel >= 0, jnp.exp(logg_ref[h] * jnp.maximum(rel, 0.0)), 0.0)

    for h in range(RET_HEADS):
        lg = logg_ref[h]
        cols = slice(h * RET_DK, (h + 1) * RET_DK)
        q_dec = jnp.exp(lg * (idx + 1.0))
        k_dec = jnp.exp(lg * (c_rows - 1.0 - idx))
        chunk_decay = jnp.exp(jnp.full((1, RET_DV), lg * c_rows, F32))
        for c in range(RET_ROWS // c_rows):
            r = slice(c * c_rows, (c + 1) * c_rows)
            q, k, v = q_ref[r, cols], k_ref[r, cols], v_ref[r, cols]
            scores = _dot_nt(q, k) * dmask_ref[h]
            inner = _dot(scores.astype(BF16), v)
            state = state_ref[h]
            cross = _dot((q.astype(F32) * q_dec).astype(BF16), state.astype(BF16))
            state_ref[h] = state * chunk_decay + _dot_tn((k.astype(F32) * k_dec).astype(BF16), v)
            y = _rms(inner + cross)
            g = g_ref[r, cols].astype(F32)
            o_ref[r, cols] = (y * (g * _sigmoid(g))).astype(o_ref.dtype)


def _retention(proj):
    log_g = jnp.log(1.0 - jnp.power(2.0, -5.0 - jnp.arange(RET_HEADS, dtype=F32)))
    rows, width = RET_ROWS, RET_HEADS * RET_DK
    assert RET_DK == RET_DV
    all_heads = lambda split: pl.BlockSpec((rows, width), lambda n, c=IN_OFFSETS[split] // width: (n, c))
    assert all(IN_OFFSETS[s] % width == 0 for s in (Q_R, K_R, V_R, G_R))
    return pl.pallas_call(
        _retention_kernel,
        grid=(SEQ // rows,),
        in_specs=[pl.BlockSpec(memory_space=pltpu.SMEM),
                  all_heads(Q_R), all_heads(K_R), all_heads(V_R), all_heads(G_R)],
        out_specs=pl.BlockSpec((rows, width), lambda n: (n, 0)),
        out_shape=jax.ShapeDtypeStruct((SEQ, width), BF16),
        scratch_shapes=[pltpu.VMEM((RET_HEADS, RET_DK, RET_DV), F32),
                        pltpu.VMEM((RET_HEADS, RET_CHUNK, RET_CHUNK), F32)],
        compiler_params=_params("arbitrary"),
        name="retention",
    )(log_g, proj, proj, proj, proj)


def _swa_kernel(sinks_ref, q_ref, k_ref, v_ref, kp_ref, vp_ref, o_ref):
    w = WINDOW
    hd = SWA_HD
    group = SWA_HEADS // SWA_KV_HEADS
    step = pl.program_id(0)
    low = lax.broadcasted_iota(jnp.int32, (w, LANES), 1) < hd

    def dup_halves(x):
        xf = x.astype(F32)
        sw = pltpu.roll(xf, hd, 1)
        lo = lax.broadcasted_iota(jnp.int32, xf.shape, 1) < hd
        return jnp.where(lo, xf, sw).astype(BF16), jnp.where(lo, sw, xf).astype(BF16)

    def per_kv_head(cur_ref, prev_ref):
        allrows = jnp.concatenate([prev_ref[...], cur_ref[...]], axis=0)
        heads = []
        for c in range(allrows.shape[1] // LANES):
            heads.extend(dup_halves(allrows[:, c * LANES:(c + 1) * LANES]))
        return heads

    k2 = per_kv_head(k_ref, kp_ref)
    v2 = per_kv_head(v_ref, vp_ref)

    qi = lax.broadcasted_iota(jnp.int32, (w, 2 * w), 0)
    kj = lax.broadcasted_iota(jnp.int32, (w, 2 * w), 1)
    dist = qi + w - kj
    band = (dist >= 0) & (dist < WINDOW)
    zero = jnp.zeros((w, LANES), BF16)
    ones = jnp.ones((2 * w, LANES), BF16)
    zero_kv = jnp.zeros((2 * w, LANES), BF16)

    tile = lambda a: jnp.concatenate([a] * group, axis=0)
    bias_cur = tile(jnp.where(band[:, w:], 0.0, NEG))
    bias_prev = tile(jnp.where(band[:, :w], 0.0, NEG))
    sink_slot = lax.broadcasted_iota(jnp.int32, (group * w, LANES), 1) == 0
    first_row = lax.broadcasted_iota(jnp.int32, (2 * w, LANES), 0) == 0

    for kh in range(SWA_KV_HEADS):
        cols = [slice((kh * group // 2 + p) * LANES, (kh * group // 2 + p + 1) * LANES)
                for p in range(group // 2)]
        sink = jnp.concatenate([jnp.full((w, LANES), sinks_ref[kh * group + g], F32)
                                for g in range(group)], axis=0)
        bias_later = jnp.concatenate([jnp.where(sink_slot, sink, bias_prev), bias_cur], axis=-1)
        bias_first = jnp.concatenate(
            [jnp.where(sink_slot, sink, jnp.where(step > 0, bias_prev, NEG)), bias_cur], axis=-1)
        for t in range(SWA_ROWS // w):
            rows = slice(t * w, (t + 1) * w)
            bias = bias_first if t == 0 else bias_later
            k_w = jnp.where(first_row, zero_kv, k2[kh][t * w:(t + 2) * w])
            v_w = jnp.where(first_row, zero_kv, v2[kh][t * w:(t + 2) * w])
            parts = []
            for c in cols:
                qc = q_ref[rows, c]
                parts += [jnp.where(low, qc, zero), jnp.where(low, zero, qc)]
            s = _dot_nt(jnp.concatenate(parts, axis=0), k_w) + bias
            e = jnp.exp(s - jnp.max(s, axis=-1, keepdims=True))
            pv = _dot(e.astype(BF16), jnp.concatenate([v_w, ones], axis=-1))
            o = pv[:, :LANES] / pv[:, LANES:]
            for p, c in enumerate(cols):
                o_ref[rows, c] = jnp.where(low, o[2 * p * w:(2 * p + 1) * w],
                                           o[(2 * p + 1) * w:(2 * p + 2) * w]).astype(o_ref.dtype)


def _swa(proj, sinks):
    w, rows = WINDOW, SWA_ROWS
    nq, nkv = SWA_HEADS * SWA_HD, SWA_KV_HEADS * SWA_HD
    per = rows // w
    cur = lambda split: pl.BlockSpec((rows, nkv), lambda i, c=IN_OFFSETS[split] // nkv: (i, c))
    prev = lambda split: pl.BlockSpec(
        (w, nkv), lambda i, c=IN_OFFSETS[split] // nkv: (jnp.maximum(i * per - 1, 0), c))
    return pl.pallas_call(
        _swa_kernel,
        grid=(SEQ // rows,),
        in_specs=[pl.BlockSpec(memory_space=pltpu.SMEM),
                  pl.BlockSpec((rows, nq), lambda i: (i, IN_OFFSETS[Q_S] // nq)),
                  cur(K_S), cur(V_S), prev(K_S), prev(V_S)],
        out_specs=pl.BlockSpec((rows, nq), lambda i: (i, 0)),
        out_shape=jax.ShapeDtypeStruct((SEQ, nq), BF16),
        compiler_params=_params("parallel"),
        name="swa",
    )(sinks, proj, proj, proj, proj, proj)


def _mem_kv_kernel(mem_ref, g_ref, w_ref, o_ref):
    memn = (_rms(mem_ref[...]) * g_ref[...]).astype(BF16)
    o_ref[...] = _dot(memn, w_ref[...]).astype(o_ref.dtype)


def _mem_kv(mem2d, g, w):
    n_out = 2 * X_HEADS * X_HD
    return pl.pallas_call(
        _mem_kv_kernel,
        out_shape=jax.ShapeDtypeStruct((N_MEM, n_out), BF16),
        compiler_params=pltpu.CompilerParams(vmem_limit_bytes=VMEM_LIMIT),
        name="mem_kv",
    )(mem2d, g.reshape(1, D_MODEL), w)


def _merge_kernel(x_ref, yr_ref, ys_ref, *refs):
    n_gate = D_MODEL // GATE_BLOCK
    gr_refs, gs_refs = refs[:n_gate], refs[n_gate:2 * n_gate]
    (kv_ref, wur_ref, wus_ref, wo_ref, gx_ref, wxq_ref, wxo_ref, gf_ref, h_ref, n_ref) = refs[2 * n_gate:]
    gate_r = jnp.concatenate([r[...] for r in gr_refs], axis=-1).astype(F32)
    gate_s = jnp.concatenate([r[...] for r in gs_refs], axis=-1).astype(F32)
    up_r = _dot(yr_ref[...], wur_ref[...])
    up_s = _dot(ys_ref[...], wus_ref[...])
    merged = _sigmoid(gate_r) * up_r + _sigmoid(gate_s) * up_s
    h1 = x_ref[...] + _dot(merged.astype(BF16), wo_ref[...])

    nx = (_rms(h1) * gx_ref[...]).astype(BF16)
    qx = _dot(nx, wxq_ref[...]).astype(BF16)
    kv = kv_ref[...]
    outs = []
    for h in range(X_HEADS):
        q_h = qx[:, h * X_HD:(h + 1) * X_HD]
        k_h = kv[:, h * X_HD:(h + 1) * X_HD]
        v_h = kv[:, (X_HEADS + h) * X_HD:(X_HEADS + h + 1) * X_HD]
        s = _dot_nt(q_h, k_h) * (X_HD ** -0.5)
        e = jnp.exp(s - jnp.max(s, axis=-1, keepdims=True))
        p = e / jnp.sum(e, axis=-1, keepdims=True)
        outs.append(_dot(p.astype(BF16), v_h))
    att = jnp.concatenate(outs, axis=-1).astype(BF16)
    h2 = h1 + _dot(att, wxo_ref[...])
    h_ref[...] = h2
    n_ref[...] = (_rms(h2) * gf_ref[...]).astype(BF16)


def _merge(x2d, yr, ys, proj, kv, w_up_ret, w_up_swa, w_o, g_x, w_xq, w_xo, g_ffn):
    tm = MERGE_TM
    rows = lambda width, col=0: pl.BlockSpec((tm, width), lambda i, c=col: (i, c))
    n_gate = D_MODEL // GATE_BLOCK
    gate = lambda split: [rows(GATE_BLOCK, IN_OFFSETS[split] // GATE_BLOCK + b) for b in range(n_gate)]
    assert IN_OFFSETS[GATE_R] % GATE_BLOCK == 0 and IN_OFFSETS[GATE_S] % GATE_BLOCK == 0
    n_x = X_HEADS * X_HD
    return pl.pallas_call(
        _merge_kernel,
        grid=(SEQ // tm,),
        in_specs=[rows(D_MODEL), rows(RET_HEADS * RET_DV), rows(SWA_HEADS * SWA_HD),
                  *gate(GATE_R), *gate(GATE_S),
                  _resident((N_MEM, 2 * n_x)),
                  _resident((RET_HEADS * RET_DV, D_MODEL)), _resident((SWA_HEADS * SWA_HD, D_MODEL)),
                  _resident((D_MODEL, D_MODEL)), _resident((1, D_MODEL)),
                  _resident((D_MODEL, n_x)), _resident((n_x, D_MODEL)), _resident((1, D_MODEL))],
        out_specs=[rows(D_MODEL), rows(D_MODEL)],
        out_shape=[jax.ShapeDtypeStruct((SEQ, D_MODEL), F32),
                   jax.ShapeDtypeStruct((SEQ, D_MODEL), BF16)],
        compiler_params=_params("parallel"),
        name="merge_xattn",
    )(x2d, yr, ys, *([proj] * (2 * n_gate)), kv, w_up_ret, w_up_swa, w_o, g_x.reshape(1, D_MODEL),
      w_xq, w_xo, g_ffn.reshape(1, D_MODEL))


def _ffn_kernel(n_ref, h_hbm, wg_ref, wu_ref, wd_ref, gf_ref, o_ref, h_ref, h_sem):
    i, f = pl.program_id(0), pl.program_id(1)
    rows = pl.ds(pl.multiple_of(i * FFN_TM, FFN_TM), FFN_TM)
    residual_copy = pltpu.make_async_copy(h_hbm.at[rows, :], h_ref, h_sem)

    @pl.when(f == 0)
    def _():
        residual_copy.start()
        for c in range(FFN_TM // FFN_SLAB):
            o_ref[c * FFN_SLAB:(c + 1) * FFN_SLAB, :] = jnp.zeros((FFN_SLAB, D_MODEL), F32)

    n2 = n_ref[...]
    ts = []
    for c in range(FFN_TF // FFN_SLAB):
        cols = slice(c * FFN_SLAB, (c + 1) * FFN_SLAB)
        a = _dot(n2, wg_ref[:, cols])
        b = _dot(n2, wu_ref[:, cols])
        ts.append((a * _sigmoid(a) * b).astype(BF16))
    t = jnp.concatenate(ts, axis=-1)
    for c in range(D_MODEL // FFN_TF):
        cols = slice(c * FFN_TF, (c + 1) * FFN_TF)
        o_ref[:, cols] += _dot(t, wd_ref[:, cols])

    @pl.when(f == pl.num_programs(1) - 1)
    def _():
        residual_copy.wait()
        for c in range(FFN_TM // FFN_SLAB):
            r = slice(c * FFN_SLAB, (c + 1) * FFN_SLAB)
            o_ref[r, :] = _rms(h_ref[r, :] + o_ref[r, :]) * gf_ref[...]


def _ffn(n2, h2, w_gate, w_up, w_down, g_final):
    tm, tf = FFN_TM, FFN_TF
    return pl.pallas_call(
        _ffn_kernel,
        grid=(SEQ // tm, D_FF // tf),
        in_specs=[pl.BlockSpec((tm, D_MODEL), lambda i, f: (i, 0)),
                  pl.BlockSpec(memory_space=pl.ANY),
                  pl.BlockSpec((D_MODEL, tf), lambda i, f: (0, f)),
                  pl.BlockSpec((D_MODEL, tf), lambda i, f: (0, f)),
                  pl.BlockSpec((tf, D_MODEL), lambda i, f: (f, 0)),
                  pl.BlockSpec((1, D_MODEL), lambda i, f: (0, 0))],
        out_specs=pl.BlockSpec((tm, D_MODEL), lambda i, f: (i, 0)),
        out_shape=jax.ShapeDtypeStruct((SEQ, D_MODEL), F32),
        scratch_shapes=[pltpu.VMEM((tm, D_MODEL), F32), pltpu.SemaphoreType.DMA(())],
        compiler_params=_params("arbitrary", "arbitrary"),
        name="ffn",
    )(n2, h2, w_gate, w_up, w_down, g_final.reshape(1, D_MODEL))


def kernel(x, mem, positions, g_mix, w_in, w_up_ret, w_up_swa, sinks, w_o, g_x, g_mem,
           w_xq, w_xkv, w_xo, g_ffn, w_ffn_gate, w_ffn_up, w_ffn_down, g_final):
    assert x.shape == (1, SEQ, D_MODEL) and mem.shape == (1, N_MEM, D_MODEL)
    assert w_in.shape == (1, D_MODEL, D_IN)
    x2d = x.reshape(SEQ, D_MODEL)
    tables = _rope_tables(positions)
    later_weights = (w_ffn_gate, w_ffn_up, w_ffn_down, w_up_ret, w_up_swa, w_o, w_xkv)
    proj, *bf16_weights = _in_proj(x2d, g_mix[0], w_in[0], tables, tuple(w[0] for w in later_weights))
    w_gate, w_up, w_down, w_ur, w_us, w_out, w_kv = bf16_weights
    yr = _retention(proj)
    ys = _swa(proj, sinks[0])
    kv = _mem_kv(mem.reshape(N_MEM, D_MODEL), g_mem[0], w_kv)
    h2, n2 = _merge(x2d, yr, ys, proj, kv, w_ur, w_us, w_out, g_x[0],
                    w_xq[0].astype(BF16), w_xo[0].astype(BF16), g_ffn[0])
    out = _ffn(n2, h2, w_gate, w_up, w_down, g_final)
    return out.reshape(1, SEQ, D_MODEL)
```

```python
import jax
import jax.numpy as jnp
import numpy as np
from jax import lax
from jax.experimental import pallas as pl
from jax.experimental.pallas import tpu as pltpu

F32 = jnp.float32
BF16 = jnp.bfloat16

D_MODEL = 2048
SEQ = 16384
N_MEM = 256
EPS = 1e-6
RET_HEADS = 4
RET_DK = 256
RET_DV = 256
RET_THETA = 10000.0
SWA_HEADS = 16
SWA_KV_HEADS = 4
SWA_HD = 64
WINDOW = 128
ROPE_THETA = 500000.0
ROPE_DIM = SWA_HD // 4
X_HEADS = 4
X_HD = 128
D_FF = -(-8 * D_MODEL // (3 * 256)) * 256
IN_SPLITS = (RET_HEADS * RET_DK, RET_HEADS * RET_DK, RET_HEADS * RET_DV, RET_HEADS * RET_DV,
             SWA_HEADS * SWA_HD, SWA_KV_HEADS * SWA_HD, SWA_KV_HEADS * SWA_HD, D_MODEL, D_MODEL)
D_IN = sum(IN_SPLITS)
IN_OFFSETS = tuple(int(o) for o in np.cumsum((0,) + IN_SPLITS[:-1]))
Q_R, K_R, V_R, G_R, Q_S, K_S, V_S, GATE_R, GATE_S = range(9)
GATE_BLOCK = 512
NEG = -1e30

LANES = 128
BF16_SUBLANES = 16
MXU_WIDTH = 256
VMEM_LIMIT = 58 * 1024 * 1024

RET_CHUNK = 256
RET_ROWS = 1024
ROPE_ROWS = 2048
PROJ_TM, PROJ_TN = 2048, 512
NORM_SLAB = 256
CAST_STEPS_PER_ROW = 16
CAST_ROW_BLOCKS, CAST_COL_BLOCKS = 32, 4
SWA_ROWS = 512
MERGE_TM = 256
FFN_TM, FFN_TF = 1024, 512
FFN_SLAB = MXU_WIDTH


def _sigmoid(x):
    return 1.0 / (1.0 + jnp.exp(-x))


def _rms(x):
    return x * lax.rsqrt(jnp.mean(x * x, axis=-1, keepdims=True) + EPS)


def _dot(a, b):
    return jnp.dot(a, b, preferred_element_type=F32)


def _dot_nt(a, b):
    return lax.dot_general(a, b, (((1,), (1,)), ((), ())), preferred_element_type=F32)


def _dot_tn(a, b):
    return lax.dot_general(a, b, (((0,), (0,)), ((), ())), preferred_element_type=F32)


def _params(*sem):
    return pltpu.CompilerParams(dimension_semantics=sem, vmem_limit_bytes=VMEM_LIMIT)


def _resident(shape):
    nd = len(shape)
    return pl.BlockSpec(shape, lambda *_: (0,) * nd, pipeline_mode=pl.Buffered(1))


def _rope_tables_kernel(pos_ref, invr_ref, invs_ref, sign_ref, rc_ref, rs_ref, sc_ref, ss_ref):
    pos = pos_ref[...].astype(F32)
    ang_r = pos * invr_ref[...]
    rc_ref[...] = jnp.cos(ang_r)
    rs_ref[...] = jnp.sin(ang_r)
    ang_s = pos * invs_ref[...]
    sc_ref[...] = jnp.cos(ang_s)
    ss_ref[...] = jnp.sin(ang_s) * sign_ref[...]


def _rope_tables(positions):
    pos = positions.reshape(SEQ, 1)
    half_r = RET_DK // 2
    inv_r = 1.0 / (RET_THETA ** (jnp.arange(half_r, dtype=F32) / half_r))
    half_s = ROPE_DIM // 2
    inv_s = 1.0 / (ROPE_THETA ** (jnp.arange(half_s, dtype=F32) / half_s))
    d = np.arange(LANES) % SWA_HD
    inv_s_lanes = jnp.where(d < ROPE_DIM, inv_s[d % half_s], 0.0).astype(F32)
    sign = jnp.asarray(np.where(d < half_s, -1.0, np.where(d < ROPE_DIM, 1.0, 0.0)).astype(np.float32))
    row = lambda v: v.reshape(1, LANES)
    tab = jax.ShapeDtypeStruct((SEQ, LANES), F32)
    vec = pl.BlockSpec((1, LANES), lambda i: (0, 0))
    blk = pl.BlockSpec((ROPE_ROWS, LANES), lambda i: (i, 0))
    return pl.pallas_call(
        _rope_tables_kernel,
        grid=(SEQ // ROPE_ROWS,),
        in_specs=[pl.BlockSpec((ROPE_ROWS, 1), lambda i: (i, 0)), vec, vec, vec],
        out_specs=[blk] * 4,
        out_shape=[tab] * 4,
        compiler_params=_params("parallel"),
        name="rope_tables",
    )(pos, row(inv_r), row(inv_s_lanes), row(sign))


def _in_proj_kernel(x_hbm, g_ref, w_ref, rc_ref, rs_ref, sc_ref, ss_ref, *refs):
    tm, tn = PROJ_TM, PROJ_TN
    i, j = pl.program_id(0), pl.program_id(1)
    blk = lambda split: IN_OFFSETS[split] // tn
    *io_refs, n_ref, acc_ref, x_buf, x_sem = refs
    n_cast = len(io_refs) // 2
    cast_in, o_ref, cast_out = io_refs[:n_cast], io_refs[n_cast], io_refs[n_cast + 1:]
    for src, dst in zip(cast_in, cast_out):
        dst[...] = src[...].astype(BF16)

    def x_copy(block):
        rows = pl.ds(pl.multiple_of(block * tm, tm), tm)
        return pltpu.make_async_copy(x_hbm.at[rows, :], x_buf, x_sem)

    @pl.when(j == 0)
    def _():
        @pl.when(i == 0)
        def _():
            x_copy(0).start()

        x_copy(i).wait()
        for c in range(tm // NORM_SLAB):
            r = slice(c * NORM_SLAB, (c + 1) * NORM_SLAB)
            n_ref[r, :] = (_rms(x_buf[r, :]) * g_ref[...]).astype(BF16)

    @pl.when((j == 1) & (i + 1 < pl.num_programs(0)))
    def _():
        x_copy(i + 1).start()

    acc_ref[...] = _dot(n_ref[...], w_ref[...].astype(BF16))
    o_ref[...] = acc_ref[...].astype(BF16)
    is_ret = (j >= blk(Q_R)) & (j < blk(V_R))
    is_qs = (j >= blk(Q_S)) & (j < blk(K_S))
    is_kvs = j == blk(K_S)

    def swa_rope(cols):
        half = ROPE_DIM // 2
        xt = acc_ref[:, cols]
        assert SWA_HD & (SWA_HD - 1) == 0
        takes_upper = (lax.broadcasted_iota(jnp.int32, xt.shape, 1) & (SWA_HD - 1)) < half
        partner = jnp.where(takes_upper, pltpu.roll(xt, LANES - half, 1), pltpu.roll(xt, half, 1))
        return xt * sc_ref[...] + partner * ss_ref[...]

    @pl.when(is_ret)
    def _():
        scale = jnp.where(j >= blk(K_R), RET_DK ** -0.5, 1.0).astype(F32)
        cos, sin = rc_ref[...] * scale, rs_ref[...] * scale
        half = RET_DK // 2
        for h in range(tn // RET_DK):
            a1 = acc_ref[:, h * RET_DK:h * RET_DK + half]
            a2 = acc_ref[:, h * RET_DK + half:(h + 1) * RET_DK]
            o_ref[:, h * RET_DK:h * RET_DK + half] = (a1 * cos - a2 * sin).astype(BF16)
            o_ref[:, h * RET_DK + half:(h + 1) * RET_DK] = (a2 * cos + a1 * sin).astype(BF16)

    @pl.when(is_qs)
    def _():
        for t in range(tn // LANES):
            cols = slice(t * LANES, (t + 1) * LANES)
            o_ref[:, cols] = (swa_rope(cols) * (SWA_HD ** -0.5)).astype(BF16)

    @pl.when(is_kvs)
    def _():
        for t in range(IN_SPLITS[K_S] // LANES):
            cols = slice(t * LANES, (t + 1) * LANES)
            o_ref[:, cols] = swa_rope(cols).astype(BF16)


def _in_proj(x2d, g, w, tables, cast_weights):
    tm, tn = PROJ_TM, PROJ_TN
    assert all(IN_OFFSETS[s] % tn == 0 for s in (Q_R, K_R, V_R, Q_S, K_S))
    assert IN_OFFSETS[V_S] == IN_OFFSETS[K_S] + IN_SPLITS[K_S] and IN_SPLITS[K_S] + IN_SPLITS[V_S] == tn
    tab = pl.BlockSpec((tm, LANES), lambda i, j: (i, 0))
    n_i, n_j = SEQ // tm, D_IN // tn
    assert n_i * CAST_STEPS_PER_ROW == CAST_ROW_BLOCKS * CAST_COL_BLOCKS and CAST_STEPS_PER_ROW <= n_j

    def cast_spec(shape):
        rows, cols = shape[0] // CAST_ROW_BLOCKS, shape[1] // CAST_COL_BLOCKS
        assert rows % BF16_SUBLANES == 0 and cols % LANES == 0 and rows * CAST_ROW_BLOCKS == shape[0]

        def index(i, j):
            b = i * CAST_STEPS_PER_ROW + jnp.minimum(j, CAST_STEPS_PER_ROW - 1)
            return b // CAST_COL_BLOCKS, b % CAST_COL_BLOCKS
        return pl.BlockSpec((rows, cols), index)

    cast_specs = [cast_spec(cw.shape) for cw in cast_weights]
    return pl.pallas_call(
        _in_proj_kernel,
        grid=(SEQ // tm, D_IN // tn),
        in_specs=[pl.BlockSpec(memory_space=pl.ANY),
                  pl.BlockSpec((1, D_MODEL), lambda i, j: (0, 0)),
                  pl.BlockSpec((D_MODEL, tn), lambda i, j: (0, j)),
                  tab, tab, tab, tab, *cast_specs],
        out_specs=[pl.BlockSpec((tm, tn), lambda i, j: (i, j)), *cast_specs],
        out_shape=[jax.ShapeDtypeStruct((SEQ, D_IN), BF16)]
                  + [jax.ShapeDtypeStruct(cw.shape, BF16) for cw in cast_weights],
        scratch_shapes=[pltpu.VMEM((tm, D_MODEL), BF16), pltpu.VMEM((tm, tn), F32),
                        pltpu.VMEM((tm, D_MODEL), F32), pltpu.SemaphoreType.DMA(())],
        compiler_params=_params("arbitrary", "arbitrary"),
        name="in_proj",
    )(x2d, g.reshape(1, D_MODEL), w, *tables, *cast_weights)


def _retention_kernel(logg_ref, q_ref, k_ref, v_ref, g_ref, o_ref, state_ref, dmask_ref):
    c_rows = RET_CHUNK
    idx = lax.broadcasted_iota(jnp.int32, (c_rows, 1), 0).astype(F32)

    @pl.when(pl.program_id(0) == 0)
    def _():
        state_ref[...] = jnp.zeros_like(state_ref)
        i = lax.broadcasted_iota(jnp.int32, (c_rows, c_rows), 0)
        j = lax.broadcasted_iota(jnp.int32, (c_rows, c_rows), 1)
        rel = (i - j).astype(F32)
        for h in range(RET_HEADS):
            dmask_ref[h] = jnp.where(rel >= 0, jnp.exp(logg_ref[h] * jnp.maximum(rel, 0.0)), 0.0)

    for h in range(RET_HEADS):
        lg = logg_ref[h]
        cols = slice(h * RET_DK, (h + 1) * RET_DK)
        q_dec = jnp.exp(lg * (idx + 1.0))
        k_dec = jnp.exp(lg * (c_rows - 1.0 - idx))
        chunk_decay = jnp.exp(jnp.full((1, RET_DV), lg * c_rows, F32))
        for c in range(RET_ROWS // c_rows):
            r = slice(c * c_rows, (c + 1) * c_rows)
            q, k, v = q_ref[r, cols], k_ref[r, cols], v_ref[r, cols]
            scores = _dot_nt(q, k) * dmask_ref[h]
            inner = _dot(scores.astype(BF16), v)
            state = state_ref[h]
            cross = _dot((q.astype(F32) * q_dec).astype(BF16), state.astype(BF16))
            state_ref[h] = state * chunk_decay + _dot_tn((k.astype(F32) * k_dec).astype(BF16), v)
            y = _rms(inner + cross)
            g = g_ref[r, cols].astype(F32)
            o_ref[r, cols] = (y * (g * _sigmoid(g))).astype(o_ref.dtype)


def _retention(proj):
    log_g = jnp.log(1.0 - jnp.power(2.0, -5.0 - jnp.arange(RET_HEADS, dtype=F32)))
    rows, width = RET_ROWS, RET_HEADS * RET_DK
    assert RET_DK == RET_DV
    all_heads = lambda split: pl.BlockSpec((rows, width), lambda n, c=IN_OFFSETS[split] // width: (n, c))
    assert all(IN_OFFSETS[s] % width == 0 for s in (Q_R, K_R, V_R, G_R))
    return pl.pallas_call(
        _retention_kernel,
        grid=(SEQ // rows,),
        in_specs=[pl.BlockSpec(memory_space=pltpu.SMEM),
                  all_heads(Q_R), all_heads(K_R), all_heads(V_R), all_heads(G_R)],
        out_specs=pl.BlockSpec((rows, width), lambda n: (n, 0)),
        out_shape=jax.ShapeDtypeStruct((SEQ, width), BF16),
        scratch_shapes=[pltpu.VMEM((RET_HEADS, RET_DK, RET_DV), F32),
                        pltpu.VMEM((RET_HEADS, RET_CHUNK, RET_CHUNK), F32)],
        compiler_params=_params("arbitrary"),
        name="retention",
    )(log_g, proj, proj, proj, proj)


def _swa_kernel(sinks_ref, q_ref, k_ref, v_ref, kp_ref, vp_ref, o_ref):
    w = WINDOW
    hd = SWA_HD
    group = SWA_HEADS // SWA_KV_HEADS
    step = pl.program_id(0)
    low = lax.broadcasted_iota(jnp.int32, (w, LANES), 1) < hd

    def dup_halves(x):
        xf = x.astype(F32)
        sw = pltpu.roll(xf, hd, 1)
        lo = lax.broadcasted_iota(jnp.int32, xf.shape, 1) < hd
        return jnp.where(lo, xf, sw).astype(BF16), jnp.where(lo, sw, xf).astype(BF16)

    def per_kv_head(cur_ref, prev_ref):
        allrows = jnp.concatenate([prev_ref[...], cur_ref[...]], axis=0)
        heads = []
        for c in range(allrows.shape[1] // LANES):
            heads.extend(dup_halves(allrows[:, c * LANES:(c + 1) * LANES]))
        return heads

    k2 = per_kv_head(k_ref, kp_ref)
    v2 = per_kv_head(v_ref, vp_ref)

    qi = lax.broadcasted_iota(jnp.int32, (w, 2 * w), 0)
    kj = lax.broadcasted_iota(jnp.int32, (w, 2 * w), 1)
    dist = qi + w - kj
    band = (dist >= 0) & (dist < WINDOW)
    zero = jnp.zeros((w, LANES), BF16)
    ones = jnp.ones((2 * w, LANES), BF16)
    zero_kv = jnp.zeros((2 * w, LANES), BF16)

    tile = lambda a: jnp.concatenate([a] * group, axis=0)
    bias_cur = tile(jnp.where(band[:, w:], 0.0, NEG))
    bias_prev = tile(jnp.where(band[:, :w], 0.0, NEG))
    sink_slot = lax.broadcasted_iota(jnp.int32, (group * w, LANES), 1) == 0
    first_row = lax.broadcasted_iota(jnp.int32, (2 * w, LANES), 0) == 0

    for kh in range(SWA_KV_HEADS):
        cols = [slice((kh * group // 2 + p) * LANES, (kh * group // 2 + p + 1) * LANES)
                for p in range(group // 2)]
        sink = jnp.concatenate([jnp.full((w, LANES), sinks_ref[kh * group + g], F32)
                                for g in range(group)], axis=0)
        bias_later = jnp.concatenate([jnp.where(sink_slot, sink, bias_prev), bias_cur], axis=-1)
        bias_first = jnp.concatenate(
            [jnp.where(sink_slot, sink, jnp.where(step > 0, bias_prev, NEG)), bias_cur], axis=-1)
        for t in range(SWA_ROWS // w):
            rows = slice(t * w, (t + 1) * w)
            bias = bias_first if t == 0 else bias_later
            k_w = jnp.where(first_row, zero_kv, k2[kh][t * w:(t + 2) * w])
            v_w = jnp.where(first_row, zero_kv, v2[kh][t * w:(t + 2) * w])
            parts = []
            for c in cols:
                qc = q_ref[rows, c]
                parts += [jnp.where(low, qc, zero), jnp.where(low, zero, qc)]
            s = _dot_nt(jnp.concatenate(parts, axis=0), k_w) + bias
            e = jnp.exp(s - jnp.max(s, axis=-1, keepdims=True))
            pv = _dot(e.astype(BF16), jnp.concatenate([v_w, ones], axis=-1))
            o = pv[:, :LANES] / pv[:, LANES:]
            for p, c in enumerate(cols):
                o_ref[rows, c] = jnp.where(low, o[2 * p * w:(2 * p + 1) * w],
                                           o[(2 * p + 1) * w:(2 * p + 2) * w]).astype(o_ref.dtype)


def _swa(proj, sinks):
    w, rows = WINDOW, SWA_ROWS
    assert 2 * SWA_HD == LANES and WINDOW == LANES and (SWA_HEADS // SWA_KV_HEADS) % 2 == 0
    nq, nkv = SWA_HEADS * SWA_HD, SWA_KV_HEADS * SWA_HD
    per = rows // w
    cur = lambda split: pl.BlockSpec((rows, nkv), lambda i, c=IN_OFFSETS[split] // nkv: (i, c))
    prev = lambda split: pl.BlockSpec(
        (w, nkv), lambda i, c=IN_OFFSETS[split] // nkv: (jnp.maximum(i * per - 1, 0), c))
    return pl.pallas_call(
        _swa_kernel,
        grid=(SEQ // rows,),
        in_specs=[pl.BlockSpec(memory_space=pltpu.SMEM),
                  pl.BlockSpec((rows, nq), lambda i: (i, IN_OFFSETS[Q_S] // nq)),
                  cur(K_S), cur(V_S), prev(K_S), prev(V_S)],
        out_specs=pl.BlockSpec((rows, nq), lambda i: (i, 0)),
        out_shape=jax.ShapeDtypeStruct((SEQ, nq), BF16),
        compiler_params=_params("parallel"),
        name="swa",
    )(sinks, proj, proj, proj, proj, proj)


def _mem_kv_kernel(mem_ref, g_ref, w_ref, o_ref):
    memn = (_rms(mem_ref[...]) * g_ref[...]).astype(BF16)
    o_ref[...] = _dot(memn, w_ref[...]).astype(o_ref.dtype)


def _mem_kv(mem2d, g, w):
    n_out = 2 * X_HEADS * X_HD
    return pl.pallas_call(
        _mem_kv_kernel,
        out_shape=jax.ShapeDtypeStruct((N_MEM, n_out), BF16),
        compiler_params=pltpu.CompilerParams(vmem_limit_bytes=VMEM_LIMIT),
        name="mem_kv",
    )(mem2d, g.reshape(1, D_MODEL), w)


def _merge_kernel(x_ref, yr_ref, ys_ref, *refs):
    n_gate = D_MODEL // GATE_BLOCK
    gr_refs, gs_refs = refs[:n_gate], refs[n_gate:2 * n_gate]
    (kv_ref, wur_ref, wus_ref, wo_ref, gx_ref, wxq_ref, wxo_ref, gf_ref, h_ref, n_ref) = refs[2 * n_gate:]
    gate_r = jnp.concatenate([r[...] for r in gr_refs], axis=-1).astype(F32)
    gate_s = jnp.concatenate([r[...] for r in gs_refs], axis=-1).astype(F32)
    up_r = _dot(yr_ref[...], wur_ref[...])
    up_s = _dot(ys_ref[...], wus_ref[...])
    merged = _sigmoid(gate_r) * up_r + _sigmoid(gate_s) * up_s
    h1 = x_ref[...] + _dot(merged.astype(BF16), wo_ref[...])

    nx = (_rms(h1) * gx_ref[...]).astype(BF16)
    qx = _dot(nx, wxq_ref[...]).astype(BF16)
    kv = kv_ref[...]
    outs = []
    for h in range(X_HEADS):
        q_h = qx[:, h * X_HD:(h + 1) * X_HD]
        k_h = kv[:, h * X_HD:(h + 1) * X_HD]
        v_h = kv[:, (X_HEADS + h) * X_HD:(X_HEADS + h + 1) * X_HD]
        s = _dot_nt(q_h, k_h) * (X_HD ** -0.5)
        e = jnp.exp(s - jnp.max(s, axis=-1, keepdims=True))
        p = e / jnp.sum(e, axis=-1, keepdims=True)
        outs.append(_dot(p.astype(BF16), v_h))
    att = jnp.concatenate(outs, axis=-1).astype(BF16)
    h2 = h1 + _dot(att, wxo_ref[...])
    h_ref[...] = h2
    n_ref[...] = (_rms(h2) * gf_ref[...]).astype(BF16)


def _merge(x2d, yr, ys, proj, kv, w_up_ret, w_up_swa, w_o, g_x, w_xq, w_xo, g_ffn):
    tm = MERGE_TM
    rows = lambda width, col=0: pl.BlockSpec((tm, width), lambda i, c=col: (i, c))
    n_gate = D_MODEL // GATE_BLOCK
    gate = lambda split: [rows(GATE_BLOCK, IN_OFFSETS[split] // GATE_BLOCK + b) for b in range(n_gate)]
    assert IN_OFFSETS[GATE_R] % GATE_BLOCK == 0 and IN_OFFSETS[GATE_S] % GATE_BLOCK == 0
    n_x = X_HEADS * X_HD
    return pl.pallas_call(
        _merge_kernel,
        grid=(SEQ // tm,),
        in_specs=[rows(D_MODEL), rows(RET_HEADS * RET_DV), rows(SWA_HEADS * SWA_HD),
                  *gate(GATE_R), *gate(GATE_S),
                  _resident((N_MEM, 2 * n_x)),
                  _resident((RET_HEADS * RET_DV, D_MODEL)), _resident((SWA_HEADS * SWA_HD, D_MODEL)),
                  _resident((D_MODEL, D_MODEL)), _resident((1, D_MODEL)),
                  _resident((D_MODEL, n_x)), _resident((n_x, D_MODEL)), _resident((1, D_MODEL))],
        out_specs=[rows(D_MODEL), rows(D_MODEL)],
        out_shape=[jax.ShapeDtypeStruct((SEQ, D_MODEL), F32),
                   jax.ShapeDtypeStruct((SEQ, D_MODEL), BF16)],
        compiler_params=_params("parallel"),
        name="merge_xattn",
    )(x2d, yr, ys, *([proj] * (2 * n_gate)), kv, w_up_ret, w_up_swa, w_o, g_x.reshape(1, D_MODEL),
      w_xq, w_xo, g_ffn.reshape(1, D_MODEL))


def _ffn_kernel(n_ref, h_hbm, wg_ref, wu_ref, wd_ref, gf_ref, o_ref, h_ref, h_sem):
    i, f = pl.program_id(0), pl.program_id(1)
    rows = pl.ds(pl.multiple_of(i * FFN_TM, FFN_TM), FFN_TM)
    residual_copy = pltpu.make_async_copy(h_hbm.at[rows, :], h_ref, h_sem)

    @pl.when(f == 0)
    def _():
        residual_copy.start()
        for c in range(FFN_TM // FFN_SLAB):
            o_ref[c * FFN_SLAB:(c + 1) * FFN_SLAB, :] = jnp.zeros((FFN_SLAB, D_MODEL), F32)

    n2 = n_ref[...]
    ts = []
    for c in range(FFN_TF // FFN_SLAB):
        cols = slice(c * FFN_SLAB, (c + 1) * FFN_SLAB)
        a = _dot(n2, wg_ref[:, cols])
        b = _dot(n2, wu_ref[:, cols])
        ts.append((a * _sigmoid(a) * b).astype(BF16))
    t = jnp.concatenate(ts, axis=-1)
    for c in range(D_MODEL // FFN_TF):
        cols = slice(c * FFN_TF, (c + 1) * FFN_TF)
        o_ref[:, cols] += _dot(t, wd_ref[:, cols])

    @pl.when(f == pl.num_programs(1) - 1)
    def _():
        residual_copy.wait()
        for c in range(FFN_TM // FFN_SLAB):
            r = slice(c * FFN_SLAB, (c + 1) * FFN_SLAB)
            o_ref[r, :] = _rms(h_ref[r, :] + o_ref[r, :]) * gf_ref[...]


def _ffn(n2, h2, w_gate, w_up, w_down, g_final):
    tm, tf = FFN_TM, FFN_TF
    return pl.pallas_call(
        _ffn_kernel,
        grid=(SEQ // tm, D_FF // tf),
        in_specs=[pl.BlockSpec((tm, D_MODEL), lambda i, f: (i, 0)),
                  pl.BlockSpec(memory_space=pl.ANY),
                  pl.BlockSpec((D_MODEL, tf), lambda i, f: (0, f)),
                  pl.BlockSpec((D_MODEL, tf), lambda i, f: (0, f)),
                  pl.BlockSpec((tf, D_MODEL), lambda i, f: (f, 0)),
                  pl.BlockSpec((1, D_MODEL), lambda i, f: (0, 0))],
        out_specs=pl.BlockSpec((tm, D_MODEL), lambda i, f: (i, 0)),
        out_shape=jax.ShapeDtypeStruct((SEQ, D_MODEL), F32),
        scratch_shapes=[pltpu.VMEM((tm, D_MODEL), F32), pltpu.SemaphoreType.DMA(())],
        compiler_params=_params("arbitrary", "arbitrary"),
        name="ffn",
    )(n2, h2, w_gate, w_up, w_down, g_final.reshape(1, D_MODEL))


def kernel(x, mem, positions, g_mix, w_in, w_up_ret, w_up_swa, sinks, w_o, g_x, g_mem,
           w_xq, w_xkv, w_xo, g_ffn, w_ffn_gate, w_ffn_up, w_ffn_down, g_final):
    assert x.shape == (1, SEQ, D_MODEL) and mem.shape == (1, N_MEM, D_MODEL)
    assert w_in.shape == (1, D_MODEL, D_IN)
    x2d = x.reshape(SEQ, D_MODEL)
    tables = _rope_tables(positions)
    later_weights = (w_ffn_gate, w_ffn_up, w_ffn_down, w_up_ret, w_up_swa, w_o, w_xkv)
    proj, *bf16_weights = _in_proj(x2d, g_mix[0], w_in[0], tables, tuple(w[0] for w in later_weights))
    w_gate, w_up, w_down, w_ur, w_us, w_out, w_kv = bf16_weights
    yr = _retention(proj)
    ys = _swa(proj, sinks[0])
    kv = _mem_kv(mem.reshape(N_MEM, D_MODEL), g_mem[0], w_kv)
    h2, n2 = _merge(x2d, yr, ys, proj, kv, w_ur, w_us, w_out, g_x[0],
                    w_xq[0].astype(BF16), w_xo[0].astype(BF16), g_ffn[0])
    out = _ffn(n2, h2, w_gate, w_up, w_down, g_final)
    return out.reshape(1, SEQ, D_MODEL)
```

```python
import jax
import jax.numpy as jnp
import numpy as np
from jax import lax
from jax.experimental import pallas as pl
from jax.experimental.pallas import tpu as pltpu

F32 = jnp.float32
BF16 = jnp.bfloat16

D_MODEL = 2048
SEQ = 16384
N_MEM = 256
EPS = 1e-6
RET_HEADS = 4
RET_DK = 256
RET_DV = 256
RET_THETA = 10000.0
SWA_HEADS = 16
SWA_KV_HEADS = 4
SWA_HD = 64
WINDOW = 128
ROPE_THETA = 500000.0
ROPE_DIM = SWA_HD // 4
X_HEADS = 4
X_HD = 128
D_FF = -(-8 * D_MODEL // (3 * 256)) * 256
IN_SPLITS = (RET_HEADS * RET_DK, RET_HEADS * RET_DK, RET_HEADS * RET_DV, RET_HEADS * RET_DV,
             SWA_HEADS * SWA_HD, SWA_KV_HEADS * SWA_HD, SWA_KV_HEADS * SWA_HD, D_MODEL, D_MODEL)
D_IN = sum(IN_SPLITS)
IN_OFFSETS = tuple(int(o) for o in np.cumsum((0,) + IN_SPLITS[:-1]))
Q_R, K_R, V_R, G_R, Q_S, K_S, V_S, GATE_R, GATE_S = range(9)
GATE_BLOCK = 512
NEG = -1e30

LANES = 128
BF16_SUBLANES = 16
MXU_WIDTH = 256
VMEM_LIMIT = 58 * 1024 * 1024

RET_CHUNK = 256
RET_ROWS = 1024
ROPE_ROWS = 2048
PROJ_TM, PROJ_TN = 2048, 512
NORM_SLAB = 256
ROPE_SLAB = 512
CAST_STEPS_PER_ROW = 16
CAST_ROW_BLOCKS, CAST_COL_BLOCKS = 32, 4
SWA_ROWS = 512
MERGE_TM = 256
FFN_TM, FFN_TF = 1024, 512
FFN_SLAB = MXU_WIDTH


def _sigmoid(x):
    return 1.0 / (1.0 + jnp.exp(-x))


def _rms(x):
    return x * lax.rsqrt(jnp.mean(x * x, axis=-1, keepdims=True) + EPS)


def _dot(a, b):
    return jnp.dot(a, b, preferred_element_type=F32)


def _dot_nt(a, b):
    return lax.dot_general(a, b, (((1,), (1,)), ((), ())), preferred_element_type=F32)


def _dot_tn(a, b):
    return lax.dot_general(a, b, (((0,), (0,)), ((), ())), preferred_element_type=F32)


def _params(*sem):
    return pltpu.CompilerParams(dimension_semantics=sem, vmem_limit_bytes=VMEM_LIMIT)


def _resident(shape):
    nd = len(shape)
    return pl.BlockSpec(shape, lambda *_: (0,) * nd, pipeline_mode=pl.Buffered(1))


def _rope_tables_kernel(pos_ref, invr_ref, invs_ref, sign_ref, rc_ref, rs_ref, sc_ref, ss_ref):
    pos = pos_ref[...].astype(F32)
    ang_r = pos * invr_ref[...]
    rc_ref[...] = jnp.cos(ang_r)
    rs_ref[...] = jnp.sin(ang_r)
    ang_s = pos * invs_ref[...]
    sc_ref[...] = jnp.cos(ang_s)
    ss_ref[...] = jnp.sin(ang_s) * sign_ref[...]


def _rope_tables(positions):
    pos = positions.reshape(SEQ, 1)
    half_r = RET_DK // 2
    inv_r = 1.0 / (RET_THETA ** (jnp.arange(half_r, dtype=F32) / half_r))
    half_s = ROPE_DIM // 2
    inv_s = 1.0 / (ROPE_THETA ** (jnp.arange(half_s, dtype=F32) / half_s))
    d = np.arange(LANES) % SWA_HD
    inv_s_lanes = jnp.where(d < ROPE_DIM, inv_s[d % half_s], 0.0).astype(F32)
    sign = jnp.asarray(np.where(d < half_s, -1.0, np.where(d < ROPE_DIM, 1.0, 0.0)).astype(np.float32))
    row = lambda v: v.reshape(1, LANES)
    tab = jax.ShapeDtypeStruct((SEQ, LANES), F32)
    vec = pl.BlockSpec((1, LANES), lambda i: (0, 0))
    blk = pl.BlockSpec((ROPE_ROWS, LANES), lambda i: (i, 0))
    return pl.pallas_call(
        _rope_tables_kernel,
        grid=(SEQ // ROPE_ROWS,),
        in_specs=[pl.BlockSpec((ROPE_ROWS, 1), lambda i: (i, 0)), vec, vec, vec],
        out_specs=[blk] * 4,
        out_shape=[tab] * 4,
        compiler_params=_params("parallel"),
        name="rope_tables",
    )(pos, row(inv_r), row(inv_s_lanes), row(sign))


def _in_proj_kernel(x_hbm, g_ref, w_ref, rc_ref, rs_ref, sc_ref, ss_ref, perm_ref, *refs):
    tm, tn = PROJ_TM, PROJ_TN
    i, j = pl.program_id(0), pl.program_id(1)
    blk = lambda split: IN_OFFSETS[split] // tn
    *io_refs, n_ref, acc_ref, x_buf, x_sem = refs
    n_cast = len(io_refs) // 2
    cast_in, o_ref, cast_out = io_refs[:n_cast], io_refs[n_cast], io_refs[n_cast + 1:]
    for src, dst in zip(cast_in, cast_out):
        dst[...] = src[...].astype(BF16)

    def x_copy(block):
        rows = pl.ds(pl.multiple_of(block * tm, tm), tm)
        return pltpu.make_async_copy(x_hbm.at[rows, :], x_buf, x_sem)

    @pl.when(j == 0)
    def _():
        @pl.when(i == 0)
        def _():
            x_copy(0).start()

        x_copy(i).wait()
        for c in range(tm // NORM_SLAB):
            r = slice(c * NORM_SLAB, (c + 1) * NORM_SLAB)
            n_ref[r, :] = (_rms(x_buf[r, :]) * g_ref[...]).astype(BF16)

    @pl.when((j == 1) & (i + 1 < pl.num_programs(0)))
    def _():
        x_copy(i + 1).start()

    acc_ref[...] = _dot(n_ref[...], w_ref[...].astype(BF16))
    o_ref[...] = acc_ref[...].astype(BF16)
    is_ret = (j >= blk(Q_R)) & (j < blk(V_R))
    is_qs = (j >= blk(Q_S)) & (j < blk(K_S))
    is_kvs = j == blk(K_S)

    def swa_rope(pair, scale):
        cols = slice(pair * MXU_WIDTH, (pair + 1) * MXU_WIDTH)
        for c in range(tm // ROPE_SLAB):
            r = slice(c * ROPE_SLAB, (c + 1) * ROPE_SLAB)
            xt = acc_ref[r, cols]
            hi = xt.astype(BF16)
            lo = (xt - hi.astype(F32)).astype(BF16)
            partner = _dot(hi, perm_ref[...]) + _dot(lo, perm_ref[...])
            cos = jnp.concatenate([sc_ref[r, :]] * (MXU_WIDTH // LANES), axis=-1)
            sin = jnp.concatenate([ss_ref[r, :]] * (MXU_WIDTH // LANES), axis=-1)
            o_ref[r, cols] = ((xt * cos + partner * sin) * scale).astype(BF16)

    @pl.when(is_ret)
    def _():
        scale = jnp.where(j >= blk(K_R), RET_DK ** -0.5, 1.0).astype(F32)
        cos, sin = rc_ref[...] * scale, rs_ref[...] * scale
        half = RET_DK // 2
        for h in range(tn // RET_DK):
            a1 = acc_ref[:, h * RET_DK:h * RET_DK + half]
            a2 = acc_ref[:, h * RET_DK + half:(h + 1) * RET_DK]
            o_ref[:, h * RET_DK:h * RET_DK + half] = (a1 * cos - a2 * sin).astype(BF16)
            o_ref[:, h * RET_DK + half:(h + 1) * RET_DK] = (a2 * cos + a1 * sin).astype(BF16)

    @pl.when(is_qs)
    def _():
        for pair in range(tn // MXU_WIDTH):
            swa_rope(pair, SWA_HD ** -0.5)

    @pl.when(is_kvs)
    def _():
        for pair in range(IN_SPLITS[K_S] // MXU_WIDTH):
            swa_rope(pair, 1.0)


def _rope_partner_matrix():
    half = ROPE_DIM // 2
    p = np.zeros((MXU_WIDTH, MXU_WIDTH), np.float32)
    for lane in range(MXU_WIDTH):
        d = lane % SWA_HD
        if d < half:
            p[lane + half, lane] = 1.0
        elif d < ROPE_DIM:
            p[lane - half, lane] = 1.0
    return jnp.asarray(p, BF16)


def _in_proj(x2d, g, w, tables, cast_weights):
    tm, tn = PROJ_TM, PROJ_TN
    assert all(IN_OFFSETS[s] % tn == 0 for s in (Q_R, K_R, V_R, Q_S, K_S))
    assert IN_OFFSETS[V_S] == IN_OFFSETS[K_S] + IN_SPLITS[K_S] and IN_SPLITS[K_S] + IN_SPLITS[V_S] == tn
    tab = pl.BlockSpec((tm, LANES), lambda i, j: (i, 0))
    n_i, n_j = SEQ // tm, D_IN // tn
    assert n_i * CAST_STEPS_PER_ROW == CAST_ROW_BLOCKS * CAST_COL_BLOCKS and CAST_STEPS_PER_ROW <= n_j

    def cast_spec(shape):
        rows, cols = shape[0] // CAST_ROW_BLOCKS, shape[1] // CAST_COL_BLOCKS
        assert rows % BF16_SUBLANES == 0 and cols % LANES == 0 and rows * CAST_ROW_BLOCKS == shape[0]

        def index(i, j):
            b = i * CAST_STEPS_PER_ROW + jnp.minimum(j, CAST_STEPS_PER_ROW - 1)
            return b // CAST_COL_BLOCKS, b % CAST_COL_BLOCKS
        return pl.BlockSpec((rows, cols), index)

    cast_specs = [cast_spec(cw.shape) for cw in cast_weights]
    return pl.pallas_call(
        _in_proj_kernel,
        grid=(SEQ // tm, D_IN // tn),
        in_specs=[pl.BlockSpec(memory_space=pl.ANY),
                  pl.BlockSpec((1, D_MODEL), lambda i, j: (0, 0)),
                  pl.BlockSpec((D_MODEL, tn), lambda i, j: (0, j)),
                  tab, tab, tab, tab, _resident((MXU_WIDTH, MXU_WIDTH)), *cast_specs],
        out_specs=[pl.BlockSpec((tm, tn), lambda i, j: (i, j)), *cast_specs],
        out_shape=[jax.ShapeDtypeStruct((SEQ, D_IN), BF16)]
                  + [jax.ShapeDtypeStruct(cw.shape, BF16) for cw in cast_weights],
        scratch_shapes=[pltpu.VMEM((tm, D_MODEL), BF16), pltpu.VMEM((tm, tn), F32),
                        pltpu.VMEM((tm, D_MODEL), F32), pltpu.SemaphoreType.DMA(())],
        compiler_params=_params("arbitrary", "arbitrary"),
        name="in_proj",
    )(x2d, g.reshape(1, D_MODEL), w, *tables, _rope_partner_matrix(), *cast_weights)


def _retention_kernel(logg_ref, q_ref, k_ref, v_ref, g_ref, o_ref, state_ref, dmask_ref):
    c_rows = RET_CHUNK
    idx = lax.broadcasted_iota(jnp.int32, (c_rows, 1), 0).astype(F32)

    @pl.when(pl.program_id(0) == 0)
    def _():
        state_ref[...] = jnp.zeros_like(state_ref)
        i = lax.broadcasted_iota(jnp.int32, (c_rows, c_rows), 0)
        j = lax.broadcasted_iota(jnp.int32, (c_rows, c_rows), 1)
        rel = (i - j).astype(F32)
        for h in range(RET_HEADS):
            dmask_ref[h] = jnp.where(rel >= 0, jnp.exp(logg_ref[h] * jnp.maximum(rel, 0.0)), 0.0)

    for h in range(RET_HEADS):
        lg = logg_ref[h]
        cols = slice(h * RET_DK, (h + 1) * RET_DK)
        q_dec = jnp.exp(lg * (idx + 1.0))
        k_dec = jnp.exp(lg * (c_rows - 1.0 - idx))
        chunk_decay = jnp.exp(jnp.full((1, RET_DV), lg * c_rows, F32))
        for c in range(RET_ROWS // c_rows):
            r = slice(c * c_rows, (c + 1) * c_rows)
            q, k, v = q_ref[r, cols], k_ref[r, cols], v_ref[r, cols]
            scores = _dot_nt(q, k) * dmask_ref[h]
            inner = _dot(scores.astype(BF16), v)
            state = state_ref[h]
            cross = _dot((q.astype(F32) * q_dec).astype(BF16), state.astype(BF16))
            state_ref[h] = state * chunk_decay + _dot_tn((k.astype(F32) * k_dec).astype(BF16), v)
            y = _rms(inner + cross)
            g = g_ref[r, cols].astype(F32)
            o_ref[r, cols] = (y * (g * _sigmoid(g))).astype(o_ref.dtype)


def _retention(proj):
    log_g = jnp.log(1.0 - jnp.power(2.0, -5.0 - jnp.arange(RET_HEADS, dtype=F32)))
    rows, width = RET_ROWS, RET_HEADS * RET_DK
    assert RET_DK == RET_DV
    all_heads = lambda split: pl.BlockSpec((rows, width), lambda n, c=IN_OFFSETS[split] // width: (n, c))
    assert all(IN_OFFSETS[s] % width == 0 for s in (Q_R, K_R, V_R, G_R))
    return pl.pallas_call(
        _retention_kernel,
        grid=(SEQ // rows,),
        in_specs=[pl.BlockSpec(memory_space=pltpu.SMEM),
                  all_heads(Q_R), all_heads(K_R), all_heads(V_R), all_heads(G_R)],
        out_specs=pl.BlockSpec((rows, width), lambda n: (n, 0)),
        out_shape=jax.ShapeDtypeStruct((SEQ, width), BF16),
        scratch_shapes=[pltpu.VMEM((RET_HEADS, RET_DK, RET_DV), F32),
                        pltpu.VMEM((RET_HEADS, RET_CHUNK, RET_CHUNK), F32)],
        compiler_params=_params("arbitrary"),
        name="retention",
    )(log_g, proj, proj, proj, proj)


def _swa_kernel(sinks_ref, q_ref, k_ref, v_ref, kp_ref, vp_ref, o_ref):
    w = WINDOW
    hd = SWA_HD
    group = SWA_HEADS // SWA_KV_HEADS
    step = pl.program_id(0)
    low = lax.broadcasted_iota(jnp.int32, (w, LANES), 1) < hd

    def dup_halves(x):
        xf = x.astype(F32)
        sw = pltpu.roll(xf, hd, 1)
        lo = lax.broadcasted_iota(jnp.int32, xf.shape, 1) < hd
        return jnp.where(lo, xf, sw).astype(BF16), jnp.where(lo, sw, xf).astype(BF16)

    def per_kv_head(cur_ref, prev_ref):
        allrows = jnp.concatenate([prev_ref[...], cur_ref[...]], axis=0)
        heads = []
        for c in range(allrows.shape[1] // LANES):
            heads.extend(dup_halves(allrows[:, c * LANES:(c + 1) * LANES]))
        return heads

    k2 = per_kv_head(k_ref, kp_ref)
    v2 = per_kv_head(v_ref, vp_ref)

    qi = lax.broadcasted_iota(jnp.int32, (w, 2 * w), 0)
    kj = lax.broadcasted_iota(jnp.int32, (w, 2 * w), 1)
    dist = qi + w - kj
    band = (dist >= 0) & (dist < WINDOW)
    zero = jnp.zeros((w, LANES), BF16)
    ones = jnp.ones((2 * w, LANES), BF16)
    zero_kv = jnp.zeros((2 * w, LANES), BF16)

    tile = lambda a: jnp.concatenate([a] * group, axis=0)
    bias_cur = tile(jnp.where(band[:, w:], 0.0, NEG))
    bias_prev = tile(jnp.where(band[:, :w], 0.0, NEG))
    sink_slot = lax.broadcasted_iota(jnp.int32, (group * w, LANES), 1) == 0
    first_row = lax.broadcasted_iota(jnp.int32, (2 * w, LANES), 0) == 0

    for kh in range(SWA_KV_HEADS):
        cols = [slice((kh * group // 2 + p) * LANES, (kh * group // 2 + p + 1) * LANES)
                for p in range(group // 2)]
        sink = jnp.concatenate([jnp.full((w, LANES), sinks_ref[kh * group + g], F32)
                                for g in range(group)], axis=0)
        bias_later = jnp.concatenate([jnp.where(sink_slot, sink, bias_prev), bias_cur], axis=-1)
        bias_first = jnp.concatenate(
            [jnp.where(sink_slot, sink, jnp.where(step > 0, bias_prev, NEG)), bias_cur], axis=-1)
        for t in range(SWA_ROWS // w):
            rows = slice(t * w, (t + 1) * w)
            bias = bias_first if t == 0 else bias_later
            k_w = jnp.where(first_row, zero_kv, k2[kh][t * w:(t + 2) * w])
            v_w = jnp.where(first_row, zero_kv, v2[kh][t * w:(t + 2) * w])
            parts = []
            for c in cols:
                qc = q_ref[rows, c]
                parts += [jnp.where(low, qc, zero), jnp.where(low, zero, qc)]
            s = _dot_nt(jnp.concatenate(parts, axis=0), k_w) + bias
            e = jnp.exp(s - jnp.max(s, axis=-1, keepdims=True))
            pv = _dot(e.astype(BF16), jnp.concatenate([v_w, ones], axis=-1))
            o = pv[:, :LANES] / pv[:, LANES:]
            for p, c in enumerate(cols):
                o_ref[rows, c] = jnp.where(low, o[2 * p * w:(2 * p + 1) * w],
                                           o[(2 * p + 1) * w:(2 * p + 2) * w]).astype(o_ref.dtype)


def _swa(proj, sinks):
    w, rows = WINDOW, SWA_ROWS
    assert 2 * SWA_HD == LANES and WINDOW == LANES and (SWA_HEADS // SWA_KV_HEADS) % 2 == 0
    nq, nkv = SWA_HEADS * SWA_HD, SWA_KV_HEADS * SWA_HD
    per = rows // w
    cur = lambda split: pl.BlockSpec((rows, nkv), lambda i, c=IN_OFFSETS[split] // nkv: (i, c))
    prev = lambda split: pl.BlockSpec(
        (w, nkv), lambda i, c=IN_OFFSETS[split] // nkv: (jnp.maximum(i * per - 1, 0), c))
    return pl.pallas_call(
        _swa_kernel,
        grid=(SEQ // rows,),
        in_specs=[pl.BlockSpec(memory_space=pltpu.SMEM),
                  pl.BlockSpec((rows, nq), lambda i: (i, IN_OFFSETS[Q_S] // nq)),
                  cur(K_S), cur(V_S), prev(K_S), prev(V_S)],
        out_specs=pl.BlockSpec((rows, nq), lambda i: (i, 0)),
        out_shape=jax.ShapeDtypeStruct((SEQ, nq), BF16),
        compiler_params=_params("parallel"),
        name="swa",
    )(sinks, proj, proj, proj, proj, proj)


def _mem_kv_kernel(mem_ref, g_ref, w_ref, o_ref):
    memn = (_rms(mem_ref[...]) * g_ref[...]).astype(BF16)
    o_ref[...] = _dot(memn, w_ref[...]).astype(o_ref.dtype)


def _mem_kv(mem2d, g, w):
    n_out = 2 * X_HEADS * X_HD
    return pl.pallas_call(
        _mem_kv_kernel,
        out_shape=jax.ShapeDtypeStruct((N_MEM, n_out), BF16),
        compiler_params=pltpu.CompilerParams(vmem_limit_bytes=VMEM_LIMIT),
        name="mem_kv",
    )(mem2d, g.reshape(1, D_MODEL), w)


def _merge_kernel(x_ref, yr_ref, ys_ref, *refs):
    n_gate = D_MODEL // GATE_BLOCK
    gr_refs, gs_refs = refs[:n_gate], refs[n_gate:2 * n_gate]
    (kv_ref, wur_ref, wus_ref, wo_ref, gx_ref, wxq_ref, wxo_ref, gf_ref, h_ref, n_ref) = refs[2 * n_gate:]
    gate_r = jnp.concatenate([r[...] for r in gr_refs], axis=-1).astype(F32)
    gate_s = jnp.concatenate([r[...] for r in gs_refs], axis=-1).astype(F32)
    up_r = _dot(yr_ref[...], wur_ref[...])
    up_s = _dot(ys_ref[...], wus_ref[...])
    merged = _sigmoid(gate_r) * up_r + _sigmoid(gate_s) * up_s
    h1 = x_ref[...] + _dot(merged.astype(BF16), wo_ref[...])

    nx = (_rms(h1) * gx_ref[...]).astype(BF16)
    qx = _dot(nx, wxq_ref[...]).astype(BF16)
    kv = kv_ref[...]
    outs = []
    for h in range(X_HEADS):
        q_h = qx[:, h * X_HD:(h + 1) * X_HD]
        k_h = kv[:, h * X_HD:(h + 1) * X_HD]
        v_h = kv[:, (X_HEADS + h) * X_HD:(X_HEADS + h + 1) * X_HD]
        s = _dot_nt(q_h, k_h) * (X_HD ** -0.5)
        e = jnp.exp(s - jnp.max(s, axis=-1, keepdims=True))
        p = e / jnp.sum(e, axis=-1, keepdims=True)
        outs.append(_dot(p.astype(BF16), v_h))
    att = jnp.concatenate(outs, axis=-1).astype(BF16)
    h2 = h1 + _dot(att, wxo_ref[...])
    h_ref[...] = h2
    n_ref[...] = (_rms(h2) * gf_ref[...]).astype(BF16)


def _merge(x2d, yr, ys, proj, kv, w_up_ret, w_up_swa, w_o, g_x, w_xq, w_xo, g_ffn):
    tm = MERGE_TM
    rows = lambda width, col=0: pl.BlockSpec((tm, width), lambda i, c=col: (i, c))
    n_gate = D_MODEL // GATE_BLOCK
    gate = lambda split: [rows(GATE_BLOCK, IN_OFFSETS[split] // GATE_BLOCK + b) for b in range(n_gate)]
    assert IN_OFFSETS[GATE_R] % GATE_BLOCK == 0 and IN_OFFSETS[GATE_S] % GATE_BLOCK == 0
    n_x = X_HEADS * X_HD
    return pl.pallas_call(
        _merge_kernel,
        grid=(SEQ // tm,),
        in_specs=[rows(D_MODEL), rows(RET_HEADS * RET_DV), rows(SWA_HEADS * SWA_HD),
                  *gate(GATE_R), *gate(GATE_S),
                  _resident((N_MEM, 2 * n_x)),
                  _resident((RET_HEADS * RET_DV, D_MODEL)), _resident((SWA_HEADS * SWA_HD, D_MODEL)),
                  _resident((D_MODEL, D_MODEL)), _resident((1, D_MODEL)),
                  _resident((D_MODEL, n_x)), _resident((n_x, D_MODEL)), _resident((1, D_MODEL))],
        out_specs=[rows(D_MODEL), rows(D_MODEL)],
        out_shape=[jax.ShapeDtypeStruct((SEQ, D_MODEL), F32),
                   jax.ShapeDtypeStruct((SEQ, D_MODEL), BF16)],
        compiler_params=_params("parallel"),
        name="merge_xattn",
    )(x2d, yr, ys, *([proj] * (2 * n_gate)), kv, w_up_ret, w_up_swa, w_o, g_x.reshape(1, D_MODEL),
      w_xq, w_xo, g_ffn.reshape(1, D_MODEL))


def _ffn_kernel(n_ref, h_hbm, wg_ref, wu_ref, wd_ref, gf_ref, o_ref, h_ref, h_sem):
    i, f = pl.program_id(0), pl.program_id(1)
    rows = pl.ds(pl.multiple_of(i * FFN_TM, FFN_TM), FFN_TM)
    residual_copy = pltpu.make_async_copy(h_hbm.at[rows, :], h_ref, h_sem)

    @pl.when(f == 0)
    def _():
        residual_copy.start()
        for c in range(FFN_TM // FFN_SLAB):
            o_ref[c * FFN_SLAB:(c + 1) * FFN_SLAB, :] = jnp.zeros((FFN_SLAB, D_MODEL), F32)

    n2 = n_ref[...]
    ts = []
    for c in range(FFN_TF // FFN_SLAB):
        cols = slice(c * FFN_SLAB, (c + 1) * FFN_SLAB)
        a = _dot(n2, wg_ref[:, cols])
        b = _dot(n2, wu_ref[:, cols])
        ts.append((a * _sigmoid(a) * b).astype(BF16))
    t = jnp.concatenate(ts, axis=-1)
    for c in range(D_MODEL // FFN_TF):
        cols = slice(c * FFN_TF, (c + 1) * FFN_TF)
        o_ref[:, cols] += _dot(t, wd_ref[:, cols])

    @pl.when(f == pl.num_programs(1) - 1)
    def _():
        residual_copy.wait()
        for c in range(FFN_TM // FFN_SLAB):
            r = slice(c * FFN_SLAB, (c + 1) * FFN_SLAB)
            o_ref[r, :] = _rms(h_ref[r, :] + o_ref[r, :]) * gf_ref[...]


def _ffn(n2, h2, w_gate, w_up, w_down, g_final):
    tm, tf = FFN_TM, FFN_TF
    return pl.pallas_call(
        _ffn_kernel,
        grid=(SEQ // tm, D_FF // tf),
        in_specs=[pl.BlockSpec((tm, D_MODEL), lambda i, f: (i, 0)),
                  pl.BlockSpec(memory_space=pl.ANY),
                  pl.BlockSpec((D_MODEL, tf), lambda i, f: (0, f)),
                  pl.BlockSpec((D_MODEL, tf), lambda i, f: (0, f)),
                  pl.BlockSpec((tf, D_MODEL), lambda i, f: (f, 0)),
                  pl.BlockSpec((1, D_MODEL), lambda i, f: (0, 0))],
        out_specs=pl.BlockSpec((tm, D_MODEL), lambda i, f: (i, 0)),
        out_shape=jax.ShapeDtypeStruct((SEQ, D_MODEL), F32),
        scratch_shapes=[pltpu.VMEM((tm, D_MODEL), F32), pltpu.SemaphoreType.DMA(())],
        compiler_params=_params("arbitrary", "arbitrary"),
        name="ffn",
    )(n2, h2, w_gate, w_up, w_down, g_final.reshape(1, D_MODEL))


def kernel(x, mem, positions, g_mix, w_in, w_up_ret, w_up_swa, sinks, w_o, g_x, g_mem,
           w_xq, w_xkv, w_xo, g_ffn, w_ffn_gate, w_ffn_up, w_ffn_down, g_final):
    assert x.shape == (1, SEQ, D_MODEL) and mem.shape == (1, N_MEM, D_MODEL)
    assert w_in.shape == (1, D_MODEL, D_IN)
    x2d = x.reshape(SEQ, D_MODEL)
    tables = _rope_tables(positions)
    later_weights = (w_ffn_gate, w_ffn_up, w_ffn_down, w_up_ret, w_up_swa, w_o)
    proj, *bf16_weights = _in_proj(x2d, g_mix[0], w_in[0], tables, tuple(w[0] for w in later_weights))
    w_gate, w_up, w_down, w_ur, w_us, w_out = bf16_weights
    yr = _retention(proj)
    ys = _swa(proj, sinks[0])
    kv = _mem_kv(mem.reshape(N_MEM, D_MODEL), g_mem[0], w_xkv[0].astype(BF16))
    h2, n2 = _merge(x2d, yr, ys, proj, kv, w_ur, w_us, w_out, g_x[0],
                    w_xq[0].astype(BF16), w_xo[0].astype(BF16), g_ffn[0])
    out = _ffn(n2, h2, w_gate, w_up, w_down, g_final)
    return out.reshape(1, SEQ, D_MODEL)
```

```python
import jax
import jax.numpy as jnp
import numpy as np
from jax import lax
from jax.experimental import pallas as pl
from jax.experimental.pallas import tpu as pltpu

F32 = jnp.float32
BF16 = jnp.bfloat16

D_MODEL = 2048
SEQ = 16384
N_MEM = 256
EPS = 1e-6
RET_HEADS = 4
RET_DK = 256
RET_DV = 256
RET_THETA = 10000.0
SWA_HEADS = 16
SWA_KV_HEADS = 4
SWA_HD = 64
WINDOW = 128
ROPE_THETA = 500000.0
ROPE_DIM = SWA_HD // 4
X_HEADS = 4
X_HD = 128
D_FF = -(-8 * D_MODEL // (3 * 256)) * 256
IN_SPLITS = (RET_HEADS * RET_DK, RET_HEADS * RET_DK, RET_HEADS * RET_DV, RET_HEADS * RET_DV,
             SWA_HEADS * SWA_HD, SWA_KV_HEADS * SWA_HD, SWA_KV_HEADS * SWA_HD, D_MODEL, D_MODEL)
D_IN = sum(IN_SPLITS)
IN_OFFSETS = tuple(int(o) for o in np.cumsum((0,) + IN_SPLITS[:-1]))
Q_R, K_R, V_R, G_R, Q_S, K_S, V_S, GATE_R, GATE_S = range(9)
GATE_PAD = -IN_OFFSETS[GATE_R] % D_MODEL
PROJ_WIDTH = D_IN + GATE_PAD
PROJ_OFFSETS = tuple(o + (GATE_PAD if s >= GATE_R else 0) for s, o in enumerate(IN_OFFSETS))
NEG = -1e30

LANES = 128
BF16_SUBLANES = 16
MXU_WIDTH = 256
VMEM_LIMIT = 58 * 1024 * 1024

RET_CHUNK = 256
RET_ROWS = 1024
ROPE_ROWS = 2048
PROJ_TM, PROJ_TN = 2048, 512
NORM_SLAB = 256
ROPE_SLAB = 512
CAST_STEPS_PER_ROW = 16
CAST_ROW_BLOCKS, CAST_COL_BLOCKS = 32, 4
SWA_ROWS = 512
MERGE_TM = 256
FFN_TM, FFN_TF = 1024, 512
FFN_SLAB = MXU_WIDTH


def _sigmoid(x):
    return 1.0 / (1.0 + jnp.exp(-x))


def _rms(x):
    return x * lax.rsqrt(jnp.mean(x * x, axis=-1, keepdims=True) + EPS)


def _dot(a, b):
    return jnp.dot(a, b, preferred_element_type=F32)


def _dot_nt(a, b):
    return lax.dot_general(a, b, (((1,), (1,)), ((), ())), preferred_element_type=F32)


def _dot_tn(a, b):
    return lax.dot_general(a, b, (((0,), (0,)), ((), ())), preferred_element_type=F32)


def _params(*sem):
    return pltpu.CompilerParams(dimension_semantics=sem, vmem_limit_bytes=VMEM_LIMIT)


def _resident(shape):
    nd = len(shape)
    return pl.BlockSpec(shape, lambda *_: (0,) * nd, pipeline_mode=pl.Buffered(1))


def _rope_tables_kernel(pos_ref, invr_ref, invs_ref, sign_ref, rc_ref, rs_ref, sc_ref, ss_ref):
    pos = pos_ref[...].astype(F32)
    ang_r = pos * invr_ref[...]
    rc_ref[...] = jnp.cos(ang_r)
    rs_ref[...] = jnp.sin(ang_r)
    ang_s = pos * invs_ref[...]
    sc_ref[...] = jnp.cos(ang_s)
    ss_ref[...] = jnp.sin(ang_s) * sign_ref[...]


def _rope_tables(positions):
    pos = positions.reshape(SEQ, 1)
    half_r = RET_DK // 2
    inv_r = 1.0 / (RET_THETA ** (jnp.arange(half_r, dtype=F32) / half_r))
    half_s = ROPE_DIM // 2
    inv_s = 1.0 / (ROPE_THETA ** (jnp.arange(half_s, dtype=F32) / half_s))
    d = np.arange(LANES) % SWA_HD
    inv_s_lanes = jnp.where(d < ROPE_DIM, inv_s[d % half_s], 0.0).astype(F32)
    sign = jnp.asarray(np.where(d < half_s, -1.0, np.where(d < ROPE_DIM, 1.0, 0.0)).astype(np.float32))
    row = lambda v: v.reshape(1, LANES)
    tab = jax.ShapeDtypeStruct((SEQ, LANES), F32)
    vec = pl.BlockSpec((1, LANES), lambda i: (0, 0))
    blk = pl.BlockSpec((ROPE_ROWS, LANES), lambda i: (i, 0))
    return pl.pallas_call(
        _rope_tables_kernel,
        grid=(SEQ // ROPE_ROWS,),
        in_specs=[pl.BlockSpec((ROPE_ROWS, 1), lambda i: (i, 0)), vec, vec, vec],
        out_specs=[blk] * 4,
        out_shape=[tab] * 4,
        compiler_params=_params("parallel"),
        name="rope_tables",
    )(pos, row(inv_r), row(inv_s_lanes), row(sign))


def _in_proj_kernel(x_hbm, g_ref, w_ref, rc_ref, rs_ref, sc_ref, ss_ref, perm_ref, *refs):
    tm, tn = PROJ_TM, PROJ_TN
    i, j = pl.program_id(0), pl.program_id(1)
    blk = lambda split: IN_OFFSETS[split] // tn
    *io_refs, n_ref, acc_ref, x_buf, x_sem = refs
    n_cast = len(io_refs) // 2
    cast_in, o_ref, cast_out = io_refs[:n_cast], io_refs[n_cast], io_refs[n_cast + 1:]
    for src, dst in zip(cast_in, cast_out):
        dst[...] = src[...].astype(BF16)

    def x_copy(block):
        rows = pl.ds(pl.multiple_of(block * tm, tm), tm)
        return pltpu.make_async_copy(x_hbm.at[rows, :], x_buf, x_sem)

    @pl.when(j == 0)
    def _():
        @pl.when(i == 0)
        def _():
            x_copy(0).start()

        x_copy(i).wait()
        for c in range(tm // NORM_SLAB):
            r = slice(c * NORM_SLAB, (c + 1) * NORM_SLAB)
            n_ref[r, :] = (_rms(x_buf[r, :]) * g_ref[...]).astype(BF16)

    @pl.when((j == 1) & (i + 1 < pl.num_programs(0)))
    def _():
        x_copy(i + 1).start()

    acc_ref[...] = _dot(n_ref[...], w_ref[...].astype(BF16))
    o_ref[...] = acc_ref[...].astype(BF16)
    is_ret = (j >= blk(Q_R)) & (j < blk(V_R))
    is_qs = (j >= blk(Q_S)) & (j < blk(K_S))
    is_kvs = j == blk(K_S)

    def swa_rope(pair, scale):
        cols = slice(pair * MXU_WIDTH, (pair + 1) * MXU_WIDTH)
        for c in range(tm // ROPE_SLAB):
            r = slice(c * ROPE_SLAB, (c + 1) * ROPE_SLAB)
            xt = acc_ref[r, cols]
            hi = xt.astype(BF16)
            lo = (xt - hi.astype(F32)).astype(BF16)
            partner = _dot(hi, perm_ref[...]) + _dot(lo, perm_ref[...])
            cos = jnp.concatenate([sc_ref[r, :]] * (MXU_WIDTH // LANES), axis=-1)
            sin = jnp.concatenate([ss_ref[r, :]] * (MXU_WIDTH // LANES), axis=-1)
            o_ref[r, cols] = ((xt * cos + partner * sin) * scale).astype(BF16)

    @pl.when(is_ret)
    def _():
        scale = jnp.where(j >= blk(K_R), RET_DK ** -0.5, 1.0).astype(F32)
        cos, sin = rc_ref[...] * scale, rs_ref[...] * scale
        half = RET_DK // 2
        for h in range(tn // RET_DK):
            a1 = acc_ref[:, h * RET_DK:h * RET_DK + half]
            a2 = acc_ref[:, h * RET_DK + half:(h + 1) * RET_DK]
            o_ref[:, h * RET_DK:h * RET_DK + half] = (a1 * cos - a2 * sin).astype(BF16)
            o_ref[:, h * RET_DK + half:(h + 1) * RET_DK] = (a2 * cos + a1 * sin).astype(BF16)

    @pl.when(is_qs)
    def _():
        for pair in range(tn // MXU_WIDTH):
            swa_rope(pair, SWA_HD ** -0.5)

    @pl.when(is_kvs)
    def _():
        for pair in range(IN_SPLITS[K_S] // MXU_WIDTH):
            swa_rope(pair, 1.0)


def _rope_partner_matrix():
    half = ROPE_DIM // 2
    p = np.zeros((MXU_WIDTH, MXU_WIDTH), np.float32)
    for lane in range(MXU_WIDTH):
        d = lane % SWA_HD
        if d < half:
            p[lane + half, lane] = 1.0
        elif d < ROPE_DIM:
            p[lane - half, lane] = 1.0
    return jnp.asarray(p, BF16)


def _in_proj(x2d, g, w, tables, cast_weights):
    tm, tn = PROJ_TM, PROJ_TN
    assert all(IN_OFFSETS[s] % tn == 0 for s in (Q_R, K_R, V_R, Q_S, K_S))
    assert IN_OFFSETS[V_S] == IN_OFFSETS[K_S] + IN_SPLITS[K_S] and IN_SPLITS[K_S] + IN_SPLITS[V_S] == tn
    tab = pl.BlockSpec((tm, LANES), lambda i, j: (i, 0))
    n_i, n_j = SEQ // tm, D_IN // tn
    assert n_i * CAST_STEPS_PER_ROW == CAST_ROW_BLOCKS * CAST_COL_BLOCKS and CAST_STEPS_PER_ROW <= n_j

    def cast_spec(shape):
        rows, cols = shape[0] // CAST_ROW_BLOCKS, shape[1] // CAST_COL_BLOCKS
        assert rows % BF16_SUBLANES == 0 and cols % LANES == 0 and rows * CAST_ROW_BLOCKS == shape[0]

        def index(i, j):
            b = i * CAST_STEPS_PER_ROW + jnp.minimum(j, CAST_STEPS_PER_ROW - 1)
            return b // CAST_COL_BLOCKS, b % CAST_COL_BLOCKS
        return pl.BlockSpec((rows, cols), index)

    cast_specs = [cast_spec(cw.shape) for cw in cast_weights]
    assert IN_OFFSETS[GATE_R] % tn == 0 and GATE_PAD % tn == 0
    gate_block, pad_blocks = IN_OFFSETS[GATE_R] // tn, GATE_PAD // tn
    return pl.pallas_call(
        _in_proj_kernel,
        grid=(SEQ // tm, D_IN // tn),
        in_specs=[pl.BlockSpec(memory_space=pl.ANY),
                  pl.BlockSpec((1, D_MODEL), lambda i, j: (0, 0)),
                  pl.BlockSpec((D_MODEL, tn), lambda i, j: (0, j)),
                  tab, tab, tab, tab, _resident((MXU_WIDTH, MXU_WIDTH)), *cast_specs],
        out_specs=[pl.BlockSpec((tm, tn), lambda i, j: (i, j + jnp.where(j >= gate_block, pad_blocks, 0))),
                   *cast_specs],
        out_shape=[jax.ShapeDtypeStruct((SEQ, PROJ_WIDTH), BF16)]
                  + [jax.ShapeDtypeStruct(cw.shape, BF16) for cw in cast_weights],
        scratch_shapes=[pltpu.VMEM((tm, D_MODEL), BF16), pltpu.VMEM((tm, tn), F32),
                        pltpu.VMEM((tm, D_MODEL), F32), pltpu.SemaphoreType.DMA(())],
        compiler_params=_params("arbitrary", "arbitrary"),
        name="in_proj",
    )(x2d, g.reshape(1, D_MODEL), w, *tables, _rope_partner_matrix(), *cast_weights)


def _retention_kernel(logg_ref, q_ref, k_ref, v_ref, g_ref, o_ref, state_ref, dmask_ref):
    c_rows = RET_CHUNK
    idx = lax.broadcasted_iota(jnp.int32, (c_rows, 1), 0).astype(F32)

    @pl.when(pl.program_id(0) == 0)
    def _():
        state_ref[...] = jnp.zeros_like(state_ref)
        i = lax.broadcasted_iota(jnp.int32, (c_rows, c_rows), 0)
        j = lax.broadcasted_iota(jnp.int32, (c_rows, c_rows), 1)
        rel = (i - j).astype(F32)
        for h in range(RET_HEADS):
            dmask_ref[h] = jnp.where(rel >= 0, jnp.exp(logg_ref[h] * jnp.maximum(rel, 0.0)), 0.0)

    for h in range(RET_HEADS):
        lg = logg_ref[h]
        cols = slice(h * RET_DK, (h + 1) * RET_DK)
        q_dec = jnp.exp(lg * (idx + 1.0))
        k_dec = jnp.exp(lg * (c_rows - 1.0 - idx))
        chunk_decay = jnp.exp(jnp.full((1, RET_DV), lg * c_rows, F32))
        for c in range(RET_ROWS // c_rows):
            r = slice(c * c_rows, (c + 1) * c_rows)
            q, k, v = q_ref[r, cols], k_ref[r, cols], v_ref[r, cols]
            scores = _dot_nt(q, k) * dmask_ref[h]
            inner = _dot(scores.astype(BF16), v)
            state = state_ref[h]
            cross = _dot((q.astype(F32) * q_dec).astype(BF16), state.astype(BF16))
            state_ref[h] = state * chunk_decay + _dot_tn((k.astype(F32) * k_dec).astype(BF16), v)
            y = _rms(inner + cross)
            g = g_ref[r, cols].astype(F32)
            o_ref[r, cols] = (y * (g * _sigmoid(g))).astype(o_ref.dtype)


def _retention(proj):
    log_g = jnp.log(1.0 - jnp.power(2.0, -5.0 - jnp.arange(RET_HEADS, dtype=F32)))
    rows, width = RET_ROWS, RET_HEADS * RET_DK
    assert RET_DK == RET_DV
    all_heads = lambda split: pl.BlockSpec((rows, width), lambda n, c=IN_OFFSETS[split] // width: (n, c))
    assert all(IN_OFFSETS[s] % width == 0 for s in (Q_R, K_R, V_R, G_R))
    return pl.pallas_call(
        _retention_kernel,
        grid=(SEQ // rows,),
        in_specs=[pl.BlockSpec(memory_space=pltpu.SMEM),
                  all_heads(Q_R), all_heads(K_R), all_heads(V_R), all_heads(G_R)],
        out_specs=pl.BlockSpec((rows, width), lambda n: (n, 0)),
        out_shape=jax.ShapeDtypeStruct((SEQ, width), BF16),
        scratch_shapes=[pltpu.VMEM((RET_HEADS, RET_DK, RET_DV), F32),
                        pltpu.VMEM((RET_HEADS, RET_CHUNK, RET_CHUNK), F32)],
        compiler_params=_params("arbitrary"),
        name="retention",
    )(log_g, proj, proj, proj, proj)


def _swa_kernel(sinks_ref, q_ref, k_ref, v_ref, kp_ref, vp_ref, o_ref):
    w = WINDOW
    hd = SWA_HD
    group = SWA_HEADS // SWA_KV_HEADS
    step = pl.program_id(0)
    low = lax.broadcasted_iota(jnp.int32, (w, LANES), 1) < hd

    def dup_halves(x):
        xf = x.astype(F32)
        sw = pltpu.roll(xf, hd, 1)
        lo = lax.broadcasted_iota(jnp.int32, xf.shape, 1) < hd
        return jnp.where(lo, xf, sw).astype(BF16), jnp.where(lo, sw, xf).astype(BF16)

    def per_kv_head(cur_ref, prev_ref):
        allrows = jnp.concatenate([prev_ref[...], cur_ref[...]], axis=0)
        heads = []
        for c in range(allrows.shape[1] // LANES):
            heads.extend(dup_halves(allrows[:, c * LANES:(c + 1) * LANES]))
        return heads

    k2 = per_kv_head(k_ref, kp_ref)
    v2 = per_kv_head(v_ref, vp_ref)

    qi = lax.broadcasted_iota(jnp.int32, (w, 2 * w), 0)
    kj = lax.broadcasted_iota(jnp.int32, (w, 2 * w), 1)
    dist = qi + w - kj
    band = (dist >= 0) & (dist < WINDOW)
    zero = jnp.zeros((w, LANES), BF16)
    ones = jnp.ones((2 * w, LANES), BF16)
    zero_kv = jnp.zeros((2 * w, LANES), BF16)

    tile = lambda a: jnp.concatenate([a] * group, axis=0)
    bias_cur = tile(jnp.where(band[:, w:], 0.0, NEG))
    bias_prev = tile(jnp.where(band[:, :w], 0.0, NEG))
    sink_slot = lax.broadcasted_iota(jnp.int32, (group * w, LANES), 1) == 0
    first_row = lax.broadcasted_iota(jnp.int32, (2 * w, LANES), 0) == 0

    for kh in range(SWA_KV_HEADS):
        cols = [slice((kh * group // 2 + p) * LANES, (kh * group // 2 + p + 1) * LANES)
                for p in range(group // 2)]
        sink = jnp.concatenate([jnp.full((w, LANES), sinks_ref[kh * group + g], F32)
                                for g in range(group)], axis=0)
        bias_later = jnp.concatenate([jnp.where(sink_slot, sink, bias_prev), bias_cur], axis=-1)
        bias_first = jnp.concatenate(
            [jnp.where(sink_slot, sink, jnp.where(step > 0, bias_prev, NEG)), bias_cur], axis=-1)
        for t in range(SWA_ROWS // w):
            rows = slice(t * w, (t + 1) * w)
            bias = bias_first if t == 0 else bias_later
            k_w = jnp.where(first_row, zero_kv, k2[kh][t * w:(t + 2) * w])
            v_w = jnp.where(first_row, zero_kv, v2[kh][t * w:(t + 2) * w])
            parts = []
            for c in cols:
                qc = q_ref[rows, c]
                parts += [jnp.where(low, qc, zero), jnp.where(low, zero, qc)]
            s = _dot_nt(jnp.concatenate(parts, axis=0), k_w) + bias
            e = jnp.exp(s - jnp.max(s, axis=-1, keepdims=True))
            pv = _dot(e.astype(BF16), jnp.concatenate([v_w, ones], axis=-1))
            o = pv[:, :LANES] / pv[:, LANES:]
            for p, c in enumerate(cols):
                o_ref[rows, c] = jnp.where(low, o[2 * p * w:(2 * p + 1) * w],
                                           o[(2 * p + 1) * w:(2 * p + 2) * w]).astype(o_ref.dtype)


def _swa(proj, sinks):
    w, rows = WINDOW, SWA_ROWS
    assert 2 * SWA_HD == LANES and WINDOW == LANES and (SWA_HEADS // SWA_KV_HEADS) % 2 == 0
    nq, nkv = SWA_HEADS * SWA_HD, SWA_KV_HEADS * SWA_HD
    per = rows // w
    cur = lambda split: pl.BlockSpec((rows, nkv), lambda i, c=IN_OFFSETS[split] // nkv: (i, c))
    prev = lambda split: pl.BlockSpec(
        (w, nkv), lambda i, c=IN_OFFSETS[split] // nkv: (jnp.maximum(i * per - 1, 0), c))
    return pl.pallas_call(
        _swa_kernel,
        grid=(SEQ // rows,),
        in_specs=[pl.BlockSpec(memory_space=pltpu.SMEM),
                  pl.BlockSpec((rows, nq), lambda i: (i, IN_OFFSETS[Q_S] // nq)),
                  cur(K_S), cur(V_S), prev(K_S), prev(V_S)],
        out_specs=pl.BlockSpec((rows, nq), lambda i: (i, 0)),
        out_shape=jax.ShapeDtypeStruct((SEQ, nq), BF16),
        compiler_params=_params("parallel"),
        name="swa",
    )(sinks, proj, proj, proj, proj, proj)


def _mem_kv_kernel(mem_ref, g_ref, w_ref, o_ref):
    memn = (_rms(mem_ref[...]) * g_ref[...]).astype(BF16)
    o_ref[...] = _dot(memn, w_ref[...]).astype(o_ref.dtype)


def _mem_kv(mem2d, g, w):
    n_out = 2 * X_HEADS * X_HD
    return pl.pallas_call(
        _mem_kv_kernel,
        out_shape=jax.ShapeDtypeStruct((N_MEM, n_out), BF16),
        compiler_params=pltpu.CompilerParams(vmem_limit_bytes=VMEM_LIMIT),
        name="mem_kv",
    )(mem2d, g.reshape(1, D_MODEL), w)


def _merge_kernel(x_ref, yr_ref, ys_ref, gr_ref, gs_ref, kv_ref,
                  wur_ref, wus_ref, wo_ref, gx_ref, wxq_ref, wxo_ref, gf_ref, h_ref, n_ref):
    up_r = _dot(yr_ref[...], wur_ref[...])
    up_s = _dot(ys_ref[...], wus_ref[...])
    merged = _sigmoid(gr_ref[...].astype(F32)) * up_r + _sigmoid(gs_ref[...].astype(F32)) * up_s
    h1 = x_ref[...] + _dot(merged.astype(BF16), wo_ref[...])

    nx = (_rms(h1) * gx_ref[...]).astype(BF16)
    qx = _dot(nx, wxq_ref[...]).astype(BF16)
    kv = kv_ref[...]
    outs = []
    for h in range(X_HEADS):
        q_h = qx[:, h * X_HD:(h + 1) * X_HD]
        k_h = kv[:, h * X_HD:(h + 1) * X_HD]
        v_h = kv[:, (X_HEADS + h) * X_HD:(X_HEADS + h + 1) * X_HD]
        s = _dot_nt(q_h, k_h) * (X_HD ** -0.5)
        e = jnp.exp(s - jnp.max(s, axis=-1, keepdims=True))
        p = e / jnp.sum(e, axis=-1, keepdims=True)
        outs.append(_dot(p.astype(BF16), v_h))
    att = jnp.concatenate(outs, axis=-1).astype(BF16)
    h2 = h1 + _dot(att, wxo_ref[...])
    h_ref[...] = h2
    n_ref[...] = (_rms(h2) * gf_ref[...]).astype(BF16)


def _merge(x2d, yr, ys, proj, kv, w_up_ret, w_up_swa, w_o, g_x, w_xq, w_xo, g_ffn):
    tm = MERGE_TM
    rows = lambda width, col=0: pl.BlockSpec((tm, width), lambda i, c=col: (i, c))
    assert PROJ_OFFSETS[GATE_R] % D_MODEL == 0 and PROJ_OFFSETS[GATE_S] % D_MODEL == 0
    n_x = X_HEADS * X_HD
    return pl.pallas_call(
        _merge_kernel,
        grid=(SEQ // tm,),
        in_specs=[rows(D_MODEL), rows(RET_HEADS * RET_DV), rows(SWA_HEADS * SWA_HD),
                  rows(D_MODEL, PROJ_OFFSETS[GATE_R] // D_MODEL), rows(D_MODEL, PROJ_OFFSETS[GATE_S] // D_MODEL),
                  _resident((N_MEM, 2 * n_x)),
                  _resident((RET_HEADS * RET_DV, D_MODEL)), _resident((SWA_HEADS * SWA_HD, D_MODEL)),
                  _resident((D_MODEL, D_MODEL)), _resident((1, D_MODEL)),
                  _resident((D_MODEL, n_x)), _resident((n_x, D_MODEL)), _resident((1, D_MODEL))],
        out_specs=[rows(D_MODEL), rows(D_MODEL)],
        out_shape=[jax.ShapeDtypeStruct((SEQ, D_MODEL), F32),
                   jax.ShapeDtypeStruct((SEQ, D_MODEL), BF16)],
        compiler_params=_params("parallel"),
        name="merge_xattn",
    )(x2d, yr, ys, proj, proj, kv, w_up_ret, w_up_swa, w_o, g_x.reshape(1, D_MODEL),
      w_xq, w_xo, g_ffn.reshape(1, D_MODEL))


def _ffn_kernel(n_ref, h_hbm, wg_ref, wu_ref, wd_ref, gf_ref, o_ref, h_ref, h_sem):
    i, f = pl.program_id(0), pl.program_id(1)
    rows = pl.ds(pl.multiple_of(i * FFN_TM, FFN_TM), FFN_TM)
    residual_copy = pltpu.make_async_copy(h_hbm.at[rows, :], h_ref, h_sem)

    @pl.when(f == 0)
    def _():
        residual_copy.start()
        for c in range(FFN_TM // FFN_SLAB):
            o_ref[c * FFN_SLAB:(c + 1) * FFN_SLAB, :] = jnp.zeros((FFN_SLAB, D_MODEL), F32)

    n2 = n_ref[...]
    ts = []
    for c in range(FFN_TF // FFN_SLAB):
        cols = slice(c * FFN_SLAB, (c + 1) * FFN_SLAB)
        a = _dot(n2, wg_ref[:, cols])
        b = _dot(n2, wu_ref[:, cols])
        ts.append((a * _sigmoid(a) * b).astype(BF16))
    t = jnp.concatenate(ts, axis=-1)
    for c in range(D_MODEL // FFN_TF):
        cols = slice(c * FFN_TF, (c + 1) * FFN_TF)
        o_ref[:, cols] += _dot(t, wd_ref[:, cols])

    @pl.when(f == pl.num_programs(1) - 1)
    def _():
        residual_copy.wait()
        for c in range(FFN_TM // FFN_SLAB):
            r = slice(c * FFN_SLAB, (c + 1) * FFN_SLAB)
            o_ref[r, :] = _rms(h_ref[r, :] + o_ref[r, :]) * gf_ref[...]


def _ffn(n2, h2, w_gate, w_up, w_down, g_final):
    tm, tf = FFN_TM, FFN_TF
    return pl.pallas_call(
        _ffn_kernel,
        grid=(SEQ // tm, D_FF // tf),
        in_specs=[pl.BlockSpec((tm, D_MODEL), lambda i, f: (i, 0)),
                  pl.BlockSpec(memory_space=pl.ANY),
                  pl.BlockSpec((D_MODEL, tf), lambda i, f: (0, f)),
                  pl.BlockSpec((D_MODEL, tf), lambda i, f: (0, f)),
                  pl.BlockSpec((tf, D_MODEL), lambda i, f: (f, 0)),
                  pl.BlockSpec((1, D_MODEL), lambda i, f: (0, 0))],
        out_specs=pl.BlockSpec((tm, D_MODEL), lambda i, f: (i, 0)),
        out_shape=jax.ShapeDtypeStruct((SEQ, D_MODEL), F32),
        scratch_shapes=[pltpu.VMEM((tm, D_MODEL), F32), pltpu.SemaphoreType.DMA(())],
        compiler_params=_params("arbitrary", "arbitrary"),
        name="ffn",
    )(n2, h2, w_gate, w_up, w_down, g_final.reshape(1, D_MODEL))


def kernel(x, mem, positions, g_mix, w_in, w_up_ret, w_up_swa, sinks, w_o, g_x, g_mem,
           w_xq, w_xkv, w_xo, g_ffn, w_ffn_gate, w_ffn_up, w_ffn_down, g_final):
    assert x.shape == (1, SEQ, D_MODEL) and mem.shape == (1, N_MEM, D_MODEL)
    assert w_in.shape == (1, D_MODEL, D_IN)
    x2d = x.reshape(SEQ, D_MODEL)
    tables = _rope_tables(positions)
    later_weights = (w_ffn_gate, w_ffn_up, w_ffn_down, w_up_ret, w_up_swa, w_o)
    proj, *bf16_weights = _in_proj(x2d, g_mix[0], w_in[0], tables, tuple(w[0] for w in later_weights))
    w_gate, w_up, w_down, w_ur, w_us, w_out = bf16_weights
    yr = _retention(proj)
    ys = _swa(proj, sinks[0])
    kv = _mem_kv(mem.reshape(N_MEM, D_MODEL), g_mem[0], w_xkv[0].astype(BF16))
    h2, n2 = _merge(x2d, yr, ys, proj, kv, w_ur, w_us, w_out, g_x[0],
                    w_xq[0].astype(BF16), w_xo[0].astype(BF16), g_ffn[0])
    out = _ffn(n2, h2, w_gate, w_up, w_down, g_final)
    return out.reshape(1, SEQ, D_MODEL)
```

```python
import jax
import jax.numpy as jnp
import numpy as np
from jax import lax
from jax.experimental import pallas as pl
from jax.experimental.pallas import tpu as pltpu

F32 = jnp.float32
BF16 = jnp.bfloat16

D_MODEL = 2048
SEQ = 16384
N_MEM = 256
EPS = 1e-6
RET_HEADS = 4
RET_DK = 256
RET_DV = 256
RET_THETA = 10000.0
SWA_HEADS = 16
SWA_KV_HEADS = 4
SWA_HD = 64
WINDOW = 128
ROPE_THETA = 500000.0
ROPE_DIM = SWA_HD // 4
X_HEADS = 4
X_HD = 128
D_FF = -(-8 * D_MODEL // (3 * 256)) * 256
IN_SPLITS = (RET_HEADS * RET_DK, RET_HEADS * RET_DK, RET_HEADS * RET_DV, RET_HEADS * RET_DV,
             SWA_HEADS * SWA_HD, SWA_KV_HEADS * SWA_HD, SWA_KV_HEADS * SWA_HD, D_MODEL, D_MODEL)
D_IN = sum(IN_SPLITS)
IN_OFFSETS = tuple(int(o) for o in np.cumsum((0,) + IN_SPLITS[:-1]))
Q_R, K_R, V_R, G_R, Q_S, K_S, V_S, GATE_R, GATE_S = range(9)
GATE_BLOCK = 512
NEG = -1e30

LANES = 128
BF16_SUBLANES = 16
MXU_WIDTH = 256
VMEM_LIMIT = 58 * 1024 * 1024

RET_CHUNK = 256
RET_ROWS = 1024
ROPE_ROWS = 2048
PROJ_TM, PROJ_TN = 2048, 512
NORM_SLAB = 256
ROPE_SLAB = 512
CAST_STEPS_PER_ROW = 16
CAST_ROW_BLOCKS, CAST_COL_BLOCKS = 32, 4
SWA_ROWS = 512
MERGE_TM = 256
FFN_TM, FFN_TF = 1024, 512
FFN_SLAB = MXU_WIDTH


def _sigmoid(x):
    return 1.0 / (1.0 + jnp.exp(-x))


def _rms(x):
    return x * lax.rsqrt(jnp.mean(x * x, axis=-1, keepdims=True) + EPS)


def _dot(a, b):
    return jnp.dot(a, b, preferred_element_type=F32)


def _dot_nt(a, b):
    return lax.dot_general(a, b, (((1,), (1,)), ((), ())), preferred_element_type=F32)


def _dot_tn(a, b):
    return lax.dot_general(a, b, (((0,), (0,)), ((), ())), preferred_element_type=F32)


def _params(*sem):
    return pltpu.CompilerParams(dimension_semantics=sem, vmem_limit_bytes=VMEM_LIMIT)


def _resident(shape):
    nd = len(shape)
    return pl.BlockSpec(shape, lambda *_: (0,) * nd, pipeline_mode=pl.Buffered(1))


def _rope_tables_kernel(pos_ref, invr_ref, invs_ref, sign_ref, spread_ref, rc_ref, rs_ref, sc_ref, ss_ref):
    pos = pos_ref[...].astype(F32)
    ang_r = pos * invr_ref[...]
    rc_ref[...] = jnp.cos(ang_r)
    rs_ref[...] = jnp.sin(ang_r)

    pack = LANES // (ROPE_DIM // 2)
    n_packed = ROPE_ROWS // pack
    ang = jnp.zeros((n_packed, LANES), F32)
    for k in range(pack):
        pos_k = pos_ref[pl.ds(k, n_packed, stride=pack), :].astype(F32)
        ang = ang + pos_k * invs_ref[k:k + 1, :]
    cos_p, sin_p = jnp.cos(ang), jnp.sin(ang)

    def spread(packed, k):
        total = jnp.zeros(packed.shape, F32)
        rest = packed
        for _ in range(3):
            term = rest.astype(BF16)
            total = total + _dot(term, spread_ref[k])
            rest = rest - term.astype(F32)
        return total

    rotated = sign_ref[...] != 0.0
    for k in range(pack):
        rows = pl.ds(k, n_packed, stride=pack)
        sc_ref[rows, :] = jnp.where(rotated, spread(cos_p, k), 1.0)
        ss_ref[rows, :] = spread(sin_p, k) * sign_ref[...]


def _rope_tables(positions):
    pos = positions.reshape(SEQ, 1)
    half_r = RET_DK // 2
    inv_r = 1.0 / (RET_THETA ** (jnp.arange(half_r, dtype=F32) / half_r))
    half_s = ROPE_DIM // 2
    inv_s = 1.0 / (ROPE_THETA ** (jnp.arange(half_s, dtype=F32) / half_s))
    d = np.arange(LANES) % SWA_HD
    sign = jnp.asarray(np.where(d < half_s, -1.0, np.where(d < ROPE_DIM, 1.0, 0.0)).astype(np.float32))
    pack = LANES // half_s
    assert ROPE_ROWS % (pack * 8) == 0
    lane = np.arange(LANES)
    inv_packed = jnp.where(lane[None, :] // half_s == np.arange(pack)[:, None],
                           inv_s[lane % half_s][None, :], 0.0).astype(F32)
    spread = np.zeros((pack, LANES, LANES), np.float32)
    for k in range(pack):
        for out_lane in np.nonzero(d < ROPE_DIM)[0]:
            spread[k, k * half_s + d[out_lane] % half_s, out_lane] = 1.0
    row = lambda v: v.reshape(1, LANES)
    tab = jax.ShapeDtypeStruct((SEQ, LANES), F32)
    vec = pl.BlockSpec((1, LANES), lambda i: (0, 0))
    blk = pl.BlockSpec((ROPE_ROWS, LANES), lambda i: (i, 0))
    return pl.pallas_call(
        _rope_tables_kernel,
        grid=(SEQ // ROPE_ROWS,),
        in_specs=[pl.BlockSpec((ROPE_ROWS, 1), lambda i: (i, 0)), vec,
                  pl.BlockSpec((pack, LANES), lambda i: (0, 0)), vec,
                  pl.BlockSpec((pack, LANES, LANES), lambda i: (0, 0, 0))],
        out_specs=[blk] * 4,
        out_shape=[tab] * 4,
        compiler_params=_params("parallel"),
        name="rope_tables",
    )(pos, row(inv_r), inv_packed, row(sign), jnp.asarray(spread, BF16))


def _in_proj_kernel(x_hbm, g_ref, w_ref, rc_ref, rs_ref, sc_ref, ss_ref, perm_ref, *refs):
    tm, tn = PROJ_TM, PROJ_TN
    i, j = pl.program_id(0), pl.program_id(1)
    blk = lambda split: IN_OFFSETS[split] // tn
    *io_refs, n_ref, acc_ref, x_buf, x_sem = refs
    n_cast = len(io_refs) // 2
    cast_in, o_ref, cast_out = io_refs[:n_cast], io_refs[n_cast], io_refs[n_cast + 1:]
    for src, dst in zip(cast_in, cast_out):
        dst[...] = src[...].astype(BF16)

    def x_copy(block):
        rows = pl.ds(pl.multiple_of(block * tm, tm), tm)
        return pltpu.make_async_copy(x_hbm.at[rows, :], x_buf, x_sem)

    @pl.when(j == 0)
    def _():
        @pl.when(i == 0)
        def _():
            x_copy(0).start()

        x_copy(i).wait()
        for c in range(tm // NORM_SLAB):
            r = slice(c * NORM_SLAB, (c + 1) * NORM_SLAB)
            n_ref[r, :] = (_rms(x_buf[r, :]) * g_ref[...]).astype(BF16)

    @pl.when((j == 1) & (i + 1 < pl.num_programs(0)))
    def _():
        x_copy(i + 1).start()

    acc_ref[...] = _dot(n_ref[...], w_ref[...].astype(BF16))
    o_ref[...] = acc_ref[...].astype(BF16)
    is_ret = (j >= blk(Q_R)) & (j < blk(V_R))
    is_qs = (j >= blk(Q_S)) & (j < blk(K_S))
    is_kvs = j == blk(K_S)

    def swa_rope(pair, scale):
        cols = slice(pair * MXU_WIDTH, (pair + 1) * MXU_WIDTH)
        for c in range(tm // ROPE_SLAB):
            r = slice(c * ROPE_SLAB, (c + 1) * ROPE_SLAB)
            xt = acc_ref[r, cols]
            hi = xt.astype(BF16)
            lo = (xt - hi.astype(F32)).astype(BF16)
            partner = _dot(hi, perm_ref[...]) + _dot(lo, perm_ref[...])
            cos = jnp.concatenate([sc_ref[r, :]] * (MXU_WIDTH // LANES), axis=-1)
            sin = jnp.concatenate([ss_ref[r, :]] * (MXU_WIDTH // LANES), axis=-1)
            o_ref[r, cols] = ((xt * cos + partner * sin) * scale).astype(BF16)

    @pl.when(is_ret)
    def _():
        scale = jnp.where(j >= blk(K_R), RET_DK ** -0.5, 1.0).astype(F32)
        cos, sin = rc_ref[...] * scale, rs_ref[...] * scale
        half = RET_DK // 2
        for h in range(tn // RET_DK):
            a1 = acc_ref[:, h * RET_DK:h * RET_DK + half]
            a2 = acc_ref[:, h * RET_DK + half:(h + 1) * RET_DK]
            o_ref[:, h * RET_DK:h * RET_DK + half] = (a1 * cos - a2 * sin).astype(BF16)
            o_ref[:, h * RET_DK + half:(h + 1) * RET_DK] = (a2 * cos + a1 * sin).astype(BF16)

    @pl.when(is_qs)
    def _():
        for pair in range(tn // MXU_WIDTH):
            swa_rope(pair, SWA_HD ** -0.5)

    @pl.when(is_kvs)
    def _():
        for pair in range(IN_SPLITS[K_S] // MXU_WIDTH):
            swa_rope(pair, 1.0)


def _rope_partner_matrix():
    half = ROPE_DIM // 2
    p = np.zeros((MXU_WIDTH, MXU_WIDTH), np.float32)
    for lane in range(MXU_WIDTH):
        d = lane % SWA_HD
        if d < half:
            p[lane + half, lane] = 1.0
        elif d < ROPE_DIM:
            p[lane - half, lane] = 1.0
    return jnp.asarray(p, BF16)


def _in_proj(x2d, g, w, tables, cast_weights):
    tm, tn = PROJ_TM, PROJ_TN
    assert all(IN_OFFSETS[s] % tn == 0 for s in (Q_R, K_R, V_R, Q_S, K_S))
    assert IN_OFFSETS[V_S] == IN_OFFSETS[K_S] + IN_SPLITS[K_S] and IN_SPLITS[K_S] + IN_SPLITS[V_S] == tn
    tab = pl.BlockSpec((tm, LANES), lambda i, j: (i, 0))
    n_i, n_j = SEQ // tm, D_IN // tn
    assert n_i * CAST_STEPS_PER_ROW == CAST_ROW_BLOCKS * CAST_COL_BLOCKS and CAST_STEPS_PER_ROW <= n_j

    def cast_spec(shape):
        rows, cols = shape[0] // CAST_ROW_BLOCKS, shape[1] // CAST_COL_BLOCKS
        assert rows % BF16_SUBLANES == 0 and cols % LANES == 0 and rows * CAST_ROW_BLOCKS == shape[0]

        def index(i, j):
            b = i * CAST_STEPS_PER_ROW + jnp.minimum(j, CAST_STEPS_PER_ROW - 1)
            return b // CAST_COL_BLOCKS, b % CAST_COL_BLOCKS
        return pl.BlockSpec((rows, cols), index)

    cast_specs = [cast_spec(cw.shape) for cw in cast_weights]
    return pl.pallas_call(
        _in_proj_kernel,
        grid=(SEQ // tm, D_IN // tn),
        in_specs=[pl.BlockSpec(memory_space=pl.ANY),
                  pl.BlockSpec((1, D_MODEL), lambda i, j: (0, 0)),
                  pl.BlockSpec((D_MODEL, tn), lambda i, j: (0, j)),
                  tab, tab, tab, tab, _resident((MXU_WIDTH, MXU_WIDTH)), *cast_specs],
        out_specs=[pl.BlockSpec((tm, tn), lambda i, j: (i, j)), *cast_specs],
        out_shape=[jax.ShapeDtypeStruct((SEQ, D_IN), BF16)]
                  + [jax.ShapeDtypeStruct(cw.shape, BF16) for cw in cast_weights],
        scratch_shapes=[pltpu.VMEM((tm, D_MODEL), BF16), pltpu.VMEM((tm, tn), F32),
                        pltpu.VMEM((tm, D_MODEL), F32), pltpu.SemaphoreType.DMA(())],
        compiler_params=_params("arbitrary", "arbitrary"),
        name="in_proj",
    )(x2d, g.reshape(1, D_MODEL), w, *tables, _rope_partner_matrix(), *cast_weights)


def _retention_kernel(logg_ref, q_ref, k_ref, v_ref, g_ref, o_ref, state_ref, dmask_ref):
    c_rows = RET_CHUNK
    idx = lax.broadcasted_iota(jnp.int32, (c_rows, 1), 0).astype(F32)

    @pl.when(pl.program_id(0) == 0)
    def _():
        state_ref[...] = jnp.zeros_like(state_ref)
        i = lax.broadcasted_iota(jnp.int32, (c_rows, c_rows), 0)
        j = lax.broadcasted_iota(jnp.int32, (c_rows, c_rows), 1)
        rel = (i - j).astype(F32)
        for h in range(RET_HEADS):
            dmask_ref[h] = jnp.where(rel >= 0, jnp.exp(logg_ref[h] * jnp.maximum(rel, 0.0)), 0.0)

    for h in range(RET_HEADS):
        lg = logg_ref[h]
        cols = slice(h * RET_DK, (h + 1) * RET_DK)
        q_dec = jnp.exp(lg * (idx + 1.0))
        k_dec = jnp.exp(lg * (c_rows - 1.0 - idx))
        chunk_decay = jnp.exp(jnp.full((1, RET_DV), lg * c_rows, F32))
        for c in range(RET_ROWS // c_rows):
            r = slice(c * c_rows, (c + 1) * c_rows)
            q, k, v = q_ref[r, cols], k_ref[r, cols], v_ref[r, cols]
            scores = _dot_nt(q, k) * dmask_ref[h]
            inner = _dot(scores.astype(BF16), v)
            state = state_ref[h]
            cross = _dot((q.astype(F32) * q_dec).astype(BF16), state.astype(BF16))
            state_ref[h] = state * chunk_decay + _dot_tn((k.astype(F32) * k_dec).astype(BF16), v)
            y = _rms(inner + cross)
            g = g_ref[r, cols].astype(F32)
            o_ref[r, cols] = (y * (g * _sigmoid(g))).astype(o_ref.dtype)


def _retention(proj):
    log_g = jnp.log(1.0 - jnp.power(2.0, -5.0 - jnp.arange(RET_HEADS, dtype=F32)))
    rows, width = RET_ROWS, RET_HEADS * RET_DK
    assert RET_DK == RET_DV
    all_heads = lambda split: pl.BlockSpec((rows, width), lambda n, c=IN_OFFSETS[split] // width: (n, c))
    assert all(IN_OFFSETS[s] % width == 0 for s in (Q_R, K_R, V_R, G_R))
    return pl.pallas_call(
        _retention_kernel,
        grid=(SEQ // rows,),
        in_specs=[pl.BlockSpec(memory_space=pltpu.SMEM),
                  all_heads(Q_R), all_heads(K_R), all_heads(V_R), all_heads(G_R)],
        out_specs=pl.BlockSpec((rows, width), lambda n: (n, 0)),
        out_shape=jax.ShapeDtypeStruct((SEQ, width), BF16),
        scratch_shapes=[pltpu.VMEM((RET_HEADS, RET_DK, RET_DV), F32),
                        pltpu.VMEM((RET_HEADS, RET_CHUNK, RET_CHUNK), F32)],
        compiler_params=_params("arbitrary"),
        name="retention",
    )(log_g, proj, proj, proj, proj)


def _swa_kernel(sinks_ref, q_ref, k_ref, v_ref, kp_ref, vp_ref, o_ref):
    w = WINDOW
    hd = SWA_HD
    group = SWA_HEADS // SWA_KV_HEADS
    step = pl.program_id(0)
    low = lax.broadcasted_iota(jnp.int32, (w, LANES), 1) < hd

    def dup_halves(x):
        xf = x.astype(F32)
        sw = pltpu.roll(xf, hd, 1)
        lo = lax.broadcasted_iota(jnp.int32, xf.shape, 1) < hd
        return jnp.where(lo, xf, sw).astype(BF16), jnp.where(lo, sw, xf).astype(BF16)

    def per_kv_head(cur_ref, prev_ref):
        allrows = jnp.concatenate([prev_ref[...], cur_ref[...]], axis=0)
        heads = []
        for c in range(allrows.shape[1] // LANES):
            heads.extend(dup_halves(allrows[:, c * LANES:(c + 1) * LANES]))
        return heads

    k2 = per_kv_head(k_ref, kp_ref)
    v2 = per_kv_head(v_ref, vp_ref)

    qi = lax.broadcasted_iota(jnp.int32, (w, 2 * w), 0)
    kj = lax.broadcasted_iota(jnp.int32, (w, 2 * w), 1)
    dist = qi + w - kj
    band = (dist >= 0) & (dist < WINDOW)
    zero = jnp.zeros((w, LANES), BF16)
    ones = jnp.ones((2 * w, LANES), BF16)
    zero_kv = jnp.zeros((2 * w, LANES), BF16)

    tile = lambda a: jnp.concatenate([a] * group, axis=0)
    bias_cur = tile(jnp.where(band[:, w:], 0.0, NEG))
    bias_prev = tile(jnp.where(band[:, :w], 0.0, NEG))
    sink_slot = lax.broadcasted_iota(jnp.int32, (group * w, LANES), 1) == 0
    first_row = lax.broadcasted_iota(jnp.int32, (2 * w, LANES), 0) == 0

    for kh in range(SWA_KV_HEADS):
        cols = [slice((kh * group // 2 + p) * LANES, (kh * group // 2 + p + 1) * LANES)
                for p in range(group // 2)]
        sink = jnp.concatenate([jnp.full((w, LANES), sinks_ref[kh * group + g], F32)
                                for g in range(group)], axis=0)
        bias_later = jnp.concatenate([jnp.where(sink_slot, sink, bias_prev), bias_cur], axis=-1)
        bias_first = jnp.concatenate(
            [jnp.where(sink_slot, sink, jnp.where(step > 0, bias_prev, NEG)), bias_cur], axis=-1)
        for t in range(SWA_ROWS // w):
            rows = slice(t * w, (t + 1) * w)
            bias = bias_first if t == 0 else bias_later
            k_w = jnp.where(first_row, zero_kv, k2[kh][t * w:(t + 2) * w])
            v_w = jnp.where(first_row, zero_kv, v2[kh][t * w:(t + 2) * w])
            parts = []
            for c in cols:
                qc = q_ref[rows, c]
                parts += [jnp.where(low, qc, zero), jnp.where(low, zero, qc)]
            s = _dot_nt(jnp.concatenate(parts, axis=0), k_w) + bias
            e = jnp.exp(s - jnp.max(s, axis=-1, keepdims=True))
            pv = _dot(e.astype(BF16), jnp.concatenate([v_w, ones], axis=-1))
            o = pv[:, :LANES] / pv[:, LANES:]
            for p, c in enumerate(cols):
                o_ref[rows, c] = jnp.where(low, o[2 * p * w:(2 * p + 1) * w],
                                           o[(2 * p + 1) * w:(2 * p + 2) * w]).astype(o_ref.dtype)


def _swa(proj, sinks):
    w, rows = WINDOW, SWA_ROWS
    assert 2 * SWA_HD == LANES and WINDOW == LANES and (SWA_HEADS // SWA_KV_HEADS) % 2 == 0
    nq, nkv = SWA_HEADS * SWA_HD, SWA_KV_HEADS * SWA_HD
    per = rows // w
    cur = lambda split: pl.BlockSpec((rows, nkv), lambda i, c=IN_OFFSETS[split] // nkv: (i, c))
    prev = lambda split: pl.BlockSpec(
        (w, nkv), lambda i, c=IN_OFFSETS[split] // nkv: (jnp.maximum(i * per - 1, 0), c))
    return pl.pallas_call(
        _swa_kernel,
        grid=(SEQ // rows,),
        in_specs=[pl.BlockSpec(memory_space=pltpu.SMEM),
                  pl.BlockSpec((rows, nq), lambda i: (i, IN_OFFSETS[Q_S] // nq)),
                  cur(K_S), cur(V_S), prev(K_S), prev(V_S)],
        out_specs=pl.BlockSpec((rows, nq), lambda i: (i, 0)),
        out_shape=jax.ShapeDtypeStruct((SEQ, nq), BF16),
        compiler_params=_params("parallel"),
        name="swa",
    )(sinks, proj, proj, proj, proj, proj)


def _mem_kv_kernel(mem_ref, g_ref, w_ref, o_ref):
    memn = (_rms(mem_ref[...]) * g_ref[...]).astype(BF16)
    o_ref[...] = _dot(memn, w_ref[...]).astype(o_ref.dtype)


def _mem_kv(mem2d, g, w):
    n_out = 2 * X_HEADS * X_HD
    return pl.pallas_call(
        _mem_kv_kernel,
        out_shape=jax.ShapeDtypeStruct((N_MEM, n_out), BF16),
        compiler_params=pltpu.CompilerParams(vmem_limit_bytes=VMEM_LIMIT),
        name="mem_kv",
    )(mem2d, g.reshape(1, D_MODEL), w)


def _merge_kernel(x_ref, yr_ref, ys_ref, *refs):
    n_gate = D_MODEL // GATE_BLOCK
    gr_refs, gs_refs = refs[:n_gate], refs[n_gate:2 * n_gate]
    (kv_ref, wur_ref, wus_ref, wo_ref, gx_ref, wxq_ref, wxo_ref, gf_ref, h_ref, n_ref) = refs[2 * n_gate:]
    gate_r = jnp.concatenate([r[...] for r in gr_refs], axis=-1).astype(F32)
    gate_s = jnp.concatenate([r[...] for r in gs_refs], axis=-1).astype(F32)
    up_r = _dot(yr_ref[...], wur_ref[...])
    up_s = _dot(ys_ref[...], wus_ref[...])
    merged = _sigmoid(gate_r) * up_r + _sigmoid(gate_s) * up_s
    h1 = x_ref[...] + _dot(merged.astype(BF16), wo_ref[...])

    nx = (_rms(h1) * gx_ref[...]).astype(BF16)
    qx = _dot(nx, wxq_ref[...]).astype(BF16)
    kv = kv_ref[...]
    outs = []
    for h in range(X_HEADS):
        q_h = qx[:, h * X_HD:(h + 1) * X_HD]
        k_h = kv[:, h * X_HD:(h + 1) * X_HD]
        v_h = kv[:, (X_HEADS + h) * X_HD:(X_HEADS + h + 1) * X_HD]
        s = _dot_nt(q_h, k_h) * (X_HD ** -0.5)
        e = jnp.exp(s - jnp.max(s, axis=-1, keepdims=True))
        p = e / jnp.sum(e, axis=-1, keepdims=True)
        outs.append(_dot(p.astype(BF16), v_h))
    att = jnp.concatenate(outs, axis=-1).astype(BF16)
    h2 = h1 + _dot(att, wxo_ref[...])
    h_ref[...] = h2
    n_ref[...] = (_rms(h2) * gf_ref[...]).astype(BF16)


def _merge(x2d, yr, ys, proj, kv, w_up_ret, w_up_swa, w_o, g_x, w_xq, w_xo, g_ffn):
    tm = MERGE_TM
    rows = lambda width, col=0: pl.BlockSpec((tm, width), lambda i, c=col: (i, c))
    n_gate = D_MODEL // GATE_BLOCK
    gate = lambda split: [rows(GATE_BLOCK, IN_OFFSETS[split] // GATE_BLOCK + b) for b in range(n_gate)]
    assert IN_OFFSETS[GATE_R] % GATE_BLOCK == 0 and IN_OFFSETS[GATE_S] % GATE_BLOCK == 0
    n_x = X_HEADS * X_HD
    return pl.pallas_call(
        _merge_kernel,
        grid=(SEQ // tm,),
        in_specs=[rows(D_MODEL), rows(RET_HEADS * RET_DV), rows(SWA_HEADS * SWA_HD),
                  *gate(GATE_R), *gate(GATE_S),
                  _resident((N_MEM, 2 * n_x)),
                  _resident((RET_HEADS * RET_DV, D_MODEL)), _resident((SWA_HEADS * SWA_HD, D_MODEL)),
                  _resident((D_MODEL, D_MODEL)), _resident((1, D_MODEL)),
                  _resident((D_MODEL, n_x)), _resident((n_x, D_MODEL)), _resident((1, D_MODEL))],
        out_specs=[rows(D_MODEL), rows(D_MODEL)],
        out_shape=[jax.ShapeDtypeStruct((SEQ, D_MODEL), F32),
                   jax.ShapeDtypeStruct((SEQ, D_MODEL), BF16)],
        compiler_params=_params("parallel"),
        name="merge_xattn",
    )(x2d, yr, ys, *([proj] * (2 * n_gate)), kv, w_up_ret, w_up_swa, w_o, g_x.reshape(1, D_MODEL),
      w_xq, w_xo, g_ffn.reshape(1, D_MODEL))


def _ffn_kernel(n_ref, h_hbm, wg_ref, wu_ref, wd_ref, gf_ref, o_ref, h_ref, h_sem):
    i, f = pl.program_id(0), pl.program_id(1)
    rows = pl.ds(pl.multiple_of(i * FFN_TM, FFN_TM), FFN_TM)
    residual_copy = pltpu.make_async_copy(h_hbm.at[rows, :], h_ref, h_sem)

    @pl.when(f == 0)
    def _():
        residual_copy.start()
        for c in range(FFN_TM // FFN_SLAB):
            o_ref[c * FFN_SLAB:(c + 1) * FFN_SLAB, :] = jnp.zeros((FFN_SLAB, D_MODEL), F32)

    n2 = n_ref[...]
    ts = []
    for c in range(FFN_TF // FFN_SLAB):
        cols = slice(c * FFN_SLAB, (c + 1) * FFN_SLAB)
        a = _dot(n2, wg_ref[:, cols])
        b = _dot(n2, wu_ref[:, cols])
        ts.append((a * _sigmoid(a) * b).astype(BF16))
    t = jnp.concatenate(ts, axis=-1)
    for c in range(D_MODEL // FFN_TF):
        cols = slice(c * FFN_TF, (c + 1) * FFN_TF)
        o_ref[:, cols] += _dot(t, wd_ref[:, cols])

    @pl.when(f == pl.num_programs(1) - 1)
    def _():
        residual_copy.wait()
        for c in range(FFN_TM // FFN_SLAB):
            r = slice(c * FFN_SLAB, (c + 1) * FFN_SLAB)
            o_ref[r, :] = _rms(h_ref[r, :] + o_ref[r, :]) * gf_ref[...]


def _ffn(n2, h2, w_gate, w_up, w_down, g_final):
    tm, tf = FFN_TM, FFN_TF
    return pl.pallas_call(
        _ffn_kernel,
        grid=(SEQ // tm, D_FF // tf),
        in_specs=[pl.BlockSpec((tm, D_MODEL), lambda i, f: (i, 0)),
                  pl.BlockSpec(memory_space=pl.ANY),
                  pl.BlockSpec((D_MODEL, tf), lambda i, f: (0, f)),
                  pl.BlockSpec((D_MODEL, tf), lambda i, f: (0, f)),
                  pl.BlockSpec((tf, D_MODEL), lambda i, f: (f, 0)),
                  pl.BlockSpec((1, D_MODEL), lambda i, f: (0, 0))],
        out_specs=pl.BlockSpec((tm, D_MODEL), lambda i, f: (i, 0)),
        out_shape=jax.ShapeDtypeStruct((SEQ, D_MODEL), F32),
        scratch_shapes=[pltpu.VMEM((tm, D_MODEL), F32), pltpu.SemaphoreType.DMA(())],
        compiler_params=_params("arbitrary", "arbitrary"),
        name="ffn",
    )(n2, h2, w_gate, w_up, w_down, g_final.reshape(1, D_MODEL))


def kernel(x, mem, positions, g_mix, w_in, w_up_ret, w_up_swa, sinks, w_o, g_x, g_mem,
           w_xq, w_xkv, w_xo, g_ffn, w_ffn_gate, w_ffn_up, w_ffn_down, g_final):
    assert x.shape == (1, SEQ, D_MODEL) and mem.shape == (1, N_MEM, D_MODEL)
    assert w_in.shape == (1, D_MODEL, D_IN)
    x2d = x.reshape(SEQ, D_MODEL)
    tables = _rope_tables(positions)
    later_weights = (w_ffn_gate, w_ffn_up, w_ffn_down, w_up_ret, w_up_swa, w_o)
    proj, *bf16_weights = _in_proj(x2d, g_mix[0], w_in[0], tables, tuple(w[0] for w in later_weights))
    w_gate, w_up, w_down, w_ur, w_us, w_out = bf16_weights
    yr = _retention(proj)
    ys = _swa(proj, sinks[0])
    kv = _mem_kv(mem.reshape(N_MEM, D_MODEL), g_mem[0], w_xkv[0].astype(BF16))
    h2, n2 = _merge(x2d, yr, ys, proj, kv, w_ur, w_us, w_out, g_x[0],
                    w_xq[0].astype(BF16), w_xo[0].astype(BF16), g_ffn[0])
    out = _ffn(n2, h2, w_gate, w_up, w_down, g_final)
    return out.reshape(1, SEQ, D_MODEL)
```

```python
import jax
import jax.numpy as jnp
import numpy as np
from jax import lax
from jax.experimental import pallas as pl
from jax.experimental.pallas import tpu as pltpu

F32 = jnp.float32
BF16 = jnp.bfloat16

D_MODEL = 2048
SEQ = 16384
N_MEM = 256
EPS = 1e-6
RET_HEADS = 4
RET_DK = 256
RET_DV = 256
RET_THETA = 10000.0
SWA_HEADS = 16
SWA_KV_HEADS = 4
SWA_HD = 64
WINDOW = 128
ROPE_THETA = 500000.0
ROPE_DIM = SWA_HD // 4
X_HEADS = 4
X_HD = 128
D_FF = -(-8 * D_MODEL // (3 * 256)) * 256
IN_SPLITS = (RET_HEADS * RET_DK, RET_HEADS * RET_DK, RET_HEADS * RET_DV, RET_HEADS * RET_DV,
             SWA_HEADS * SWA_HD, SWA_KV_HEADS * SWA_HD, SWA_KV_HEADS * SWA_HD, D_MODEL, D_MODEL)
D_IN = sum(IN_SPLITS)
IN_OFFSETS = tuple(int(o) for o in np.cumsum((0,) + IN_SPLITS[:-1]))
Q_R, K_R, V_R, G_R, Q_S, K_S, V_S, GATE_R, GATE_S = range(9)
GATE_BLOCK = 512
NEG = -1e30

LANES = 128
BF16_SUBLANES = 16
MXU_WIDTH = 256
VMEM_LIMIT = 58 * 1024 * 1024

RET_CHUNK = 256
RET_ROWS = 1024
ROPE_ROWS = 2048
PROJ_TM, PROJ_TN = 2048, 512
NORM_SLAB = 256
ROPE_SLAB = 512
CAST_STEPS_PER_ROW = 16
CAST_ROW_BLOCKS, CAST_COL_BLOCKS = 32, 4
SWA_ROWS = 512
MERGE_TM = 256
FFN_TM, FFN_TF = 1024, 512
FFN_SLAB = MXU_WIDTH


def _sigmoid(x):
    return 1.0 / (1.0 + jnp.exp(-x))


def _rms(x):
    return x * lax.rsqrt(jnp.mean(x * x, axis=-1, keepdims=True) + EPS)


def _dot(a, b):
    return jnp.dot(a, b, preferred_element_type=F32)


def _dot_nt(a, b):
    return lax.dot_general(a, b, (((1,), (1,)), ((), ())), preferred_element_type=F32)


def _dot_tn(a, b):
    return lax.dot_general(a, b, (((0,), (0,)), ((), ())), preferred_element_type=F32)


def _params(*sem):
    return pltpu.CompilerParams(dimension_semantics=sem, vmem_limit_bytes=VMEM_LIMIT)


def _resident(shape):
    nd = len(shape)
    return pl.BlockSpec(shape, lambda *_: (0,) * nd, pipeline_mode=pl.Buffered(1))


def _rope_tables_kernel(pos_ref, invr_ref, invs_ref, sign_ref, spread_ref, *refs):
    n_cast = (len(refs) - 4) // 2
    cast_in, (rc_ref, rs_ref, sc_ref, ss_ref), cast_out = refs[:n_cast], refs[n_cast:n_cast + 4], refs[n_cast + 4:]
    for src, dst in zip(cast_in, cast_out):
        dst[...] = src[...].astype(BF16)

    pos = pos_ref[...].astype(F32)
    ang_r = pos * invr_ref[...]
    rc_ref[...] = jnp.cos(ang_r)
    rs_ref[...] = jnp.sin(ang_r)

    pack = LANES // (ROPE_DIM // 2)
    n_packed = ROPE_ROWS // pack
    ang = jnp.zeros((n_packed, LANES), F32)
    for k in range(pack):
        pos_k = pos_ref[pl.ds(k, n_packed, stride=pack), :].astype(F32)
        ang = ang + pos_k * invs_ref[k:k + 1, :]
    cos_p, sin_p = jnp.cos(ang), jnp.sin(ang)

    def spread(packed, k):
        total = jnp.zeros(packed.shape, F32)
        rest = packed
        for _ in range(3):
            term = rest.astype(BF16)
            total = total + _dot(term, spread_ref[k])
            rest = rest - term.astype(F32)
        return total

    rotated = sign_ref[...] != 0.0
    for k in range(pack):
        rows = pl.ds(k, n_packed, stride=pack)
        sc_ref[rows, :] = jnp.where(rotated, spread(cos_p, k), 1.0)
        ss_ref[rows, :] = spread(sin_p, k) * sign_ref[...]


def _rope_tables(positions, cast_weights=()):
    pos = positions.reshape(SEQ, 1)
    steps = SEQ // ROPE_ROWS
    assert all(cw.shape[0] % (steps * BF16_SUBLANES) == 0 for cw in cast_weights)
    slabs = [pl.BlockSpec((cw.shape[0] // steps, cw.shape[1]), lambda i: (i, 0)) for cw in cast_weights]
    half_r = RET_DK // 2
    inv_r = 1.0 / (RET_THETA ** (jnp.arange(half_r, dtype=F32) / half_r))
    half_s = ROPE_DIM // 2
    inv_s = 1.0 / (ROPE_THETA ** (jnp.arange(half_s, dtype=F32) / half_s))
    d = np.arange(LANES) % SWA_HD
    sign = jnp.asarray(np.where(d < half_s, -1.0, np.where(d < ROPE_DIM, 1.0, 0.0)).astype(np.float32))
    pack = LANES // half_s
    assert ROPE_ROWS % (pack * 8) == 0
    lane = np.arange(LANES)
    inv_packed = jnp.where(lane[None, :] // half_s == np.arange(pack)[:, None],
                           inv_s[lane % half_s][None, :], 0.0).astype(F32)
    spread = np.zeros((pack, LANES, LANES), np.float32)
    for k in range(pack):
        for out_lane in np.nonzero(d < ROPE_DIM)[0]:
            spread[k, k * half_s + d[out_lane] % half_s, out_lane] = 1.0
    row = lambda v: v.reshape(1, LANES)
    tab = jax.ShapeDtypeStruct((SEQ, LANES), F32)
    vec = pl.BlockSpec((1, LANES), lambda i: (0, 0))
    blk = pl.BlockSpec((ROPE_ROWS, LANES), lambda i: (i, 0))
    outs = pl.pallas_call(
        _rope_tables_kernel,
        grid=(steps,),
        in_specs=[pl.BlockSpec((ROPE_ROWS, 1), lambda i: (i, 0)), vec,
                  pl.BlockSpec((pack, LANES), lambda i: (0, 0)), vec,
                  pl.BlockSpec((pack, LANES, LANES), lambda i: (0, 0, 0)), *slabs],
        out_specs=[blk] * 4 + slabs,
        out_shape=[tab] * 4 + [jax.ShapeDtypeStruct(cw.shape, BF16) for cw in cast_weights],
        compiler_params=_params("parallel"),
        name="rope_tables",
    )(pos, row(inv_r), inv_packed, row(sign), jnp.asarray(spread, BF16), *cast_weights)
    return outs[:4], outs[4:]


def _in_proj_kernel(x_hbm, g_ref, w_ref, rc_ref, rs_ref, sc_ref, ss_ref, perm_ref, *refs):
    tm, tn = PROJ_TM, PROJ_TN
    i, j = pl.program_id(0), pl.program_id(1)
    blk = lambda split: IN_OFFSETS[split] // tn
    *io_refs, n_ref, acc_ref, x_buf, x_sem = refs
    n_cast = len(io_refs) // 2
    cast_in, o_ref, cast_out = io_refs[:n_cast], io_refs[n_cast], io_refs[n_cast + 1:]
    for src, dst in zip(cast_in, cast_out):
        dst[...] = src[...].astype(BF16)

    def x_copy(block):
        rows = pl.ds(pl.multiple_of(block * tm, tm), tm)
        return pltpu.make_async_copy(x_hbm.at[rows, :], x_buf, x_sem)

    @pl.when(j == 0)
    def _():
        @pl.when(i == 0)
        def _():
            x_copy(0).start()

        x_copy(i).wait()
        for c in range(tm // NORM_SLAB):
            r = slice(c * NORM_SLAB, (c + 1) * NORM_SLAB)
            n_ref[r, :] = (_rms(x_buf[r, :]) * g_ref[...]).astype(BF16)

    @pl.when((j == 1) & (i + 1 < pl.num_programs(0)))
    def _():
        x_copy(i + 1).start()

    acc_ref[...] = _dot(n_ref[...], w_ref[...].astype(BF16))
    o_ref[...] = acc_ref[...].astype(BF16)
    is_ret = (j >= blk(Q_R)) & (j < blk(V_R))
    is_qs = (j >= blk(Q_S)) & (j < blk(K_S))
    is_kvs = j == blk(K_S)

    def swa_rope(pair, scale):
        cols = slice(pair * MXU_WIDTH, (pair + 1) * MXU_WIDTH)
        for c in range(tm // ROPE_SLAB):
            r = slice(c * ROPE_SLAB, (c + 1) * ROPE_SLAB)
            xt = acc_ref[r, cols]
            hi = xt.astype(BF16)
            lo = (xt - hi.astype(F32)).astype(BF16)
            partner = _dot(hi, perm_ref[...]) + _dot(lo, perm_ref[...])
            cos = jnp.concatenate([sc_ref[r, :]] * (MXU_WIDTH // LANES), axis=-1)
            sin = jnp.concatenate([ss_ref[r, :]] * (MXU_WIDTH // LANES), axis=-1)
            o_ref[r, cols] = ((xt * cos + partner * sin) * scale).astype(BF16)

    @pl.when(is_ret)
    def _():
        scale = jnp.where(j >= blk(K_R), RET_DK ** -0.5, 1.0).astype(F32)
        cos, sin = rc_ref[...] * scale, rs_ref[...] * scale
        half = RET_DK // 2
        for h in range(tn // RET_DK):
            a1 = acc_ref[:, h * RET_DK:h * RET_DK + half]
            a2 = acc_ref[:, h * RET_DK + half:(h + 1) * RET_DK]
            o_ref[:, h * RET_DK:h * RET_DK + half] = (a1 * cos - a2 * sin).astype(BF16)
            o_ref[:, h * RET_DK + half:(h + 1) * RET_DK] = (a2 * cos + a1 * sin).astype(BF16)

    @pl.when(is_qs)
    def _():
        for pair in range(tn // MXU_WIDTH):
            swa_rope(pair, SWA_HD ** -0.5)

    @pl.when(is_kvs)
    def _():
        for pair in range(IN_SPLITS[K_S] // MXU_WIDTH):
            swa_rope(pair, 1.0)


def _rope_partner_matrix():
    half = ROPE_DIM // 2
    p = np.zeros((MXU_WIDTH, MXU_WIDTH), np.float32)
    for lane in range(MXU_WIDTH):
        d = lane % SWA_HD
        if d < half:
            p[lane + half, lane] = 1.0
        elif d < ROPE_DIM:
            p[lane - half, lane] = 1.0
    return jnp.asarray(p, BF16)


def _in_proj(x2d, g, w, tables, cast_weights):
    tm, tn = PROJ_TM, PROJ_TN
    assert all(IN_OFFSETS[s] % tn == 0 for s in (Q_R, K_R, V_R, Q_S, K_S))
    assert IN_OFFSETS[V_S] == IN_OFFSETS[K_S] + IN_SPLITS[K_S] and IN_SPLITS[K_S] + IN_SPLITS[V_S] == tn
    tab = pl.BlockSpec((tm, LANES), lambda i, j: (i, 0))
    n_i, n_j = SEQ // tm, D_IN // tn
    assert n_i * CAST_STEPS_PER_ROW == CAST_ROW_BLOCKS * CAST_COL_BLOCKS and CAST_STEPS_PER_ROW <= n_j

    def cast_spec(shape):
        rows, cols = shape[0] // CAST_ROW_BLOCKS, shape[1] // CAST_COL_BLOCKS
        assert rows % BF16_SUBLANES == 0 and cols % LANES == 0 and rows * CAST_ROW_BLOCKS == shape[0]

        def index(i, j):
            b = i * CAST_STEPS_PER_ROW + jnp.minimum(j, CAST_STEPS_PER_ROW - 1)
            return b // CAST_COL_BLOCKS, b % CAST_COL_BLOCKS
        return pl.BlockSpec((rows, cols), index)

    cast_specs = [cast_spec(cw.shape) for cw in cast_weights]
    return pl.pallas_call(
        _in_proj_kernel,
        grid=(SEQ // tm, D_IN // tn),
        in_specs=[pl.BlockSpec(memory_space=pl.ANY),
                  pl.BlockSpec((1, D_MODEL), lambda i, j: (0, 0)),
                  pl.BlockSpec((D_MODEL, tn), lambda i, j: (0, j)),
                  tab, tab, tab, tab, _resident((MXU_WIDTH, MXU_WIDTH)), *cast_specs],
        out_specs=[pl.BlockSpec((tm, tn), lambda i, j: (i, j)), *cast_specs],
        out_shape=[jax.ShapeDtypeStruct((SEQ, D_IN), BF16)]
                  + [jax.ShapeDtypeStruct(cw.shape, BF16) for cw in cast_weights],
        scratch_shapes=[pltpu.VMEM((tm, D_MODEL), BF16), pltpu.VMEM((tm, tn), F32),
                        pltpu.VMEM((tm, D_MODEL), F32), pltpu.SemaphoreType.DMA(())],
        compiler_params=_params("arbitrary", "arbitrary"),
        name="in_proj",
    )(x2d, g.reshape(1, D_MODEL), w, *tables, _rope_partner_matrix(), *cast_weights)


def _retention_kernel(logg_ref, q_ref, k_ref, v_ref, g_ref, o_ref, state_ref, dmask_ref):
    c_rows = RET_CHUNK
    idx = lax.broadcasted_iota(jnp.int32, (c_rows, 1), 0).astype(F32)

    @pl.when(pl.program_id(0) == 0)
    def _():
        state_ref[...] = jnp.zeros_like(state_ref)
        i = lax.broadcasted_iota(jnp.int32, (c_rows, c_rows), 0)
        j = lax.broadcasted_iota(jnp.int32, (c_rows, c_rows), 1)
        rel = (i - j).astype(F32)
        for h in range(RET_HEADS):
            dmask_ref[h] = jnp.where(rel >= 0, jnp.exp(logg_ref[h] * jnp.maximum(rel, 0.0)), 0.0)

    for h in range(RET_HEADS):
        lg = logg_ref[h]
        cols = slice(h * RET_DK, (h + 1) * RET_DK)
        q_dec = jnp.exp(lg * (idx + 1.0))
        k_dec = jnp.exp(lg * (c_rows - 1.0 - idx))
        chunk_decay = jnp.exp(jnp.full((1, RET_DV), lg * c_rows, F32))
        for c in range(RET_ROWS // c_rows):
            r = slice(c * c_rows, (c + 1) * c_rows)
            q, k, v = q_ref[r, cols], k_ref[r, cols], v_ref[r, cols]
            scores = _dot_nt(q, k) * dmask_ref[h]
            inner = _dot(scores.astype(BF16), v)
            state = state_ref[h]
            cross = _dot((q.astype(F32) * q_dec).astype(BF16), state.astype(BF16))
            state_ref[h] = state * chunk_decay + _dot_tn((k.astype(F32) * k_dec).astype(BF16), v)
            y = _rms(inner + cross)
            g = g_ref[r, cols].astype(F32)
            o_ref[r, cols] = (y * (g * _sigmoid(g))).astype(o_ref.dtype)


def _retention(proj):
    log_g = jnp.log(1.0 - jnp.power(2.0, -5.0 - jnp.arange(RET_HEADS, dtype=F32)))
    rows, width = RET_ROWS, RET_HEADS * RET_DK
    assert RET_DK == RET_DV
    all_heads = lambda split: pl.BlockSpec((rows, width), lambda n, c=IN_OFFSETS[split] // width: (n, c))
    assert all(IN_OFFSETS[s] % width == 0 for s in (Q_R, K_R, V_R, G_R))
    return pl.pallas_call(
        _retention_kernel,
        grid=(SEQ // rows,),
        in_specs=[pl.BlockSpec(memory_space=pltpu.SMEM),
                  all_heads(Q_R), all_heads(K_R), all_heads(V_R), all_heads(G_R)],
        out_specs=pl.BlockSpec((rows, width), lambda n: (n, 0)),
        out_shape=jax.ShapeDtypeStruct((SEQ, width), BF16),
        scratch_shapes=[pltpu.VMEM((RET_HEADS, RET_DK, RET_DV), F32),
                        pltpu.VMEM((RET_HEADS, RET_CHUNK, RET_CHUNK), F32)],
        compiler_params=_params("arbitrary"),
        name="retention",
    )(log_g, proj, proj, proj, proj)


def _swa_kernel(sinks_ref, q_ref, k_ref, v_ref, kp_ref, vp_ref, o_ref):
    w = WINDOW
    hd = SWA_HD
    group = SWA_HEADS // SWA_KV_HEADS
    step = pl.program_id(0)
    low = lax.broadcasted_iota(jnp.int32, (w, LANES), 1) < hd

    def dup_halves(x):
        xf = x.astype(F32)
        sw = pltpu.roll(xf, hd, 1)
        lo = lax.broadcasted_iota(jnp.int32, xf.shape, 1) < hd
        return jnp.where(lo, xf, sw).astype(BF16), jnp.where(lo, sw, xf).astype(BF16)

    def per_kv_head(cur_ref, prev_ref):
        allrows = jnp.concatenate([prev_ref[...], cur_ref[...]], axis=0)
        heads = []
        for c in range(allrows.shape[1] // LANES):
            heads.extend(dup_halves(allrows[:, c * LANES:(c + 1) * LANES]))
        return heads

    k2 = per_kv_head(k_ref, kp_ref)
    v2 = per_kv_head(v_ref, vp_ref)

    qi = lax.broadcasted_iota(jnp.int32, (w, 2 * w), 0)
    kj = lax.broadcasted_iota(jnp.int32, (w, 2 * w), 1)
    dist = qi + w - kj
    band = (dist >= 0) & (dist < WINDOW)
    zero = jnp.zeros((w, LANES), BF16)
    ones = jnp.ones((2 * w, LANES), BF16)
    zero_kv = jnp.zeros((2 * w, LANES), BF16)

    tile = lambda a: jnp.concatenate([a] * group, axis=0)
    bias_cur = tile(jnp.where(band[:, w:], 0.0, NEG))
    bias_prev = tile(jnp.where(band[:, :w], 0.0, NEG))
    sink_slot = lax.broadcasted_iota(jnp.int32, (group * w, LANES), 1) == 0
    first_row = lax.broadcasted_iota(jnp.int32, (2 * w, LANES), 0) == 0

    for kh in range(SWA_KV_HEADS):
        cols = [slice((kh * group // 2 + p) * LANES, (kh * group // 2 + p + 1) * LANES)
                for p in range(group // 2)]
        sink = jnp.concatenate([jnp.full((w, LANES), sinks_ref[kh * group + g], F32)
                                for g in range(group)], axis=0)
        bias_later = jnp.concatenate([jnp.where(sink_slot, sink, bias_prev), bias_cur], axis=-1)
        bias_first = jnp.concatenate(
            [jnp.where(sink_slot, sink, jnp.where(step > 0, bias_prev, NEG)), bias_cur], axis=-1)
        for t in range(SWA_ROWS // w):
            rows = slice(t * w, (t + 1) * w)
            bias = bias_first if t == 0 else bias_later
            k_w = jnp.where(first_row, zero_kv, k2[kh][t * w:(t + 2) * w])
            v_w = jnp.where(first_row, zero_kv, v2[kh][t * w:(t + 2) * w])
            parts = []
            for c in cols:
                qc = q_ref[rows, c]
                parts += [jnp.where(low, qc, zero), jnp.where(low, zero, qc)]
            s = _dot_nt(jnp.concatenate(parts, axis=0), k_w) + bias
            e = jnp.exp(s - jnp.max(s, axis=-1, keepdims=True))
            pv = _dot(e.astype(BF16), jnp.concatenate([v_w, ones], axis=-1))
            o = pv[:, :LANES] / pv[:, LANES:]
            for p, c in enumerate(cols):
                o_ref[rows, c] = jnp.where(low, o[2 * p * w:(2 * p + 1) * w],
                                           o[(2 * p + 1) * w:(2 * p + 2) * w]).astype(o_ref.dtype)


def _swa(proj, sinks):
    w, rows = WINDOW, SWA_ROWS
    assert 2 * SWA_HD == LANES and WINDOW == LANES and (SWA_HEADS // SWA_KV_HEADS) % 2 == 0
    nq, nkv = SWA_HEADS * SWA_HD, SWA_KV_HEADS * SWA_HD
    per = rows // w
    cur = lambda split: pl.BlockSpec((rows, nkv), lambda i, c=IN_OFFSETS[split] // nkv: (i, c))
    prev = lambda split: pl.BlockSpec(
        (w, nkv), lambda i, c=IN_OFFSETS[split] // nkv: (jnp.maximum(i * per - 1, 0), c))
    return pl.pallas_call(
        _swa_kernel,
        grid=(SEQ // rows,),
        in_specs=[pl.BlockSpec(memory_space=pltpu.SMEM),
                  pl.BlockSpec((rows, nq), lambda i: (i, IN_OFFSETS[Q_S] // nq)),
                  cur(K_S), cur(V_S), prev(K_S), prev(V_S)],
        out_specs=pl.BlockSpec((rows, nq), lambda i: (i, 0)),
        out_shape=jax.ShapeDtypeStruct((SEQ, nq), BF16),
        compiler_params=_params("parallel"),
        name="swa",
    )(sinks, proj, proj, proj, proj, proj)


def _mem_kv_kernel(mem_ref, g_ref, w_ref, o_ref):
    memn = (_rms(mem_ref[...]) * g_ref[...]).astype(BF16)
    o_ref[...] = _dot(memn, w_ref[...]).astype(o_ref.dtype)


def _mem_kv(mem2d, g, w):
    n_out = 2 * X_HEADS * X_HD
    return pl.pallas_call(
        _mem_kv_kernel,
        out_shape=jax.ShapeDtypeStruct((N_MEM, n_out), BF16),
        compiler_params=pltpu.CompilerParams(vmem_limit_bytes=VMEM_LIMIT),
        name="mem_kv",
    )(mem2d, g.reshape(1, D_MODEL), w)


def _merge_kernel(x_ref, yr_ref, ys_ref, *refs):
    n_gate = D_MODEL // GATE_BLOCK
    gr_refs, gs_refs = refs[:n_gate], refs[n_gate:2 * n_gate]
    (kv_ref, wur_ref, wus_ref, wo_ref, gx_ref, wxq_ref, wxo_ref, gf_ref, h_ref, n_ref) = refs[2 * n_gate:]
    gate_r = jnp.concatenate([r[...] for r in gr_refs], axis=-1).astype(F32)
    gate_s = jnp.concatenate([r[...] for r in gs_refs], axis=-1).astype(F32)
    up_r = _dot(yr_ref[...], wur_ref[...])
    up_s = _dot(ys_ref[...], wus_ref[...])
    merged = _sigmoid(gate_r) * up_r + _sigmoid(gate_s) * up_s
    h1 = x_ref[...] + _dot(merged.astype(BF16), wo_ref[...])

    nx = (_rms(h1) * gx_ref[...]).astype(BF16)
    qx = _dot(nx, wxq_ref[...]).astype(BF16)
    kv = kv_ref[...]
    outs = []
    for h in range(X_HEADS):
        q_h = qx[:, h * X_HD:(h + 1) * X_HD]
        k_h = kv[:, h * X_HD:(h + 1) * X_HD]
        v_h = kv[:, (X_HEADS + h) * X_HD:(X_HEADS + h + 1) * X_HD]
        s = _dot_nt(q_h, k_h) * (X_HD ** -0.5)
        e = jnp.exp(s - jnp.max(s, axis=-1, keepdims=True))
        p = e / jnp.sum(e, axis=-1, keepdims=True)
        outs.append(_dot(p.astype(BF16), v_h))
    att = jnp.concatenate(outs, axis=-1).astype(BF16)
    h2 = h1 + _dot(att, wxo_ref[...])
    h_ref[...] = h2
    n_ref[...] = (_rms(h2) * gf_ref[...]).astype(BF16)


def _merge(x2d, yr, ys, proj, kv, w_up_ret, w_up_swa, w_o, g_x, w_xq, w_xo, g_ffn):
    tm = MERGE_TM
    rows = lambda width, col=0: pl.BlockSpec((tm, width), lambda i, c=col: (i, c))
    n_gate = D_MODEL // GATE_BLOCK
    gate = lambda split: [rows(GATE_BLOCK, IN_OFFSETS[split] // GATE_BLOCK + b) for b in range(n_gate)]
    assert IN_OFFSETS[GATE_R] % GATE_BLOCK == 0 and IN_OFFSETS[GATE_S] % GATE_BLOCK == 0
    n_x = X_HEADS * X_HD
    return pl.pallas_call(
        _merge_kernel,
        grid=(SEQ // tm,),
        in_specs=[rows(D_MODEL), rows(RET_HEADS * RET_DV), rows(SWA_HEADS * SWA_HD),
                  *gate(GATE_R), *gate(GATE_S),
                  _resident((N_MEM, 2 * n_x)),
                  _resident((RET_HEADS * RET_DV, D_MODEL)), _resident((SWA_HEADS * SWA_HD, D_MODEL)),
                  _resident((D_MODEL, D_MODEL)), _resident((1, D_MODEL)),
                  _resident((D_MODEL, n_x)), _resident((n_x, D_MODEL)), _resident((1, D_MODEL))],
        out_specs=[rows(D_MODEL), rows(D_MODEL)],
        out_shape=[jax.ShapeDtypeStruct((SEQ, D_MODEL), F32),
                   jax.ShapeDtypeStruct((SEQ, D_MODEL), BF16)],
        compiler_params=_params("parallel"),
        name="merge_xattn",
    )(x2d, yr, ys, *([proj] * (2 * n_gate)), kv, w_up_ret, w_up_swa, w_o, g_x.reshape(1, D_MODEL),
      w_xq, w_xo, g_ffn.reshape(1, D_MODEL))


def _ffn_kernel(n_ref, h_hbm, wg_ref, wu_ref, wd_ref, gf_ref, o_ref, h_ref, h_sem):
    i, f = pl.program_id(0), pl.program_id(1)
    rows = pl.ds(pl.multiple_of(i * FFN_TM, FFN_TM), FFN_TM)
    residual_copy = pltpu.make_async_copy(h_hbm.at[rows, :], h_ref, h_sem)

    @pl.when(f == 0)
    def _():
        residual_copy.start()
        for c in range(FFN_TM // FFN_SLAB):
            o_ref[c * FFN_SLAB:(c + 1) * FFN_SLAB, :] = jnp.zeros((FFN_SLAB, D_MODEL), F32)

    n2 = n_ref[...]
    ts = []
    for c in range(FFN_TF // FFN_SLAB):
        cols = slice(c * FFN_SLAB, (c + 1) * FFN_SLAB)
        a = _dot(n2, wg_ref[:, cols])
        b = _dot(n2, wu_ref[:, cols])
        ts.append((a * _sigmoid(a) * b).astype(BF16))
    t = jnp.concatenate(ts, axis=-1)
    for c in range(D_MODEL // FFN_TF):
        cols = slice(c * FFN_TF, (c + 1) * FFN_TF)
        o_ref[:, cols] += _dot(t, wd_ref[:, cols])

    @pl.when(f == pl.num_programs(1) - 1)
    def _():
        residual_copy.wait()
        for c in range(FFN_TM // FFN_SLAB):
            r = slice(c * FFN_SLAB, (c + 1) * FFN_SLAB)
            o_ref[r, :] = _rms(h_ref[r, :] + o_ref[r, :]) * gf_ref[...]


def _ffn(n2, h2, w_gate, w_up, w_down, g_final):
    tm, tf = FFN_TM, FFN_TF
    return pl.pallas_call(
        _ffn_kernel,
        grid=(SEQ // tm, D_FF // tf),
        in_specs=[pl.BlockSpec((tm, D_MODEL), lambda i, f: (i, 0)),
                  pl.BlockSpec(memory_space=pl.ANY),
                  pl.BlockSpec((D_MODEL, tf), lambda i, f: (0, f)),
                  pl.BlockSpec((D_MODEL, tf), lambda i, f: (0, f)),
                  pl.BlockSpec((tf, D_MODEL), lambda i, f: (f, 0)),
                  pl.BlockSpec((1, D_MODEL), lambda i, f: (0, 0))],
        out_specs=pl.BlockSpec((tm, D_MODEL), lambda i, f: (i, 0)),
        out_shape=jax.ShapeDtypeStruct((SEQ, D_MODEL), F32),
        scratch_shapes=[pltpu.VMEM((tm, D_MODEL), F32), pltpu.SemaphoreType.DMA(())],
        compiler_params=_params("arbitrary", "arbitrary"),
        name="ffn",
    )(n2, h2, w_gate, w_up, w_down, g_final.reshape(1, D_MODEL))


def kernel(x, mem, positions, g_mix, w_in, w_up_ret, w_up_swa, sinks, w_o, g_x, g_mem,
           w_xq, w_xkv, w_xo, g_ffn, w_ffn_gate, w_ffn_up, w_ffn_down, g_final):
    assert x.shape == (1, SEQ, D_MODEL) and mem.shape == (1, N_MEM, D_MODEL)
    assert w_in.shape == (1, D_MODEL, D_IN)
    x2d = x.reshape(SEQ, D_MODEL)
    tables, (w_q, w_xout, w_kv) = _rope_tables(positions, (w_xq[0], w_xo[0], w_xkv[0]))
    later_weights = (w_ffn_gate, w_ffn_up, w_ffn_down, w_up_ret, w_up_swa, w_o)
    proj, *bf16_weights = _in_proj(x2d, g_mix[0], w_in[0], tables, tuple(w[0] for w in later_weights))
    w_gate, w_up, w_down, w_ur, w_us, w_out = bf16_weights
    yr = _retention(proj)
    ys = _swa(proj, sinks[0])
    kv = _mem_kv(mem.reshape(N_MEM, D_MODEL), g_mem[0], w_kv)
    h2, n2 = _merge(x2d, yr, ys, proj, kv, w_ur, w_us, w_out, g_x[0], w_q, w_xout, g_ffn[0])
    out = _ffn(n2, h2, w_gate, w_up, w_down, g_final)
    return out.reshape(1, SEQ, D_MODEL)
```

```python
import jax
import jax.numpy as jnp
import numpy as np
from jax import lax
from jax.experimental import pallas as pl
from jax.experimental.pallas import tpu as pltpu

F32 = jnp.float32
BF16 = jnp.bfloat16

D_MODEL = 2048
SEQ = 16384
N_MEM = 256
EPS = 1e-6
RET_HEADS = 4
RET_DK = 256
RET_DV = 256
RET_THETA = 10000.0
SWA_HEADS = 16
SWA_KV_HEADS = 4
SWA_HD = 64
WINDOW = 128
ROPE_THETA = 500000.0
ROPE_DIM = SWA_HD // 4
X_HEADS = 4
X_HD = 128
D_FF = -(-8 * D_MODEL // (3 * 256)) * 256
IN_SPLITS = (RET_HEADS * RET_DK, RET_HEADS * RET_DK, RET_HEADS * RET_DV, RET_HEADS * RET_DV,
             SWA_HEADS * SWA_HD, SWA_KV_HEADS * SWA_HD, SWA_KV_HEADS * SWA_HD, D_MODEL, D_MODEL)
D_IN = sum(IN_SPLITS)
IN_OFFSETS = tuple(int(o) for o in np.cumsum((0,) + IN_SPLITS[:-1]))
Q_R, K_R, V_R, G_R, Q_S, K_S, V_S, GATE_R, GATE_S = range(9)
GATE_BLOCK = 512
NEG = -1e30

LANES = 128
BF16_SUBLANES = 16
MXU_WIDTH = 256
VMEM_LIMIT = 58 * 1024 * 1024

RET_CHUNK = 256
RET_ROWS = 2048
ROPE_ROWS = 2048
PROJ_TM, PROJ_TN = 2048, 512
NORM_SLAB = 256
ROPE_SLAB = 512
CAST_STEPS_PER_ROW = 16
CAST_ROW_BLOCKS, CAST_COL_BLOCKS = 32, 4
SWA_ROWS = 1024
MERGE_TM = 256
FFN_TM, FFN_TF = 1024, 512
FFN_SLAB = MXU_WIDTH


def _sigmoid(x):
    return 1.0 / (1.0 + jnp.exp(-x))


def _rms(x):
    return x * lax.rsqrt(jnp.mean(x * x, axis=-1, keepdims=True) + EPS)


def _dot(a, b):
    return jnp.dot(a, b, preferred_element_type=F32)


def _dot_nt(a, b):
    return lax.dot_general(a, b, (((1,), (1,)), ((), ())), preferred_element_type=F32)


def _dot_tn(a, b):
    return lax.dot_general(a, b, (((0,), (0,)), ((), ())), preferred_element_type=F32)


def _params(*sem):
    return pltpu.CompilerParams(dimension_semantics=sem, vmem_limit_bytes=VMEM_LIMIT)


def _resident(shape):
    nd = len(shape)
    return pl.BlockSpec(shape, lambda *_: (0,) * nd, pipeline_mode=pl.Buffered(1))


def _rope_tables_kernel(pos_ref, invr_ref, invs_ref, sign_ref, spread_ref, *refs):
    n_cast = (len(refs) - 4) // 2
    cast_in, (rc_ref, rs_ref, sc_ref, ss_ref), cast_out = refs[:n_cast], refs[n_cast:n_cast + 4], refs[n_cast + 4:]
    for src, dst in zip(cast_in, cast_out):
        dst[...] = src[...].astype(BF16)

    pos = pos_ref[...].astype(F32)
    ang_r = pos * invr_ref[...]
    rc_ref[...] = jnp.cos(ang_r)
    rs_ref[...] = jnp.sin(ang_r)

    pack = LANES // (ROPE_DIM // 2)
    n_packed = ROPE_ROWS // pack
    ang = jnp.zeros((n_packed, LANES), F32)
    for k in range(pack):
        pos_k = pos_ref[pl.ds(k, n_packed, stride=pack), :].astype(F32)
        ang = ang + pos_k * invs_ref[k:k + 1, :]
    cos_p, sin_p = jnp.cos(ang), jnp.sin(ang)

    def spread(packed, k):
        total = jnp.zeros(packed.shape, F32)
        rest = packed
        for _ in range(3):
            term = rest.astype(BF16)
            total = total + _dot(term, spread_ref[k])
            rest = rest - term.astype(F32)
        return total

    rotated = sign_ref[...] != 0.0
    for k in range(pack):
        rows = pl.ds(k, n_packed, stride=pack)
        sc_ref[rows, :] = jnp.where(rotated, spread(cos_p, k), 1.0)
        ss_ref[rows, :] = spread(sin_p, k) * sign_ref[...]


def _rope_tables(positions, cast_weights=()):
    pos = positions.reshape(SEQ, 1)
    steps = SEQ // ROPE_ROWS
    assert all(cw.shape[0] % (steps * BF16_SUBLANES) == 0 for cw in cast_weights)
    slabs = [pl.BlockSpec((cw.shape[0] // steps, cw.shape[1]), lambda i: (i, 0)) for cw in cast_weights]
    half_r = RET_DK // 2
    inv_r = 1.0 / (RET_THETA ** (jnp.arange(half_r, dtype=F32) / half_r))
    half_s = ROPE_DIM // 2
    inv_s = 1.0 / (ROPE_THETA ** (jnp.arange(half_s, dtype=F32) / half_s))
    d = np.arange(LANES) % SWA_HD
    sign = jnp.asarray(np.where(d < half_s, -1.0, np.where(d < ROPE_DIM, 1.0, 0.0)).astype(np.float32))
    pack = LANES // half_s
    assert ROPE_ROWS % (pack * 8) == 0
    lane = np.arange(LANES)
    inv_packed = jnp.where(lane[None, :] // half_s == np.arange(pack)[:, None],
                           inv_s[lane % half_s][None, :], 0.0).astype(F32)
    spread = np.zeros((pack, LANES, LANES), np.float32)
    for k in range(pack):
        for out_lane in np.nonzero(d < ROPE_DIM)[0]:
            spread[k, k * half_s + d[out_lane] % half_s, out_lane] = 1.0
    row = lambda v: v.reshape(1, LANES)
    tab = jax.ShapeDtypeStruct((SEQ, LANES), F32)
    vec = pl.BlockSpec((1, LANES), lambda i: (0, 0))
    blk = pl.BlockSpec((ROPE_ROWS, LANES), lambda i: (i, 0))
    outs = pl.pallas_call(
        _rope_tables_kernel,
        grid=(steps,),
        in_specs=[pl.BlockSpec((ROPE_ROWS, 1), lambda i: (i, 0)), vec,
                  pl.BlockSpec((pack, LANES), lambda i: (0, 0)), vec,
                  pl.BlockSpec((pack, LANES, LANES), lambda i: (0, 0, 0)), *slabs],
        out_specs=[blk] * 4 + slabs,
        out_shape=[tab] * 4 + [jax.ShapeDtypeStruct(cw.shape, BF16) for cw in cast_weights],
        compiler_params=_params("parallel"),
        name="rope_tables",
    )(pos, row(inv_r), inv_packed, row(sign), jnp.asarray(spread, BF16), *cast_weights)
    return outs[:4], outs[4:]


def _in_proj_kernel(x_hbm, g_ref, w_ref, rc_ref, rs_ref, sc_ref, ss_ref, perm_ref, *refs):
    tm, tn = PROJ_TM, PROJ_TN
    i, j = pl.program_id(0), pl.program_id(1)
    blk = lambda split: IN_OFFSETS[split] // tn
    *io_refs, n_ref, acc_ref, x_buf, x_sem = refs
    n_cast = len(io_refs) // 2
    cast_in, o_ref, cast_out = io_refs[:n_cast], io_refs[n_cast], io_refs[n_cast + 1:]
    for src, dst in zip(cast_in, cast_out):
        dst[...] = src[...].astype(BF16)

    def x_copy(block):
        rows = pl.ds(pl.multiple_of(block * tm, tm), tm)
        return pltpu.make_async_copy(x_hbm.at[rows, :], x_buf, x_sem)

    @pl.when(j == 0)
    def _():
        @pl.when(i == 0)
        def _():
            x_copy(0).start()

        x_copy(i).wait()
        for c in range(tm // NORM_SLAB):
            r = slice(c * NORM_SLAB, (c + 1) * NORM_SLAB)
            n_ref[r, :] = (_rms(x_buf[r, :]) * g_ref[...]).astype(BF16)

    @pl.when((j == 1) & (i + 1 < pl.num_programs(0)))
    def _():
        x_copy(i + 1).start()

    acc_ref[...] = _dot(n_ref[...], w_ref[...].astype(BF16))
    o_ref[...] = acc_ref[...].astype(BF16)
    is_ret = (j >= blk(Q_R)) & (j < blk(V_R))
    is_qs = (j >= blk(Q_S)) & (j < blk(K_S))
    is_kvs = j == blk(K_S)

    def swa_rope(pair, scale):
        cols = slice(pair * MXU_WIDTH, (pair + 1) * MXU_WIDTH)
        for c in range(tm // ROPE_SLAB):
            r = slice(c * ROPE_SLAB, (c + 1) * ROPE_SLAB)
            xt = acc_ref[r, cols]
            hi = xt.astype(BF16)
            lo = (xt - hi.astype(F32)).astype(BF16)
            partner = _dot(hi, perm_ref[...]) + _dot(lo, perm_ref[...])
            cos = jnp.concatenate([sc_ref[r, :]] * (MXU_WIDTH // LANES), axis=-1)
            sin = jnp.concatenate([ss_ref[r, :]] * (MXU_WIDTH // LANES), axis=-1)
            o_ref[r, cols] = ((xt * cos + partner * sin) * scale).astype(BF16)

    @pl.when(is_ret)
    def _():
        scale = jnp.where(j >= blk(K_R), RET_DK ** -0.5, 1.0).astype(F32)
        cos, sin = rc_ref[...] * scale, rs_ref[...] * scale
        half = RET_DK // 2
        for h in range(tn // RET_DK):
            a1 = acc_ref[:, h * RET_DK:h * RET_DK + half]
            a2 = acc_ref[:, h * RET_DK + half:(h + 1) * RET_DK]
            o_ref[:, h * RET_DK:h * RET_DK + half] = (a1 * cos - a2 * sin).astype(BF16)
            o_ref[:, h * RET_DK + half:(h + 1) * RET_DK] = (a2 * cos + a1 * sin).astype(BF16)

    @pl.when(is_qs)
    def _():
        for pair in range(tn // MXU_WIDTH):
            swa_rope(pair, SWA_HD ** -0.5)

    @pl.when(is_kvs)
    def _():
        for pair in range(IN_SPLITS[K_S] // MXU_WIDTH):
            swa_rope(pair, 1.0)


def _rope_partner_matrix():
    half = ROPE_DIM // 2
    p = np.zeros((MXU_WIDTH, MXU_WIDTH), np.float32)
    for lane in range(MXU_WIDTH):
        d = lane % SWA_HD
        if d < half:
            p[lane + half, lane] = 1.0
        elif d < ROPE_DIM:
            p[lane - half, lane] = 1.0
    return jnp.asarray(p, BF16)


def _in_proj(x2d, g, w, tables, cast_weights):
    tm, tn = PROJ_TM, PROJ_TN
    assert all(IN_OFFSETS[s] % tn == 0 for s in (Q_R, K_R, V_R, Q_S, K_S))
    assert IN_OFFSETS[V_S] == IN_OFFSETS[K_S] + IN_SPLITS[K_S] and IN_SPLITS[K_S] + IN_SPLITS[V_S] == tn
    tab = pl.BlockSpec((tm, LANES), lambda i, j: (i, 0))
    n_i, n_j = SEQ // tm, D_IN // tn
    assert n_i * CAST_STEPS_PER_ROW == CAST_ROW_BLOCKS * CAST_COL_BLOCKS and CAST_STEPS_PER_ROW <= n_j

    def cast_spec(shape):
        rows, cols = shape[0] // CAST_ROW_BLOCKS, shape[1] // CAST_COL_BLOCKS
        assert rows % BF16_SUBLANES == 0 and cols % LANES == 0 and rows * CAST_ROW_BLOCKS == shape[0]

        def index(i, j):
            b = i * CAST_STEPS_PER_ROW + jnp.minimum(j, CAST_STEPS_PER_ROW - 1)
            return b // CAST_COL_BLOCKS, b % CAST_COL_BLOCKS
        return pl.BlockSpec((rows, cols), index)

    cast_specs = [cast_spec(cw.shape) for cw in cast_weights]
    return pl.pallas_call(
        _in_proj_kernel,
        grid=(SEQ // tm, D_IN // tn),
        in_specs=[pl.BlockSpec(memory_space=pl.ANY),
                  pl.BlockSpec((1, D_MODEL), lambda i, j: (0, 0)),
                  pl.BlockSpec((D_MODEL, tn), lambda i, j: (0, j)),
                  tab, tab, tab, tab, _resident((MXU_WIDTH, MXU_WIDTH)), *cast_specs],
        out_specs=[pl.BlockSpec((tm, tn), lambda i, j: (i, j)), *cast_specs],
        out_shape=[jax.ShapeDtypeStruct((SEQ, D_IN), BF16)]
                  + [jax.ShapeDtypeStruct(cw.shape, BF16) for cw in cast_weights],
        scratch_shapes=[pltpu.VMEM((tm, D_MODEL), BF16), pltpu.VMEM((tm, tn), F32),
                        pltpu.VMEM((tm, D_MODEL), F32), pltpu.SemaphoreType.DMA(())],
        compiler_params=_params("arbitrary", "arbitrary"),
        name="in_proj",
    )(x2d, g.reshape(1, D_MODEL), w, *tables, _rope_partner_matrix(), *cast_weights)


def _retention_kernel(logg_ref, q_ref, k_ref, v_ref, g_ref, o_ref, state_ref, dmask_ref):
    c_rows = RET_CHUNK
    idx = lax.broadcasted_iota(jnp.int32, (c_rows, 1), 0).astype(F32)

    @pl.when(pl.program_id(0) == 0)
    def _():
        state_ref[...] = jnp.zeros_like(state_ref)
        i = lax.broadcasted_iota(jnp.int32, (c_rows, c_rows), 0)
        j = lax.broadcasted_iota(jnp.int32, (c_rows, c_rows), 1)
        rel = (i - j).astype(F32)
        for h in range(RET_HEADS):
            dmask_ref[h] = jnp.where(rel >= 0, jnp.exp(logg_ref[h] * jnp.maximum(rel, 0.0)), 0.0)

    for h in range(RET_HEADS):
        lg = logg_ref[h]
        cols = slice(h * RET_DK, (h + 1) * RET_DK)
        q_dec = jnp.exp(lg * (idx + 1.0))
        k_dec = jnp.exp(lg * (c_rows - 1.0 - idx))
        chunk_decay = jnp.exp(jnp.full((1, RET_DV), lg * c_rows, F32))
        for c in range(RET_ROWS // c_rows):
            r = slice(c * c_rows, (c + 1) * c_rows)
            q, k, v = q_ref[r, cols], k_ref[r, cols], v_ref[r, cols]
            scores = _dot_nt(q, k) * dmask_ref[h]
            inner = _dot(scores.astype(BF16), v)
            state = state_ref[h]
            cross = _dot((q.astype(F32) * q_dec).astype(BF16), state.astype(BF16))
            state_ref[h] = state * chunk_decay + _dot_tn((k.astype(F32) * k_dec).astype(BF16), v)
            y = _rms(inner + cross)
            g = g_ref[r, cols].astype(F32)
            o_ref[r, cols] = (y * (g * _sigmoid(g))).astype(o_ref.dtype)


def _retention(proj):
    log_g = jnp.log(1.0 - jnp.power(2.0, -5.0 - jnp.arange(RET_HEADS, dtype=F32)))
    rows, width = RET_ROWS, RET_HEADS * RET_DK
    assert RET_DK == RET_DV
    all_heads = lambda split: pl.BlockSpec((rows, width), lambda n, c=IN_OFFSETS[split] // width: (n, c))
    assert all(IN_OFFSETS[s] % width == 0 for s in (Q_R, K_R, V_R, G_R))
    return pl.pallas_call(
        _retention_kernel,
        grid=(SEQ // rows,),
        in_specs=[pl.BlockSpec(memory_space=pltpu.SMEM),
                  all_heads(Q_R), all_heads(K_R), all_heads(V_R), all_heads(G_R)],
        out_specs=pl.BlockSpec((rows, width), lambda n: (n, 0)),
        out_shape=jax.ShapeDtypeStruct((SEQ, width), BF16),
        scratch_shapes=[pltpu.VMEM((RET_HEADS, RET_DK, RET_DV), F32),
                        pltpu.VMEM((RET_HEADS, RET_CHUNK, RET_CHUNK), F32)],
        compiler_params=_params("arbitrary"),
        name="retention",
    )(log_g, proj, proj, proj, proj)


def _swa_kernel(sinks_ref, q_ref, k_ref, v_ref, kp_ref, vp_ref, o_ref):
    w = WINDOW
    hd = SWA_HD
    group = SWA_HEADS // SWA_KV_HEADS
    step = pl.program_id(0)
    low = lax.broadcasted_iota(jnp.int32, (w, LANES), 1) < hd

    def dup_halves(x):
        xf = x.astype(F32)
        sw = pltpu.roll(xf, hd, 1)
        lo = lax.broadcasted_iota(jnp.int32, xf.shape, 1) < hd
        return jnp.where(lo, xf, sw).astype(BF16), jnp.where(lo, sw, xf).astype(BF16)

    def per_kv_head(cur_ref, prev_ref):
        allrows = jnp.concatenate([prev_ref[...], cur_ref[...]], axis=0)
        heads = []
        for c in range(allrows.shape[1] // LANES):
            heads.extend(dup_halves(allrows[:, c * LANES:(c + 1) * LANES]))
        return heads

    k2 = per_kv_head(k_ref, kp_ref)
    v2 = per_kv_head(v_ref, vp_ref)

    qi = lax.broadcasted_iota(jnp.int32, (w, 2 * w), 0)
    kj = lax.broadcasted_iota(jnp.int32, (w, 2 * w), 1)
    dist = qi + w - kj
    band = (dist >= 0) & (dist < WINDOW)
    zero = jnp.zeros((w, LANES), BF16)
    ones = jnp.ones((2 * w, LANES), BF16)
    zero_kv = jnp.zeros((2 * w, LANES), BF16)

    tile = lambda a: jnp.concatenate([a] * group, axis=0)
    bias_cur = tile(jnp.where(band[:, w:], 0.0, NEG))
    bias_prev = tile(jnp.where(band[:, :w], 0.0, NEG))
    sink_slot = lax.broadcasted_iota(jnp.int32, (group * w, LANES), 1) == 0
    first_row = lax.broadcasted_iota(jnp.int32, (2 * w, LANES), 0) == 0

    for kh in range(SWA_KV_HEADS):
        cols = [slice((kh * group // 2 + p) * LANES, (kh * group // 2 + p + 1) * LANES)
                for p in range(group // 2)]
        sink = jnp.concatenate([jnp.full((w, LANES), sinks_ref[kh * group + g], F32)
                                for g in range(group)], axis=0)
        bias_later = jnp.concatenate([jnp.where(sink_slot, sink, bias_prev), bias_cur], axis=-1)
        bias_first = jnp.concatenate(
            [jnp.where(sink_slot, sink, jnp.where(step > 0, bias_prev, NEG)), bias_cur], axis=-1)
        for t in range(SWA_ROWS // w):
            rows = slice(t * w, (t + 1) * w)
            bias = bias_first if t == 0 else bias_later
            k_w = jnp.where(first_row, zero_kv, k2[kh][t * w:(t + 2) * w])
            v_w = jnp.where(first_row, zero_kv, v2[kh][t * w:(t + 2) * w])
            parts = []
            for c in cols:
                qc = q_ref[rows, c]
                parts += [jnp.where(low, qc, zero), jnp.where(low, zero, qc)]
            s = _dot_nt(jnp.concatenate(parts, axis=0), k_w) + bias
            e = jnp.exp(s - jnp.max(s, axis=-1, keepdims=True))
            pv = _dot(e.astype(BF16), jnp.concatenate([v_w, ones], axis=-1))
            o = pv[:, :LANES] / pv[:, LANES:]
            for p, c in enumerate(cols):
                o_ref[rows, c] = jnp.where(low, o[2 * p * w:(2 * p + 1) * w],
                                           o[(2 * p + 1) * w:(2 * p + 2) * w]).astype(o_ref.dtype)


def _swa(proj, sinks):
    w, rows = WINDOW, SWA_ROWS
    assert 2 * SWA_HD == LANES and WINDOW == LANES and (SWA_HEADS // SWA_KV_HEADS) % 2 == 0
    nq, nkv = SWA_HEADS * SWA_HD, SWA_KV_HEADS * SWA_HD
    per = rows // w
    cur = lambda split: pl.BlockSpec((rows, nkv), lambda i, c=IN_OFFSETS[split] // nkv: (i, c))
    prev = lambda split: pl.BlockSpec(
        (w, nkv), lambda i, c=IN_OFFSETS[split] // nkv: (jnp.maximum(i * per - 1, 0), c))
    return pl.pallas_call(
        _swa_kernel,
        grid=(SEQ // rows,),
        in_specs=[pl.BlockSpec(memory_space=pltpu.SMEM),
                  pl.BlockSpec((rows, nq), lambda i: (i, IN_OFFSETS[Q_S] // nq)),
                  cur(K_S), cur(V_S), prev(K_S), prev(V_S)],
        out_specs=pl.BlockSpec((rows, nq), lambda i: (i, 0)),
        out_shape=jax.ShapeDtypeStruct((SEQ, nq), BF16),
        compiler_params=_params("parallel"),
        name="swa",
    )(sinks, proj, proj, proj, proj, proj)


def _mem_kv_kernel(mem_ref, g_ref, w_ref, o_ref):
    memn = (_rms(mem_ref[...]) * g_ref[...]).astype(BF16)
    o_ref[...] = _dot(memn, w_ref[...]).astype(o_ref.dtype)


def _mem_kv(mem2d, g, w):
    n_out = 2 * X_HEADS * X_HD
    return pl.pallas_call(
        _mem_kv_kernel,
        out_shape=jax.ShapeDtypeStruct((N_MEM, n_out), BF16),
        compiler_params=pltpu.CompilerParams(vmem_limit_bytes=VMEM_LIMIT),
        name="mem_kv",
    )(mem2d, g.reshape(1, D_MODEL), w)


def _merge_kernel(x_ref, yr_ref, ys_ref, *refs):
    n_gate = D_MODEL // GATE_BLOCK
    gr_refs, gs_refs = refs[:n_gate], refs[n_gate:2 * n_gate]
    (kv_ref, wur_ref, wus_ref, wo_ref, gx_ref, wxq_ref, wxo_ref, gf_ref, h_ref, n_ref) = refs[2 * n_gate:]
    gate_r = jnp.concatenate([r[...] for r in gr_refs], axis=-1).astype(F32)
    gate_s = jnp.concatenate([r[...] for r in gs_refs], axis=-1).astype(F32)
    up_r = _dot(yr_ref[...], wur_ref[...])
    up_s = _dot(ys_ref[...], wus_ref[...])
    merged = _sigmoid(gate_r) * up_r + _sigmoid(gate_s) * up_s
    h1 = x_ref[...] + _dot(merged.astype(BF16), wo_ref[...])

    nx = (_rms(h1) * gx_ref[...]).astype(BF16)
    qx = _dot(nx, wxq_ref[...]).astype(BF16)
    kv = kv_ref[...]
    outs = []
    for h in range(X_HEADS):
        q_h = qx[:, h * X_HD:(h + 1) * X_HD]
        k_h = kv[:, h * X_HD:(h + 1) * X_HD]
        v_h = kv[:, (X_HEADS + h) * X_HD:(X_HEADS + h + 1) * X_HD]
        s = _dot_nt(q_h, k_h) * (X_HD ** -0.5)
        e = jnp.exp(s - jnp.max(s, axis=-1, keepdims=True))
        p = e / jnp.sum(e, axis=-1, keepdims=True)
        outs.append(_dot(p.astype(BF16), v_h))
    att = jnp.concatenate(outs, axis=-1).astype(BF16)
    h2 = h1 + _dot(att, wxo_ref[...])
    h_ref[...] = h2
    n_ref[...] = (_rms(h2) * gf_ref[...]).astype(BF16)


def _merge(x2d, yr, ys, proj, kv, w_up_ret, w_up_swa, w_o, g_x, w_xq, w_xo, g_ffn):
    tm = MERGE_TM
    rows = lambda width, col=0: pl.BlockSpec((tm, width), lambda i, c=col: (i, c))
    n_gate = D_MODEL // GATE_BLOCK
    gate = lambda split: [rows(GATE_BLOCK, IN_OFFSETS[split] // GATE_BLOCK + b) for b in range(n_gate)]
    assert IN_OFFSETS[GATE_R] % GATE_BLOCK == 0 and IN_OFFSETS[GATE_S] % GATE_BLOCK == 0
    n_x = X_HEADS * X_HD
    return pl.pallas_call(
        _merge_kernel,
        grid=(SEQ // tm,),
        in_specs=[rows(D_MODEL), rows(RET_HEADS * RET_DV), rows(SWA_HEADS * SWA_HD),
                  *gate(GATE_R), *gate(GATE_S),
                  _resident((N_MEM, 2 * n_x)),
                  _resident((RET_HEADS * RET_DV, D_MODEL)), _resident((SWA_HEADS * SWA_HD, D_MODEL)),
                  _resident((D_MODEL, D_MODEL)), _resident((1, D_MODEL)),
                  _resident((D_MODEL, n_x)), _resident((n_x, D_MODEL)), _resident((1, D_MODEL))],
        out_specs=[rows(D_MODEL), rows(D_MODEL)],
        out_shape=[jax.ShapeDtypeStruct((SEQ, D_MODEL), F32),
                   jax.ShapeDtypeStruct((SEQ, D_MODEL), BF16)],
        compiler_params=_params("parallel"),
        name="merge_xattn",
    )(x2d, yr, ys, *([proj] * (2 * n_gate)), kv, w_up_ret, w_up_swa, w_o, g_x.reshape(1, D_MODEL),
      w_xq, w_xo, g_ffn.reshape(1, D_MODEL))


def _ffn_kernel(n_ref, h_hbm, wg_ref, wu_ref, wd_ref, gf_ref, o_ref, h_ref, h_sem):
    i, f = pl.program_id(0), pl.program_id(1)
    rows = pl.ds(pl.multiple_of(i * FFN_TM, FFN_TM), FFN_TM)
    residual_copy = pltpu.make_async_copy(h_hbm.at[rows, :], h_ref, h_sem)

    @pl.when(f == 0)
    def _():
        residual_copy.start()
        for c in range(FFN_TM // FFN_SLAB):
            o_ref[c * FFN_SLAB:(c + 1) * FFN_SLAB, :] = jnp.zeros((FFN_SLAB, D_MODEL), F32)

    n2 = n_ref[...]
    ts = []
    for c in range(FFN_TF // FFN_SLAB):
        cols = slice(c * FFN_SLAB, (c + 1) * FFN_SLAB)
        a = _dot(n2, wg_ref[:, cols])
        b = _dot(n2, wu_ref[:, cols])
        ts.append((a * _sigmoid(a) * b).astype(BF16))
    t = jnp.concatenate(ts, axis=-1)
    for c in range(D_MODEL // FFN_TF):
        cols = slice(c * FFN_TF, (c + 1) * FFN_TF)
        o_ref[:, cols] += _dot(t, wd_ref[:, cols])

    @pl.when(f == pl.num_programs(1) - 1)
    def _():
        residual_copy.wait()
        for c in range(FFN_TM // FFN_SLAB):
            r = slice(c * FFN_SLAB, (c + 1) * FFN_SLAB)
            o_ref[r, :] = _rms(h_ref[r, :] + o_ref[r, :]) * gf_ref[...]


def _ffn(n2, h2, w_gate, w_up, w_down, g_final):
    tm, tf = FFN_TM, FFN_TF
    return pl.pallas_call(
        _ffn_kernel,
        grid=(SEQ // tm, D_FF // tf),
        in_specs=[pl.BlockSpec((tm, D_MODEL), lambda i, f: (i, 0)),
                  pl.BlockSpec(memory_space=pl.ANY),
                  pl.BlockSpec((D_MODEL, tf), lambda i, f: (0, f)),
                  pl.BlockSpec((D_MODEL, tf), lambda i, f: (0, f)),
                  pl.BlockSpec((tf, D_MODEL), lambda i, f: (f, 0)),
                  pl.BlockSpec((1, D_MODEL), lambda i, f: (0, 0))],
        out_specs=pl.BlockSpec((tm, D_MODEL), lambda i, f: (i, 0)),
        out_shape=jax.ShapeDtypeStruct((SEQ, D_MODEL), F32),
        scratch_shapes=[pltpu.VMEM((tm, D_MODEL), F32), pltpu.SemaphoreType.DMA(())],
        compiler_params=_params("arbitrary", "arbitrary"),
        name="ffn",
    )(n2, h2, w_gate, w_up, w_down, g_final.reshape(1, D_MODEL))


def kernel(x, mem, positions, g_mix, w_in, w_up_ret, w_up_swa, sinks, w_o, g_x, g_mem,
           w_xq, w_xkv, w_xo, g_ffn, w_ffn_gate, w_ffn_up, w_ffn_down, g_final):
    assert x.shape == (1, SEQ, D_MODEL) and mem.shape == (1, N_MEM, D_MODEL)
    assert w_in.shape == (1, D_MODEL, D_IN)
    x2d = x.reshape(SEQ, D_MODEL)
    tables, (w_q, w_xout, w_kv) = _rope_tables(positions, (w_xq[0], w_xo[0], w_xkv[0]))
    later_weights = (w_ffn_gate, w_ffn_up, w_ffn_down, w_up_ret, w_up_swa, w_o)
    proj, *bf16_weights = _in_proj(x2d, g_mix[0], w_in[0], tables, tuple(w[0] for w in later_weights))
    w_gate, w_up, w_down, w_ur, w_us, w_out = bf16_weights
    yr = _retention(proj)
    ys = _swa(proj, sinks[0])
    kv = _mem_kv(mem.reshape(N_MEM, D_MODEL), g_mem[0], w_kv)
    h2, n2 = _merge(x2d, yr, ys, proj, kv, w_ur, w_us, w_out, g_x[0], w_q, w_xout, g_ffn[0])
    out = _ffn(n2, h2, w_gate, w_up, w_down, g_final)
    return out.reshape(1, SEQ, D_MODEL)
```

```python
import jax
import jax.numpy as jnp
import numpy as np
from jax import lax
from jax.experimental import pallas as pl
from jax.experimental.pallas import tpu as pltpu

F32 = jnp.float32
BF16 = jnp.bfloat16

D_MODEL = 2048
SEQ = 16384
N_MEM = 256
EPS = 1e-6
RET_HEADS = 4
RET_DK = 256
RET_DV = 256
RET_THETA = 10000.0
SWA_HEADS = 16
SWA_KV_HEADS = 4
SWA_HD = 64
WINDOW = 128
ROPE_THETA = 500000.0
ROPE_DIM = SWA_HD // 4
X_HEADS = 4
X_HD = 128
D_FF = -(-8 * D_MODEL // (3 * 256)) * 256
IN_SPLITS = (RET_HEADS * RET_DK, RET_HEADS * RET_DK, RET_HEADS * RET_DV, RET_HEADS * RET_DV,
             SWA_HEADS * SWA_HD, SWA_KV_HEADS * SWA_HD, SWA_KV_HEADS * SWA_HD, D_MODEL, D_MODEL)
D_IN = sum(IN_SPLITS)
IN_OFFSETS = tuple(int(o) for o in np.cumsum((0,) + IN_SPLITS[:-1]))
Q_R, K_R, V_R, G_R, Q_S, K_S, V_S, GATE_R, GATE_S = range(9)
GATE_BLOCK = 512
NEG = -1e30

LANES = 128
BF16_SUBLANES = 16
MXU_WIDTH = 256
VMEM_LIMIT = 58 * 1024 * 1024

RET_CHUNK = 256
RET_ROWS = 2048
ROPE_ROWS = 2048
PROJ_TM, PROJ_TN = 2048, 512
NORM_SLAB = 256
ROPE_SLAB = 512
CAST_STEPS_PER_ROW = 16
CAST_ROW_BLOCKS, CAST_COL_BLOCKS = 32, 4
SWA_ROWS = 1024
MERGE_TM = 256
FFN_TM, FFN_TF = 1024, 512
FFN_SLAB = MXU_WIDTH


def _sigmoid(x):
    return 1.0 / (1.0 + jnp.exp(-x))


def _rms(x):
    return x * lax.rsqrt(jnp.mean(x * x, axis=-1, keepdims=True) + EPS)


def _dot(a, b):
    return jnp.dot(a, b, preferred_element_type=F32)


def _dot_nt(a, b):
    return lax.dot_general(a, b, (((1,), (1,)), ((), ())), preferred_element_type=F32)


def _dot_tn(a, b):
    return lax.dot_general(a, b, (((0,), (0,)), ((), ())), preferred_element_type=F32)


def _params(*sem):
    return pltpu.CompilerParams(dimension_semantics=sem, vmem_limit_bytes=VMEM_LIMIT)


def _resident(shape):
    nd = len(shape)
    return pl.BlockSpec(shape, lambda *_: (0,) * nd, pipeline_mode=pl.Buffered(1))


def _rope_tables_kernel(pos_ref, invr_ref, invs_ref, sign_ref, spread_ref, *refs):
    n_cast = (len(refs) - 4) // 2
    cast_in, (rc_ref, rs_ref, sc_ref, ss_ref), cast_out = refs[:n_cast], refs[n_cast:n_cast + 4], refs[n_cast + 4:]
    for src, dst in zip(cast_in, cast_out):
        dst[...] = src[...].astype(BF16)

    pos = pos_ref[...].astype(F32)
    ang_r = pos * invr_ref[...]
    rc_ref[...] = jnp.cos(ang_r)
    rs_ref[...] = jnp.sin(ang_r)

    pack = LANES // (ROPE_DIM // 2)
    n_packed = ROPE_ROWS // pack
    ang = jnp.zeros((n_packed, LANES), F32)
    for k in range(pack):
        pos_k = pos_ref[pl.ds(k, n_packed, stride=pack), :].astype(F32)
        ang = ang + pos_k * invs_ref[k:k + 1, :]
    cos_p, sin_p = jnp.cos(ang), jnp.sin(ang)

    def spread(packed, k):
        total = jnp.zeros(packed.shape, F32)
        rest = packed
        for _ in range(3):
            term = rest.astype(BF16)
            total = total + _dot(term, spread_ref[k])
            rest = rest - term.astype(F32)
        return total

    rotated = sign_ref[...] != 0.0
    for k in range(pack):
        rows = pl.ds(k, n_packed, stride=pack)
        sc_ref[rows, :] = jnp.where(rotated, spread(cos_p, k), 1.0)
        ss_ref[rows, :] = spread(sin_p, k) * sign_ref[...]


def _rope_tables(positions, cast_weights=()):
    pos = positions.reshape(SEQ, 1)
    steps = SEQ // ROPE_ROWS
    assert all(cw.shape[0] % (steps * BF16_SUBLANES) == 0 for cw in cast_weights)
    slabs = [pl.BlockSpec((cw.shape[0] // steps, cw.shape[1]), lambda i: (i, 0)) for cw in cast_weights]
    half_r = RET_DK // 2
    inv_r = 1.0 / (RET_THETA ** (jnp.arange(half_r, dtype=F32) / half_r))
    half_s = ROPE_DIM // 2
    inv_s = 1.0 / (ROPE_THETA ** (jnp.arange(half_s, dtype=F32) / half_s))
    d = np.arange(LANES) % SWA_HD
    sign = jnp.asarray(np.where(d < half_s, -1.0, np.where(d < ROPE_DIM, 1.0, 0.0)).astype(np.float32))
    pack = LANES // half_s
    assert ROPE_ROWS % (pack * 8) == 0
    lane = np.arange(LANES)
    inv_packed = jnp.where(lane[None, :] // half_s == np.arange(pack)[:, None],
                           inv_s[lane % half_s][None, :], 0.0).astype(F32)
    spread = np.zeros((pack, LANES, LANES), np.float32)
    for k in range(pack):
        for out_lane in np.nonzero(d < ROPE_DIM)[0]:
            spread[k, k * half_s + d[out_lane] % half_s, out_lane] = 1.0
    row = lambda v: v.reshape(1, LANES)
    tab = jax.ShapeDtypeStruct((SEQ, LANES), F32)
    vec = pl.BlockSpec((1, LANES), lambda i: (0, 0))
    blk = pl.BlockSpec((ROPE_ROWS, LANES), lambda i: (i, 0))
    outs = pl.pallas_call(
        _rope_tables_kernel,
        grid=(steps,),
        in_specs=[pl.BlockSpec((ROPE_ROWS, 1), lambda i: (i, 0)), vec,
                  pl.BlockSpec((pack, LANES), lambda i: (0, 0)), vec,
                  pl.BlockSpec((pack, LANES, LANES), lambda i: (0, 0, 0)), *slabs],
        out_specs=[blk] * 4 + slabs,
        out_shape=[tab] * 4 + [jax.ShapeDtypeStruct(cw.shape, BF16) for cw in cast_weights],
        compiler_params=_params("parallel"),
        name="rope_tables",
    )(pos, row(inv_r), inv_packed, row(sign), jnp.asarray(spread, BF16), *cast_weights)
    return outs[:4], outs[4:]


def _in_proj_kernel(x_hbm, g_ref, w_ref, rc_ref, rs_ref, sc_ref, ss_ref, perm_ref, *refs):
    tm, tn = PROJ_TM, PROJ_TN
    i, j = pl.program_id(0), pl.program_id(1)
    blk = lambda split: IN_OFFSETS[split] // tn
    *io_refs, n_ref, acc_ref, x_buf, x_sem = refs
    n_cast = len(io_refs) // 2
    cast_in, o_ref, cast_out = io_refs[:n_cast], io_refs[n_cast], io_refs[n_cast + 1:]
    for src, dst in zip(cast_in, cast_out):
        dst[...] = src[...].astype(BF16)

    def x_copy(block):
        rows = pl.ds(pl.multiple_of(block * tm, tm), tm)
        return pltpu.make_async_copy(x_hbm.at[rows, :], x_buf, x_sem)

    @pl.when(j == 0)
    def _():
        @pl.when(i == 0)
        def _():
            x_copy(0).start()

        x_copy(i).wait()
        for c in range(tm // NORM_SLAB):
            r = slice(c * NORM_SLAB, (c + 1) * NORM_SLAB)
            n_ref[r, :] = (_rms(x_buf[r, :]) * g_ref[...]).astype(BF16)

    @pl.when((j == 1) & (i + 1 < pl.num_programs(0)))
    def _():
        x_copy(i + 1).start()

    acc_ref[...] = _dot(n_ref[...], w_ref[...].astype(BF16))
    o_ref[...] = acc_ref[...].astype(BF16)
    is_ret = (j >= blk(Q_R)) & (j < blk(V_R))
    is_qs = (j >= blk(Q_S)) & (j < blk(K_S))
    is_kvs = j == blk(K_S)

    def swa_rope(pair, scale):
        cols = slice(pair * MXU_WIDTH, (pair + 1) * MXU_WIDTH)
        for c in range(tm // ROPE_SLAB):
            r = slice(c * ROPE_SLAB, (c + 1) * ROPE_SLAB)
            xt = acc_ref[r, cols]
            hi = xt.astype(BF16)
            lo = (xt - hi.astype(F32)).astype(BF16)
            partner = _dot(hi, perm_ref[...]) + _dot(lo, perm_ref[...])
            cos = jnp.concatenate([sc_ref[r, :]] * (MXU_WIDTH // LANES), axis=-1)
            sin = jnp.concatenate([ss_ref[r, :]] * (MXU_WIDTH // LANES), axis=-1)
            o_ref[r, cols] = ((xt * cos + partner * sin) * scale).astype(BF16)

    @pl.when(is_ret)
    def _():
        scale = jnp.where(j >= blk(K_R), RET_DK ** -0.5, 1.0).astype(F32)
        cos, sin = rc_ref[...] * scale, rs_ref[...] * scale
        half = RET_DK // 2
        for h in range(tn // RET_DK):
            a1 = acc_ref[:, h * RET_DK:h * RET_DK + half]
            a2 = acc_ref[:, h * RET_DK + half:(h + 1) * RET_DK]
            o_ref[:, h * RET_DK:h * RET_DK + half] = (a1 * cos - a2 * sin).astype(BF16)
            o_ref[:, h * RET_DK + half:(h + 1) * RET_DK] = (a2 * cos + a1 * sin).astype(BF16)

    @pl.when(is_qs)
    def _():
        for pair in range(tn // MXU_WIDTH):
            swa_rope(pair, SWA_HD ** -0.5)

    @pl.when(is_kvs)
    def _():
        for pair in range(IN_SPLITS[K_S] // MXU_WIDTH):
            swa_rope(pair, 1.0)


def _rope_partner_matrix():
    half = ROPE_DIM // 2
    p = np.zeros((MXU_WIDTH, MXU_WIDTH), np.float32)
    for lane in range(MXU_WIDTH):
        d = lane % SWA_HD
        if d < half:
            p[lane + half, lane] = 1.0
        elif d < ROPE_DIM:
            p[lane - half, lane] = 1.0
    return jnp.asarray(p, BF16)


def _in_proj(x2d, g, w, tables, cast_weights):
    tm, tn = PROJ_TM, PROJ_TN
    assert all(IN_OFFSETS[s] % tn == 0 for s in (Q_R, K_R, V_R, Q_S, K_S))
    assert IN_OFFSETS[V_S] == IN_OFFSETS[K_S] + IN_SPLITS[K_S] and IN_SPLITS[K_S] + IN_SPLITS[V_S] == tn
    tab = pl.BlockSpec((tm, LANES), lambda i, j: (i, 0))
    n_i, n_j = SEQ // tm, D_IN // tn
    assert n_i * CAST_STEPS_PER_ROW == CAST_ROW_BLOCKS * CAST_COL_BLOCKS and CAST_STEPS_PER_ROW <= n_j

    def cast_spec(shape):
        rows, cols = shape[0] // CAST_ROW_BLOCKS, shape[1] // CAST_COL_BLOCKS
        assert rows % BF16_SUBLANES == 0 and cols % LANES == 0 and rows * CAST_ROW_BLOCKS == shape[0]

        def index(i, j):
            b = i * CAST_STEPS_PER_ROW + jnp.minimum(j, CAST_STEPS_PER_ROW - 1)
            return b // CAST_COL_BLOCKS, b % CAST_COL_BLOCKS
        return pl.BlockSpec((rows, cols), index)

    cast_specs = [cast_spec(cw.shape) for cw in cast_weights]
    return pl.pallas_call(
        _in_proj_kernel,
        grid=(SEQ // tm, D_IN // tn),
        in_specs=[pl.BlockSpec(memory_space=pl.ANY),
                  pl.BlockSpec((1, D_MODEL), lambda i, j: (0, 0)),
                  pl.BlockSpec((D_MODEL, tn), lambda i, j: (0, j)),
                  tab, tab, tab, tab, _resident((MXU_WIDTH, MXU_WIDTH)), *cast_specs],
        out_specs=[pl.BlockSpec((tm, tn), lambda i, j: (i, j)), *cast_specs],
        out_shape=[jax.ShapeDtypeStruct((SEQ, D_IN), BF16)]
                  + [jax.ShapeDtypeStruct(cw.shape, BF16) for cw in cast_weights],
        scratch_shapes=[pltpu.VMEM((tm, D_MODEL), BF16), pltpu.VMEM((tm, tn), F32),
                        pltpu.VMEM((tm, D_MODEL), F32), pltpu.SemaphoreType.DMA(())],
        compiler_params=_params("arbitrary", "arbitrary"),
        name="in_proj",
    )(x2d, g.reshape(1, D_MODEL), w, *tables, _rope_partner_matrix(), *cast_weights)


def _retention_kernel(logg_ref, q_ref, k_ref, v_ref, g_ref, o_ref, state_ref, dmask_ref):
    c_rows = RET_CHUNK
    idx = lax.broadcasted_iota(jnp.int32, (c_rows, 1), 0).astype(F32)

    @pl.when(pl.program_id(0) == 0)
    def _():
        state_ref[...] = jnp.zeros_like(state_ref)
        i = lax.broadcasted_iota(jnp.int32, (c_rows, c_rows), 0)
        j = lax.broadcasted_iota(jnp.int32, (c_rows, c_rows), 1)
        rel = (i - j).astype(F32)
        for h in range(RET_HEADS):
            dmask_ref[h] = jnp.where(rel >= 0, jnp.exp(logg_ref[h] * jnp.maximum(rel, 0.0)), 0.0)

    for h in range(RET_HEADS):
        lg = logg_ref[h]
        cols = slice(h * RET_DK, (h + 1) * RET_DK)
        q_dec = jnp.exp(lg * (idx + 1.0))
        k_dec = jnp.exp(lg * (c_rows - 1.0 - idx))
        chunk_decay = jnp.exp(jnp.full((1, RET_DV), lg * c_rows, F32))
        for c in range(RET_ROWS // c_rows):
            r = slice(c * c_rows, (c + 1) * c_rows)
            q, k, v = q_ref[r, cols], k_ref[r, cols], v_ref[r, cols]
            scores = _dot_nt(q, k) * dmask_ref[h]
            inner = _dot(scores.astype(BF16), v)
            state = state_ref[h]
            cross = _dot((q.astype(F32) * q_dec).astype(BF16), state.astype(BF16))
            state_ref[h] = state * chunk_decay + _dot_tn((k.astype(F32) * k_dec).astype(BF16), v)
            y = _rms(inner + cross)
            g = g_ref[r, cols].astype(F32)
            o_ref[r, cols] = (y * (g * _sigmoid(g))).astype(o_ref.dtype)


def _retention(proj):
    log_g = jnp.log(1.0 - jnp.power(2.0, -5.0 - jnp.arange(RET_HEADS, dtype=F32)))
    rows, width = RET_ROWS, RET_HEADS * RET_DK
    assert RET_DK == RET_DV
    all_heads = lambda split: pl.BlockSpec((rows, width), lambda n, c=IN_OFFSETS[split] // width: (n, c))
    assert all(IN_OFFSETS[s] % width == 0 for s in (Q_R, K_R, V_R, G_R))
    return pl.pallas_call(
        _retention_kernel,
        grid=(SEQ // rows,),
        in_specs=[pl.BlockSpec(memory_space=pltpu.SMEM),
                  all_heads(Q_R), all_heads(K_R), all_heads(V_R), all_heads(G_R)],
        out_specs=pl.BlockSpec((rows, width), lambda n: (n, 0)),
        out_shape=jax.ShapeDtypeStruct((SEQ, width), BF16),
        scratch_shapes=[pltpu.VMEM((RET_HEADS, RET_DK, RET_DV), F32),
                        pltpu.VMEM((RET_HEADS, RET_CHUNK, RET_CHUNK), F32)],
        compiler_params=_params("arbitrary"),
        name="retention",
    )(log_g, proj, proj, proj, proj)


def _swa_kernel(sinks_ref, q_ref, k_ref, v_ref, kp_ref, vp_ref, o_ref):
    w = WINDOW
    hd = SWA_HD
    group = SWA_HEADS // SWA_KV_HEADS
    step = pl.program_id(0)
    low = lax.broadcasted_iota(jnp.int32, (w, LANES), 1) < hd

    def dup_halves(x):
        xf = x.astype(F32)
        sw = pltpu.roll(xf, hd, 1)
        lo = lax.broadcasted_iota(jnp.int32, xf.shape, 1) < hd
        return jnp.where(lo, xf, sw).astype(BF16), jnp.where(lo, sw, xf).astype(BF16)

    def per_kv_head(cur_ref, prev_ref):
        allrows = jnp.concatenate([prev_ref[...], cur_ref[...]], axis=0)
        heads = []
        for c in range(allrows.shape[1] // LANES):
            heads.extend(dup_halves(allrows[:, c * LANES:(c + 1) * LANES]))
        return heads

    k2 = per_kv_head(k_ref, kp_ref)
    v2 = per_kv_head(v_ref, vp_ref)

    qi = lax.broadcasted_iota(jnp.int32, (w, 2 * w), 0)
    kj = lax.broadcasted_iota(jnp.int32, (w, 2 * w), 1)
    dist = qi + w - kj
    band = (dist >= 0) & (dist < WINDOW)
    zero = jnp.zeros((w, LANES), BF16)
    ones = jnp.ones((2 * w, LANES), BF16)
    zero_kv = jnp.zeros((2 * w, LANES), BF16)

    tile = lambda a: jnp.concatenate([a] * group, axis=0)
    bias_cur = tile(jnp.where(band[:, w:], 0.0, NEG))
    bias_prev = tile(jnp.where(band[:, :w], 0.0, NEG))
    sink_slot = lax.broadcasted_iota(jnp.int32, (group * w, LANES), 1) == 0
    first_row = lax.broadcasted_iota(jnp.int32, (2 * w, LANES), 0) == 0

    for kh in range(SWA_KV_HEADS):
        cols = [slice((kh * group // 2 + p) * LANES, (kh * group // 2 + p + 1) * LANES)
                for p in range(group // 2)]
        sink = jnp.concatenate([jnp.full((w, LANES), sinks_ref[kh * group + g], F32)
                                for g in range(group)], axis=0)
        bias_later = jnp.concatenate([jnp.where(sink_slot, sink, bias_prev), bias_cur], axis=-1)
        bias_first = jnp.concatenate(
            [jnp.where(sink_slot, sink, jnp.where(step > 0, bias_prev, NEG)), bias_cur], axis=-1)
        for t in range(SWA_ROWS // w):
            rows = slice(t * w, (t + 1) * w)
            bias = bias_first if t == 0 else bias_later
            k_w = jnp.where(first_row, zero_kv, k2[kh][t * w:(t + 2) * w])
            v_w = jnp.where(first_row, zero_kv, v2[kh][t * w:(t + 2) * w])
            parts = []
            for c in cols:
                qc = q_ref[rows, c]
                parts += [jnp.where(low, qc, zero), jnp.where(low, zero, qc)]
            s = _dot_nt(jnp.concatenate(parts, axis=0), k_w) + bias
            e = jnp.exp(s - jnp.max(s, axis=-1, keepdims=True))
            pv = _dot(e.astype(BF16), jnp.concatenate([v_w, ones], axis=-1))
            o = pv[:, :LANES] / pv[:, LANES:]
            for p, c in enumerate(cols):
                o_ref[rows, c] = jnp.where(low, o[2 * p * w:(2 * p + 1) * w],
                                           o[(2 * p + 1) * w:(2 * p + 2) * w]).astype(o_ref.dtype)


def _swa(proj, sinks):
    w, rows = WINDOW, SWA_ROWS
    assert 2 * SWA_HD == LANES and WINDOW == LANES and (SWA_HEADS // SWA_KV_HEADS) % 2 == 0
    nq, nkv = SWA_HEADS * SWA_HD, SWA_KV_HEADS * SWA_HD
    per = rows // w
    cur = lambda split: pl.BlockSpec((rows, nkv), lambda i, c=IN_OFFSETS[split] // nkv: (i, c))
    prev = lambda split: pl.BlockSpec(
        (w, nkv), lambda i, c=IN_OFFSETS[split] // nkv: (jnp.maximum(i * per - 1, 0), c))
    return pl.pallas_call(
        _swa_kernel,
        grid=(SEQ // rows,),
        in_specs=[pl.BlockSpec(memory_space=pltpu.SMEM),
                  pl.BlockSpec((rows, nq), lambda i: (i, IN_OFFSETS[Q_S] // nq)),
                  cur(K_S), cur(V_S), prev(K_S), prev(V_S)],
        out_specs=pl.BlockSpec((rows, nq), lambda i: (i, 0)),
        out_shape=jax.ShapeDtypeStruct((SEQ, nq), BF16),
        compiler_params=_params("parallel"),
        name="swa",
    )(sinks, proj, proj, proj, proj, proj)


def _mem_kv_kernel(mem_ref, g_ref, w_ref, o_ref):
    memn = (_rms(mem_ref[...]) * g_ref[...]).astype(BF16)
    o_ref[...] = _dot(memn, w_ref[...]).astype(o_ref.dtype)


def _mem_kv(mem2d, g, w):
    n_out = 2 * X_HEADS * X_HD
    return pl.pallas_call(
        _mem_kv_kernel,
        out_shape=jax.ShapeDtypeStruct((N_MEM, n_out), BF16),
        compiler_params=pltpu.CompilerParams(vmem_limit_bytes=VMEM_LIMIT),
        name="mem_kv",
    )(mem2d, g.reshape(1, D_MODEL), w)


def _merge_kernel(x_ref, yr_ref, ys_ref, *refs):
    n_gate = D_MODEL // GATE_BLOCK
    gr_refs, gs_refs = refs[:n_gate], refs[n_gate:2 * n_gate]
    (kv_ref, wur_ref, wus_ref, wo_ref, gx_ref, wxq_ref, wxo_ref, gf_ref, h_ref, n_ref) = refs[2 * n_gate:]
    yr, ys = yr_ref[...], ys_ref[...]
    slabs = []
    for c, (gr_ref, gs_ref) in enumerate(zip(gr_refs, gs_refs)):
        cols = slice(c * GATE_BLOCK, (c + 1) * GATE_BLOCK)
        slabs.append((_sigmoid(gr_ref[...].astype(F32)) * _dot(yr, wur_ref[:, cols])
                      + _sigmoid(gs_ref[...].astype(F32)) * _dot(ys, wus_ref[:, cols])).astype(BF16))
    h1 = x_ref[...] + _dot(jnp.concatenate(slabs, axis=-1), wo_ref[...])

    nx = (_rms(h1) * gx_ref[...]).astype(BF16)
    qx = _dot(nx, wxq_ref[...]).astype(BF16)
    kv = kv_ref[...]
    outs = []
    for h in range(X_HEADS):
        q_h = qx[:, h * X_HD:(h + 1) * X_HD]
        k_h = kv[:, h * X_HD:(h + 1) * X_HD]
        v_h = kv[:, (X_HEADS + h) * X_HD:(X_HEADS + h + 1) * X_HD]
        s = _dot_nt(q_h, k_h) * (X_HD ** -0.5)
        e = jnp.exp(s - jnp.max(s, axis=-1, keepdims=True))
        p = e / jnp.sum(e, axis=-1, keepdims=True)
        outs.append(_dot(p.astype(BF16), v_h))
    att = jnp.concatenate(outs, axis=-1).astype(BF16)
    h2 = h1 + _dot(att, wxo_ref[...])
    h_ref[...] = h2
    n_ref[...] = (_rms(h2) * gf_ref[...]).astype(BF16)


def _merge(x2d, yr, ys, proj, kv, w_up_ret, w_up_swa, w_o, g_x, w_xq, w_xo, g_ffn):
    tm = MERGE_TM
    rows = lambda width, col=0: pl.BlockSpec((tm, width), lambda i, c=col: (i, c))
    n_gate = D_MODEL // GATE_BLOCK
    gate = lambda split: [rows(GATE_BLOCK, IN_OFFSETS[split] // GATE_BLOCK + b) for b in range(n_gate)]
    assert IN_OFFSETS[GATE_R] % GATE_BLOCK == 0 and IN_OFFSETS[GATE_S] % GATE_BLOCK == 0
    n_x = X_HEADS * X_HD
    return pl.pallas_call(
        _merge_kernel,
        grid=(SEQ // tm,),
        in_specs=[rows(D_MODEL), rows(RET_HEADS * RET_DV), rows(SWA_HEADS * SWA_HD),
                  *gate(GATE_R), *gate(GATE_S),
                  _resident((N_MEM, 2 * n_x)),
                  _resident((RET_HEADS * RET_DV, D_MODEL)), _resident((SWA_HEADS * SWA_HD, D_MODEL)),
                  _resident((D_MODEL, D_MODEL)), _resident((1, D_MODEL)),
                  _resident((D_MODEL, n_x)), _resident((n_x, D_MODEL)), _resident((1, D_MODEL))],
        out_specs=[rows(D_MODEL), rows(D_MODEL)],
        out_shape=[jax.ShapeDtypeStruct((SEQ, D_MODEL), F32),
                   jax.ShapeDtypeStruct((SEQ, D_MODEL), BF16)],
        compiler_params=_params("parallel"),
        name="merge_xattn",
    )(x2d, yr, ys, *([proj] * (2 * n_gate)), kv, w_up_ret, w_up_swa, w_o, g_x.reshape(1, D_MODEL),
      w_xq, w_xo, g_ffn.reshape(1, D_MODEL))


def _ffn_kernel(n_ref, h_hbm, wg_ref, wu_ref, wd_ref, gf_ref, o_ref, h_ref, h_sem):
    i, f = pl.program_id(0), pl.program_id(1)
    rows = pl.ds(pl.multiple_of(i * FFN_TM, FFN_TM), FFN_TM)
    residual_copy = pltpu.make_async_copy(h_hbm.at[rows, :], h_ref, h_sem)

    @pl.when(f == 0)
    def _():
        residual_copy.start()
        for c in range(FFN_TM // FFN_SLAB):
            o_ref[c * FFN_SLAB:(c + 1) * FFN_SLAB, :] = jnp.zeros((FFN_SLAB, D_MODEL), F32)

    n2 = n_ref[...]
    ts = []
    for c in range(FFN_TF // FFN_SLAB):
        cols = slice(c * FFN_SLAB, (c + 1) * FFN_SLAB)
        a = _dot(n2, wg_ref[:, cols])
        b = _dot(n2, wu_ref[:, cols])
        ts.append((a * _sigmoid(a) * b).astype(BF16))
    t = jnp.concatenate(ts, axis=-1)
    for c in range(D_MODEL // FFN_TF):
        cols = slice(c * FFN_TF, (c + 1) * FFN_TF)
        o_ref[:, cols] += _dot(t, wd_ref[:, cols])

    @pl.when(f == pl.num_programs(1) - 1)
    def _():
        residual_copy.wait()
        for c in range(FFN_TM // FFN_SLAB):
            r = slice(c * FFN_SLAB, (c + 1) * FFN_SLAB)
            o_ref[r, :] = _rms(h_ref[r, :] + o_ref[r, :]) * gf_ref[...]


def _ffn(n2, h2, w_gate, w_up, w_down, g_final):
    tm, tf = FFN_TM, FFN_TF
    return pl.pallas_call(
        _ffn_kernel,
        grid=(SEQ // tm, D_FF // tf),
        in_specs=[pl.BlockSpec((tm, D_MODEL), lambda i, f: (i, 0)),
                  pl.BlockSpec(memory_space=pl.ANY),
                  pl.BlockSpec((D_MODEL, tf), lambda i, f: (0, f)),
                  pl.BlockSpec((D_MODEL, tf), lambda i, f: (0, f)),
                  pl.BlockSpec((tf, D_MODEL), lambda i, f: (f, 0)),
                  pl.BlockSpec((1, D_MODEL), lambda i, f: (0, 0))],
        out_specs=pl.BlockSpec((tm, D_MODEL), lambda i, f: (i, 0)),
        out_shape=jax.ShapeDtypeStruct((SEQ, D_MODEL), F32),
        scratch_shapes=[pltpu.VMEM((tm, D_MODEL), F32), pltpu.SemaphoreType.DMA(())],
        compiler_params=_params("arbitrary", "arbitrary"),
        name="ffn",
    )(n2, h2, w_gate, w_up, w_down, g_final.reshape(1, D_MODEL))


def kernel(x, mem, positions, g_mix, w_in, w_up_ret, w_up_swa, sinks, w_o, g_x, g_mem,
           w_xq, w_xkv, w_xo, g_ffn, w_ffn_gate, w_ffn_up, w_ffn_down, g_final):
    assert x.shape == (1, SEQ, D_MODEL) and mem.shape == (1, N_MEM, D_MODEL)
    assert w_in.shape == (1, D_MODEL, D_IN)
    x2d = x.reshape(SEQ, D_MODEL)
    tables, (w_q, w_xout, w_kv) = _rope_tables(positions, (w_xq[0], w_xo[0], w_xkv[0]))
    later_weights = (w_ffn_gate, w_ffn_up, w_ffn_down, w_up_ret, w_up_swa, w_o)
    proj, *bf16_weights = _in_proj(x2d, g_mix[0], w_in[0], tables, tuple(w[0] for w in later_weights))
    w_gate, w_up, w_down, w_ur, w_us, w_out = bf16_weights
    yr = _retention(proj)
    ys = _swa(proj, sinks[0])
    kv = _mem_kv(mem.reshape(N_MEM, D_MODEL), g_mem[0], w_kv)
    h2, n2 = _merge(x2d, yr, ys, proj, kv, w_ur, w_us, w_out, g_x[0], w_q, w_xout, g_ffn[0])
    out = _ffn(n2, h2, w_gate, w_up, w_down, g_final)
    return out.reshape(1, SEQ, D_MODEL)
```
